```python
import math
import jax, jax.numpy as jnp
from jax import lax
import numpy as np

D_MODEL = 1024
BATCH = 8
SEQ = 2048
DEPTH = 1
DEC_BATCH = 8
DEC_SEQ = 16
PAST_LEN = 2048

CHUNK = 64
N_HEADS = 16
N_KV_HEADS = 2
HEAD_DIM = 64
GROUP = N_HEADS // N_KV_HEADS
ATT_WIDTH = N_HEADS * HEAD_DIM
KV_WIDTH = N_KV_HEADS * HEAD_DIM
WINDOW = 128
WINDOW_CHUNKS = WINDOW // CHUNK
ATTN_SCALE = HEAD_DIM ** -0.5
N_BUCKETS = 32
MAX_DISTANCE = 128
HG_EXPAND = 128
HG_HEADS = D_MODEL // HG_EXPAND
HG_DK = HG_EXPAND
HG_DV = D_MODEL // HG_HEADS
HG_WIDTH = HG_HEADS * HG_DK
HG_VWIDTH = HG_HEADS * HG_DV
D_FF = -(-8 * D_MODEL // (3 * 256)) * 256
IN_WIDTH = ATT_WIDTH + 2 * KV_WIDTH + 2 * HG_WIDTH + 2 * HG_VWIDTH + 2 * D_MODEL
RMS_EPS = 1e-6

kernel_name = "hybrid_swa_hgrn2_streaming_step"


def _rmsnorm(x, w):
    xf = x.astype(jnp.float32)
    y = xf * lax.rsqrt(jnp.mean(xf * xf, axis=-1, keepdims=True) + RMS_EPS) * w.astype(jnp.float32)
    return y.astype(x.dtype)


def _split_proj(proj):
    sizes = (ATT_WIDTH, KV_WIDTH, KV_WIDTH, HG_WIDTH, HG_WIDTH, HG_VWIDTH, HG_VWIDTH, D_MODEL, D_MODEL)
    idx = np.cumsum(sizes)[:-1].tolist()
    return jnp.split(proj, idx, axis=-1)


def _t5_bucket(rel):
    nb = N_BUCKETS // 2
    max_exact = nb // 2
    ret = jnp.where(rel > 0, nb, 0)
    n = jnp.abs(rel)
    nf = jnp.maximum(n, 1).astype(jnp.float32)
    large = max_exact + (jnp.log(nf / max_exact) / math.log(MAX_DISTANCE / max_exact)
                         * (nb - max_exact)).astype(jnp.int32)
    large = jnp.minimum(large, nb - 1)
    return ret + jnp.where(n < max_exact, n, large)


def _rel_bias(table, cq, lk):
    rel = jnp.arange(lk, dtype=jnp.int32)[None, :] - WINDOW - jnp.arange(cq, dtype=jnp.int32)[:, None]
    b = table[_t5_bucket(rel)]
    return b.transpose(2, 0, 1).reshape(N_KV_HEADS, GROUP, cq, lk).astype(jnp.float32)


def _attend(q, k, v, sinks, bias, valid):
    s = jnp.einsum('bnqkgd,bnlkd->bnkgql', q, k).astype(jnp.float32) * ATTN_SCALE + bias
    s = jnp.where(valid[None, :, None, None], s, -jnp.inf)
    sink = sinks.astype(jnp.float32).reshape(N_KV_HEADS, GROUP)[None, None, :, :, None]
    m = jnp.maximum(jnp.max(s, axis=-1), sink)
    p = jnp.exp(s - m[..., None])
    denom = jnp.sum(p, axis=-1) + jnp.exp(sink - m)
    p = (p / denom[..., None]).astype(v.dtype)
    return jnp.einsum('bnkgql,bnlkd->bnqkgd', p, v)


def _hgrn_gates(hq, hf, hi, lb):
    b, t, _ = hq.shape
    f = lb + (1.0 - lb) * jax.nn.sigmoid(hf.astype(jnp.float32))

    def heads(a, d):
        return a.reshape(b, t, HG_HEADS, d).transpose(0, 2, 1, 3)

    return (heads(hq.astype(jnp.float32), HG_DK), heads(1.0 - f, HG_DK),
            heads(hi.astype(jnp.float32), HG_DV), heads(jnp.log(f), HG_DK))


def _hgrn_block(s0, q, k, v, logf):
    c = q.shape[2]
    bcum = jnp.cumsum(logf, axis=2)
    inter = jnp.einsum('bhtd,bhde->bhte', q * jnp.exp(bcum), s0)
    causal = (jnp.arange(c)[:, None] >= jnp.arange(c)[None, :])[None, None, :, :, None]
    decay = jnp.exp(jnp.where(causal, bcum[:, :, :, None, :] - bcum[:, :, None, :, :], -jnp.inf))
    a = jnp.einsum('bhtd,bhtsd,bhsd->bhts', q, decay, k)
    intra = jnp.einsum('bhts,bhse->bhte', a, v)
    blast = bcum[:, :, -1:, :]
    s_new = jnp.exp(blast[:, :, 0, :])[..., None] * s0 + jnp.einsum(
        'bhsd,bhse->bhde', k * jnp.exp(blast - bcum), v)
    return s_new, inter + intra


def _hgrn_readout(o, og, gnorm_w):
    b, _, t, _ = o.shape
    o = o.transpose(0, 2, 1, 3)
    o = o * lax.rsqrt(jnp.mean(o * o, axis=-1, keepdims=True) + RMS_EPS) * gnorm_w.astype(jnp.float32)
    o = o * jax.nn.silu(og.astype(jnp.float32).reshape(b, t, HG_HEADS, HG_DV))
    return o.reshape(b, t, HG_VWIDTH).astype(og.dtype)


def _merge(att, hg, ga, gh, w_out):
    return (jax.nn.sigmoid(ga) * att + jax.nn.sigmoid(gh) * hg) @ w_out


def _ffn(x, norm_w, w_gate_up, w_down):
    h = _rmsnorm(x, norm_w)
    g, u = jnp.split(h @ w_gate_up, 2, axis=-1)
    return (jax.nn.silu(g) * u) @ w_down


def _mixer_prompt(x, norm_w, w_in, w_out, sinks, rel_table, lb, gnorm_w):
    b, s, _ = x.shape
    n = s // CHUNK
    h = _rmsnorm(x, norm_w)
    q, k, v, hq, hf, hi, hog, ga, gh = _split_proj(h @ w_in)
    k = k.reshape(b, s, N_KV_HEADS, HEAD_DIM)
    v = v.reshape(b, s, N_KV_HEADS, HEAD_DIM)
    pad = ((0, 0), (WINDOW, 0), (0, 0), (0, 0))
    kp = jnp.pad(k, pad).reshape(b, n + WINDOW_CHUNKS, CHUNK, N_KV_HEADS, HEAD_DIM)
    vp = jnp.pad(v, pad).reshape(b, n + WINDOW_CHUNKS, CHUNK, N_KV_HEADS, HEAD_DIM)
    kw = jnp.concatenate([kp[:, j:j + n] for j in range(WINDOW_CHUNKS + 1)], axis=2)
    vw = jnp.concatenate([vp[:, j:j + n] for j in range(WINDOW_CHUNKS + 1)], axis=2)
    key_chunk = (jnp.arange(n)[:, None, None] - WINDOW_CHUNKS
                 + (jnp.arange(WINDOW + CHUNK) // CHUNK)[None, None, :])
    valid = key_chunk >= 0
    att = _attend(q.reshape(b, n, CHUNK, N_KV_HEADS, GROUP, HEAD_DIM), kw, vw, sinks,
                  _rel_bias(rel_table, CHUNK, WINDOW + CHUNK), valid).reshape(b, s, ATT_WIDTH)
    qh, kh, vh, lf = _hgrn_gates(hq, hf, hi, lb)

    def to_blocks(a):
        return a.reshape(b, HG_HEADS, n, CHUNK, a.shape[-1]).transpose(2, 0, 1, 3, 4)

    s0 = jnp.zeros((b, HG_HEADS, HG_DK, HG_DV), jnp.float32)
    s_fin, o = lax.scan(lambda st, blk: _hgrn_block(st, *blk), s0,
                        (to_blocks(qh), to_blocks(kh), to_blocks(vh), to_blocks(lf)))
    o = o.transpose(1, 2, 0, 3, 4).reshape(b, HG_HEADS, s, HG_DV)
    hg = _hgrn_readout(o, hog, gnorm_w)
    y = x + _merge(att, hg, ga, gh, w_out)
    return y, k[:, s - WINDOW:], v[:, s - WINDOW:], s_fin.astype(x.dtype)


def _mixer_sample(x, cache_k, cache_v, state, norm_w, w_in, w_out, sinks, rel_table, lb, gnorm_w):
    b, t, _ = x.shape
    h = _rmsnorm(x, norm_w)
    q, k, v, hq, hf, hi, hog, ga, gh = _split_proj(h @ w_in)
    k = k.reshape(b, t, N_KV_HEADS, HEAD_DIM)
    v = v.reshape(b, t, N_KV_HEADS, HEAD_DIM)
    k_all = jnp.concatenate([cache_k.astype(k.dtype), k], axis=1)
    v_all = jnp.concatenate([cache_v.astype(v.dtype), v], axis=1)
    valid = jnp.ones((1, 1, WINDOW + t), dtype=bool)
    att = _attend(q.reshape(b, 1, t, N_KV_HEADS, GROUP, HEAD_DIM), k_all[:, None], v_all[:, None],
                  sinks, _rel_bias(rel_table, t, WINDOW + t), valid).reshape(b, t, ATT_WIDTH)
    s_new, o = _hgrn_block(state.astype(jnp.float32), *_hgrn_gates(hq, hf, hi, lb))
    hg = _hgrn_readout(o, hog, gnorm_w)
    y = x + _merge(att, hg, ga, gh, w_out)
    return y, k_all[:, t:], v_all[:, t:], s_new.astype(x.dtype)


def setup_inputs(seed: int = 0) -> dict:
    key = jax.random.key(seed)
    ks = jax.random.split(key, 20)
    f32 = jnp.float32

    def nrm(k, shape, scale):
        return jax.random.normal(k, shape, f32) * scale

    return {
        "x_prompt": nrm(ks[0], (BATCH, SEQ, D_MODEL), 1.0),
        "x_sample": nrm(ks[1], (DEC_BATCH, DEC_SEQ, D_MODEL), 1.0),
        "cache_k": nrm(ks[2], (DEPTH, DEC_BATCH, WINDOW, N_KV_HEADS, HEAD_DIM), 1.0),
        "cache_v": nrm(ks[3], (DEPTH, DEC_BATCH, WINDOW, N_KV_HEADS, HEAD_DIM), 1.0),
        "state_hgrn": nrm(ks[4], (DEPTH, DEC_BATCH, HG_HEADS, HG_DK, HG_DV), 0.3),
        "norm_mix": 1.0 + nrm(ks[5], (DEPTH, D_MODEL), 0.01),
        "w_in": nrm(ks[6], (DEPTH, D_MODEL, IN_WIDTH), D_MODEL ** -0.5),
        "w_out": nrm(ks[7], (DEPTH, D_MODEL, D_MODEL), D_MODEL ** -0.5),
        "attn_sinks": nrm(ks[8], (DEPTH, N_HEADS), 0.5),
        "rel_bias_table": nrm(ks[9], (N_BUCKETS, N_HEADS), 0.1),
        "hgrn_lb": nrm(ks[10], (DEPTH + 1, HG_WIDTH), 0.1),
        "hgrn_norm": 1.0 + nrm(ks[11], (DEPTH, HG_DV), 0.01),
        "norm_ffn": 1.0 + nrm(ks[12], (DEPTH, D_MODEL), 0.01),
        "w_gate_up": nrm(ks[13], (DEPTH, D_MODEL, 2 * D_FF), D_MODEL ** -0.5),
        "w_down": nrm(ks[14], (DEPTH, D_FF, D_MODEL), D_FF ** -0.5),
        "norm_final": 1.0 + nrm(ks[15], (D_MODEL,), 0.01),
    }


def reference(x_prompt, x_sample, cache_k, cache_v, state_hgrn, norm_mix, w_in, w_out, attn_sinks,
              rel_bias_table, hgrn_lb, hgrn_norm, norm_ffn, w_gate_up, w_down, norm_final):
    lb_all = jnp.cumsum(jax.nn.softmax(hgrn_lb.astype(jnp.float32), axis=0), axis=0)
    xp, xs = x_prompt, x_sample
    nkp, nvp, nsp, nks, nvs, nss = [], [], [], [], [], []
    for l in range(DEPTH):
        xp, kp_, vp_, sp_ = _mixer_prompt(xp, norm_mix[l], w_in[l], w_out[l], attn_sinks[l],
                                          rel_bias_table, lb_all[l], hgrn_norm[l])
        xs, ks_, vs_, ss_ = _mixer_sample(xs, cache_k[l], cache_v[l], state_hgrn[l], norm_mix[l], w_in[l],
                                          w_out[l], attn_sinks[l], rel_bias_table, lb_all[l], hgrn_norm[l])
        xp = xp + _ffn(xp, norm_ffn[l], w_gate_up[l], w_down[l])
        xs = xs + _ffn(xs, norm_ffn[l], w_gate_up[l], w_down[l])
        nkp.append(kp_); nvp.append(vp_); nsp.append(sp_)
        nks.append(ks_); nvs.append(vs_); nss.append(ss_)
    y_prompt = _rmsnorm(xp, norm_final)
    y_sample = _rmsnorm(xs, norm_final)
    return (y_prompt, y_sample, jnp.stack(nkp), jnp.stack(nvp), jnp.stack(nsp),
            jnp.stack(nks), jnp.stack(nvs), jnp.stack(nss))
```

```python
import functools
import math

import numpy as np
import jax
import jax.numpy as jnp
from jax import lax
from jax.experimental import pallas as pl
from jax.experimental.pallas import tpu as pltpu

CHUNK = 64
N_HEADS = 16
N_KV_HEADS = 2
HEAD_DIM = 64
GROUP = N_HEADS // N_KV_HEADS
KV_WIDTH = N_KV_HEADS * HEAD_DIM
WINDOW = 128
ATTN_SCALE = HEAD_DIM ** -0.5
N_BUCKETS = 32
MAX_DISTANCE = 128
HG_DK = 128
RMS_EPS = 1e-6

LANES = 128
VMEM_LIMIT_BYTES = 56 * 1024 * 1024

_NT = (((1,), (1,)), ((), ()))
_TN = (((0,), (0,)), ((), ()))


def _bf(x):
    return x.astype(jnp.bfloat16)


def _dot(a, b, dims=None):
    if dims is None:
        return jnp.dot(a, b, preferred_element_type=jnp.float32)
    return lax.dot_general(a, b, dims, preferred_element_type=jnp.float32)


def _rms(x, w):
    return x * lax.rsqrt(jnp.mean(x * x, axis=-1, keepdims=True) + RMS_EPS) * w


def _t5_bucket(rel):
    nb = N_BUCKETS // 2
    max_exact = nb // 2
    ret = jnp.where(rel > 0, nb, 0)
    n = jnp.abs(rel)
    nf = jnp.maximum(n, 1).astype(jnp.float32)
    large = max_exact + (jnp.log(nf / max_exact) / math.log(MAX_DISTANCE / max_exact)
                         * (nb - max_exact)).astype(jnp.int32)
    large = jnp.minimum(large, nb - 1)
    return ret + jnp.where(n < max_exact, n, large)


def _levels(c):
    out, m = [], c // 2
    while m >= 1:
        out.append(m)
        m //= 2
    return out


def _level_masks(c):
    t = np.arange(c)[:, None]
    s = np.arange(c)[None, :]
    masks = []
    for m in _levels(c):
        masks.append((t // (2 * m) == s // (2 * m)) & ((t // m) % 2 == 1) & ((s // m) % 2 == 0))
    masks.append(t == s)
    return np.stack(masks).astype(np.float32)


def _block_tri(t, c):
    i = np.arange(t)[:, None]
    j = np.arange(t)[None, :]
    return ((i // c == j // c) & (j <= i)).astype(np.float32)


def _mixer_kernel(*refs, tile, chunk, carry, d_model, layer):
    n_chunks = tile // chunk
    hist_len = WINDOW
    keys = hist_len + chunk
    levels = _levels(chunk)
    n_hg = d_model // HG_DK
    att_w = N_HEADS * HEAD_DIM

    refs = list(refs)
    (x_ref, nw_ref, win_ref, wout_ref, sinks_ref, table_ref, bucket_ref, lbp_ref, gn_ref,
     tri_ref, lmask_ref) = refs[:11]
    refs = refs[11:]
    if carry:
        y_ref, ko_ref, vo_ref, so_ref = refs[:4]
        refs = refs[4:]
    else:
        hk_ref, hv_ref, si_ref, y_ref, ko_ref, vo_ref, so_ref = refs[:7]
        refs = refs[7:]
    (bias_s, q_s, kv_s, hq_s, kk_s, hi_s, lf_s, bc_s, og_s, ga_s, gh_s, att_s, hg_s) = refs[:13]
    refs = refs[13:]
    if carry:
        kh_s, vh_s, st_s = refs

    first = (pl.program_id(0) == 0) & (pl.program_id(1) == 0)
    t_idx = pl.program_id(1)

    @pl.when(first)
    def _build_bias():
        bucket = bucket_ref[...]
        for h in range(N_HEADS):
            acc = jnp.zeros(bucket.shape, jnp.float32)
            for b in range(N_BUCKETS):
                acc = jnp.where(bucket == b, table_ref[b, h], acc)
            bias_s[h] = acc

    if carry:
        @pl.when(t_idx == 0)
        def _reset():
            kh_s[...] = jnp.zeros_like(kh_s)
            vh_s[...] = jnp.zeros_like(vh_s)
            st_s[...] = jnp.zeros_like(st_s)

    x = x_ref[0]
    h = _bf(_rms(x, nw_ref[...]))

    def proj(i0, width):
        return _dot(h, win_ref[:, i0:i0 + width])

    off = 0
    q_s[...] = _bf(proj(off, att_w) * ATTN_SCALE)
    off += att_w
    kv_s[...] = proj(off, 2 * KV_WIDTH)
    off += 2 * KV_WIDTH
    hq_s[...] = proj(off, d_model)
    off += d_model
    lbp = lbp_ref[...]
    e = jnp.exp(lbp - jnp.max(lbp, axis=0, keepdims=True))
    lb = jnp.sum(e[:layer + 1], axis=0, keepdims=True) / jnp.sum(e, axis=0, keepdims=True)
    f = lb + (1.0 - lb) * jax.nn.sigmoid(proj(off, d_model))
    off += d_model
    kk_s[...] = 1.0 - f
    lf = jnp.log(f)
    lf_s[...] = lf
    l_hi = _bf(lf)
    r1 = lf - l_hi.astype(jnp.float32)
    l_mid = _bf(r1)
    l_lo = _bf(r1 - l_mid.astype(jnp.float32))
    tri = tri_ref[...]
    bc_s[...] = _dot(tri, l_hi) + _dot(tri, l_mid) + _dot(tri, l_lo)
    hi_s[...] = proj(off, d_model)
    off += d_model
    og_s[...] = jax.nn.silu(proj(off, d_model))
    off += d_model
    ga_s[...] = jax.nn.sigmoid(proj(off, d_model))
    off += d_model
    gh_s[...] = jax.nn.sigmoid(proj(off, d_model))

    lane_k = lax.broadcasted_iota(jnp.int32, (keys, LANES), 1)
    lo_k = lane_k < HEAD_DIM
    row_c = lax.broadcasted_iota(jnp.int32, (chunk, HG_DK), 0)
    col_keys = lax.broadcasted_iota(jnp.int32, (chunk, keys), 1)
    gn = gn_ref[...]

    def chunk_body(c, _):
        r0 = pl.multiple_of(c * chunk, chunk)
        rows = pl.ds(r0, chunk)

        if carry:
            k_hist, v_hist = kh_s[...], vh_s[...]
        else:
            k_hist, v_hist = hk_ref[c], hv_ref[c]
        k_full = jnp.concatenate([k_hist, kv_s[rows, 0:KV_WIDTH]], axis=0)
        v_full = jnp.concatenate([v_hist, kv_s[rows, KV_WIDTH:2 * KV_WIDTH]], axis=0)
        k_rot = pltpu.roll(k_full, HEAD_DIM, axis=1)
        v_rot = pltpu.roll(v_full, HEAD_DIM, axis=1)
        k_a = [_bf(jnp.where(lo_k, k_full, 0.0)), _bf(jnp.where(lo_k, k_rot, 0.0))]
        k_b = [_bf(jnp.where(lo_k, 0.0, k_rot)), _bf(jnp.where(lo_k, 0.0, k_full))]
        v_a = [_bf(jnp.where(lo_k, v_full, 0.0)), _bf(jnp.where(lo_k, v_rot, 0.0))]
        v_b = [_bf(jnp.where(lo_k, 0.0, v_rot)), _bf(jnp.where(lo_k, 0.0, v_full))]
        if carry:
            n_valid = jnp.minimum((t_idx * n_chunks + c) * chunk, hist_len)
            valid = col_keys >= hist_len - n_valid

        def softmax(s, head):
            s = s + bias_s[head]
            if carry:
                s = jnp.where(valid, s, -jnp.inf)
            sink = sinks_ref[head]
            m = jnp.maximum(jnp.max(s, axis=-1, keepdims=True), sink)
            p = jnp.exp(s - m)
            denom = jnp.sum(p, axis=-1, keepdims=True) + jnp.exp(sink - m)
            return _bf(p / denom)

        for j in range(N_HEADS // 2):
            kvh = (2 * j) // GROUP
            qp = q_s[rows, j * LANES:(j + 1) * LANES]
            p_a = softmax(_dot(qp, k_a[kvh], _NT), 2 * j)
            p_b = softmax(_dot(qp, k_b[kvh], _NT), 2 * j + 1)
            att_s[rows, j * LANES:(j + 1) * LANES] = _dot(p_a, v_a[kvh]) + _dot(p_b, v_b[kvh])
        if carry:
            kh_s[...] = k_full[chunk:]
            vh_s[...] = v_full[chunk:]

        for hd in range(n_hg):
            cols = slice(hd * HG_DK, (hd + 1) * HG_DK)
            q = hq_s[rows, cols]
            kk = kk_s[rows, cols]
            v = _bf(hi_s[rows, cols])
            bc = bc_s[rows, cols]
            lfh = lf_s[rows, cols]
            if carry:
                s0t = st_s[hd]
            else:
                s0t = si_ref[c, hd].T
            b_last = bc[chunk - 1:chunk, :]
            inter = _dot(_bf(q * jnp.exp(bc)), _bf(s0t), _NT)
            a = _dot(_bf(q), _bf(kk), _NT) * lmask_ref[len(levels)]
            for li, m in enumerate(levels):
                upper = ((row_c >> int(math.log2(m))) & 1) == 1
                if m == 1:
                    z = jnp.where(upper, q * (1.0 - kk), kk)
                else:
                    if 2 * m >= 8:
                        g = chunk // (2 * m)
                        ref_row = bc.reshape(g, 2 * m, HG_DK)[:, m - 1:m, :]
                        ref = jnp.broadcast_to(ref_row, (g, 2 * m, HG_DK)).reshape(chunk, HG_DK)
                        arg = jnp.where(upper, bc - ref, ref - bc)
                    else:
                        assert m == 2
                        r4 = row_c & 3
                        nxt = pltpu.roll(lfh, chunk - 1, axis=0)
                        prv = pltpu.roll(lfh, 1, axis=0)
                        arg = jnp.where(r4 == 0, nxt, jnp.where(r4 == 1, 0.0, jnp.where(r4 == 2, lfh, lfh + prv)))
                    z = jnp.where(upper, q, kk) * jnp.exp(arg)
                z = _bf(z)
                a = a + _dot(z, z, _NT) * lmask_ref[li]
            o = inter + _dot(_bf(a), v)
            k_dec = _bf(kk * jnp.exp(b_last - bc))
            s_new_t = s0t * jnp.exp(b_last) + _dot(v, k_dec, _TN)
            if carry:
                st_s[hd] = s_new_t
            else:
                so_ref[c, hd] = s_new_t.T
            o = o * lax.rsqrt(jnp.mean(o * o, axis=-1, keepdims=True) + RMS_EPS) * gn
            hg_s[rows, cols] = o * og_s[rows, cols]
        return 0

    lax.fori_loop(0, n_chunks, chunk_body, 0)

    merged = _bf(ga_s[...] * att_s[...] + gh_s[...] * hg_s[...])
    y_ref[0] = x + _dot(merged, wout_ref[...])

    if carry:
        @pl.when(t_idx == pl.num_programs(1) - 1)
        def _emit():
            ko_ref[0] = kv_s[tile - WINDOW:tile, 0:KV_WIDTH]
            vo_ref[0] = kv_s[tile - WINDOW:tile, KV_WIDTH:2 * KV_WIDTH]
            for hd in range(n_hg):
                so_ref[0, hd] = st_s[hd].T
    else:
        ko_ref[0] = kv_s[:, 0:KV_WIDTH]
        vo_ref[0] = kv_s[:, KV_WIDTH:2 * KV_WIDTH]


def _full_spec(shape):
    nd = len(shape)
    return pl.BlockSpec(shape, lambda *_: (0,) * nd)


def _mixer(x, hist, norm_w, w_in, w_out, sinks, table, lb_params, gnorm_w, *, layer, carry, tile, chunk):
    nb, s, d = x.shape
    n_t = s // tile
    keys = WINDOW + chunk
    n_hg = d // HG_DK
    n_streams = nb if carry else s // chunk
    rel = (jnp.arange(keys, dtype=jnp.int32)[None, :] - WINDOW
           - jnp.arange(chunk, dtype=jnp.int32)[:, None])
    bucket = _t5_bucket(rel).astype(jnp.int32)
    tri = jnp.asarray(_block_tri(tile, chunk), jnp.bfloat16)
    lmask = jnp.asarray(_level_masks(chunk))
    smem = pl.BlockSpec(memory_space=pltpu.SMEM)

    in_arrays = [x, norm_w.reshape(1, d), w_in, w_out, sinks, table, bucket, lb_params,
                 gnorm_w.reshape(1, HG_DK), tri, lmask]
    in_specs = [pl.BlockSpec((1, tile, d), lambda b, t: (b, t, 0)), _full_spec((1, d)),
                _full_spec(w_in.shape), _full_spec(w_out.shape), smem, smem, _full_spec(bucket.shape),
                _full_spec(lb_params.shape), _full_spec((1, HG_DK)), _full_spec(tri.shape),
                _full_spec(lmask.shape)]
    if carry:
        kv_rows = WINDOW
        st_block = (1, n_hg, HG_DK, HG_DK)
    else:
        in_arrays += list(hist)
        in_specs += [_full_spec(a.shape) for a in hist]
        kv_rows = s
        st_block = (n_streams, n_hg, HG_DK, HG_DK)
    n_kv_out = nb
    out_shape = [jax.ShapeDtypeStruct((nb, s, d), jnp.float32),
                 jax.ShapeDtypeStruct((n_kv_out, kv_rows, KV_WIDTH), jnp.float32),
                 jax.ShapeDtypeStruct((n_kv_out, kv_rows, KV_WIDTH), jnp.float32),
                 jax.ShapeDtypeStruct((n_streams, n_hg, HG_DK, HG_DK), jnp.float32)]
    out_specs = [pl.BlockSpec((1, tile, d), lambda b, t: (b, t, 0)),
                 pl.BlockSpec((1, kv_rows, KV_WIDTH), lambda b, t: (b, 0, 0)),
                 pl.BlockSpec((1, kv_rows, KV_WIDTH), lambda b, t: (b, 0, 0)),
                 pl.BlockSpec(st_block, lambda b, t: (b, 0, 0, 0))]
    f32 = jnp.float32
    scratch = [pltpu.VMEM((N_HEADS, chunk, keys), f32),
               pltpu.VMEM((tile, N_HEADS * HEAD_DIM), jnp.bfloat16),
               pltpu.VMEM((tile, 2 * KV_WIDTH), f32)]
    scratch += [pltpu.VMEM((tile, d), f32) for _ in range(10)]
    if carry:
        scratch += [pltpu.VMEM((WINDOW, KV_WIDTH), f32), pltpu.VMEM((WINDOW, KV_WIDTH), f32),
                    pltpu.VMEM((n_hg, HG_DK, HG_DK), f32)]
    kern = functools.partial(_mixer_kernel, tile=tile, chunk=chunk, carry=carry, d_model=d, layer=layer)
    return pl.pallas_call(
        kern,
        grid=(nb, n_t),
        in_specs=in_specs,
        out_specs=out_specs,
        out_shape=out_shape,
        scratch_shapes=scratch,
        compiler_params=pltpu.CompilerParams(
            dimension_semantics=("arbitrary", "arbitrary"), vmem_limit_bytes=VMEM_LIMIT_BYTES),
        name="mixer_prompt" if carry else "mixer_sample",
    )(*in_arrays)


def _ffn_kernel(x_ref, nw_ref, wgu_ref, wd_ref, fw_ref, y_ref, act_s, *, d_ff, col_tile, final_norm):
    x = x_ref[...]
    h = _bf(_rms(x, nw_ref[...]))
    for j in range(d_ff // col_tile):
        g = _dot(h, wgu_ref[:, j * col_tile:(j + 1) * col_tile])
        u = _dot(h, wgu_ref[:, d_ff + j * col_tile:d_ff + (j + 1) * col_tile])
        act_s[:, j * col_tile:(j + 1) * col_tile] = _bf(jax.nn.silu(g) * u)
    y = x + _dot(act_s[...], wd_ref[...])
    if final_norm:
        y = _rms(y, fw_ref[...])
    y_ref[...] = y


def _ffn(x, norm_w, w_gate_up, w_down, final_w, *, tile, final_norm):
    n, d = x.shape
    d_ff = w_down.shape[0]
    kern = functools.partial(_ffn_kernel, d_ff=d_ff, col_tile=2 * LANES, final_norm=final_norm)
    return pl.pallas_call(
        kern,
        grid=(n // tile,),
        in_specs=[pl.BlockSpec((tile, d), lambda i: (i, 0)), _full_spec((1, d)),
                  _full_spec(w_gate_up.shape), _full_spec(w_down.shape), _full_spec((1, d))],
        out_specs=pl.BlockSpec((tile, d), lambda i: (i, 0)),
        out_shape=jax.ShapeDtypeStruct((n, d), jnp.float32),
        scratch_shapes=[pltpu.VMEM((tile, d_ff), jnp.bfloat16)],
        compiler_params=pltpu.CompilerParams(
            dimension_semantics=("arbitrary",), vmem_limit_bytes=VMEM_LIMIT_BYTES),
        name="ffn",
    )(x, norm_w.reshape(1, d), w_gate_up, w_down, final_w.reshape(1, d))


def kernel(x_prompt, x_sample, cache_k, cache_v, state_hgrn, norm_mix, w_in, w_out, attn_sinks, rel_bias_table,
           hgrn_lb, hgrn_norm, norm_ffn, w_gate_up, w_down, norm_final):
    depth = w_in.shape[0]
    batch, seq, d = x_prompt.shape
    dec_batch, dec_seq, _ = x_sample.shape
    xp = x_prompt
    xs = x_sample.reshape(1, dec_batch * dec_seq, d)
    outs = [[] for _ in range(6)]
    for l in range(depth):
        w_in_l, w_out_l = _bf(w_in[l]), _bf(w_out[l])
        w_gu_l, w_d_l = _bf(w_gate_up[l]), _bf(w_down[l])
        shared = (norm_mix[l], w_in_l, w_out_l, attn_sinks[l], rel_bias_table, hgrn_lb, hgrn_norm[l])
        xp, kp, vp, sp = _mixer(xp, None, *shared, layer=l, carry=True, tile=4 * CHUNK, chunk=CHUNK)
        hist = (cache_k[l].reshape(dec_batch, WINDOW, KV_WIDTH), cache_v[l].reshape(dec_batch, WINDOW, KV_WIDTH),
                state_hgrn[l])
        xs, ks, vs, ss = _mixer(xs, hist, *shared, layer=l, carry=False, tile=dec_batch * dec_seq, chunk=dec_seq)
        last = l == depth - 1
        xp = _ffn(xp.reshape(batch * seq, d), norm_ffn[l], w_gu_l, w_d_l, norm_final,
                  tile=512, final_norm=last).reshape(batch, seq, d)
        xs = _ffn(xs.reshape(dec_batch * dec_seq, d), norm_ffn[l], w_gu_l, w_d_l, norm_final,
                  tile=dec_batch * dec_seq, final_norm=last).reshape(1, dec_batch * dec_seq, d)
        ks = jnp.concatenate([hist[0][:, dec_seq:], ks.reshape(dec_batch, dec_seq, KV_WIDTH)], axis=1)
        vs = jnp.concatenate([hist[1][:, dec_seq:], vs.reshape(dec_batch, dec_seq, KV_WIDTH)], axis=1)
        kv_shape = (-1, WINDOW, N_KV_HEADS, HEAD_DIM)
        for acc, val in zip(outs, (kp.reshape(kv_shape), vp.reshape(kv_shape), sp,
                                   ks.reshape(kv_shape), vs.reshape(kv_shape), ss)):
            acc.append(val)
    return (xp, xs.reshape(dec_batch, dec_seq, d)) + tuple(jnp.stack(o) for o in outs)
```

```python
import functools
import math

import numpy as np
import jax
import jax.numpy as jnp
from jax import lax
from jax.experimental import pallas as pl
from jax.experimental.pallas import tpu as pltpu

CHUNK = 64
N_HEADS = 16
N_KV_HEADS = 2
HEAD_DIM = 64
GROUP = N_HEADS // N_KV_HEADS
KV_WIDTH = N_KV_HEADS * HEAD_DIM
WINDOW = 128
ATTN_SCALE = HEAD_DIM ** -0.5
N_BUCKETS = 32
MAX_DISTANCE = 128
HG_DK = 128
RMS_EPS = 1e-6

LANES = 128
VMEM_LIMIT_BYTES = 56 * 1024 * 1024

_NT = (((1,), (1,)), ((), ()))
_TN = (((0,), (0,)), ((), ()))


def _bf(x):
    return x.astype(jnp.bfloat16)


def _dot(a, b, dims=None):
    if dims is None:
        return jnp.dot(a, b, preferred_element_type=jnp.float32)
    return lax.dot_general(a, b, dims, preferred_element_type=jnp.float32)


def _rms(x, w):
    return x * lax.rsqrt(jnp.mean(x * x, axis=-1, keepdims=True) + RMS_EPS) * w


def _t5_bucket(rel):
    nb = N_BUCKETS // 2
    max_exact = nb // 2
    ret = jnp.where(rel > 0, nb, 0)
    n = jnp.abs(rel)
    nf = jnp.maximum(n, 1).astype(jnp.float32)
    large = max_exact + (jnp.log(nf / max_exact) / math.log(MAX_DISTANCE / max_exact)
                         * (nb - max_exact)).astype(jnp.int32)
    large = jnp.minimum(large, nb - 1)
    return ret + jnp.where(n < max_exact, n, large)


def _levels(c):
    out, m = [], c // 2
    while m >= 1:
        out.append(m)
        m //= 2
    return out


def _level_masks(c):
    t = np.arange(c)[:, None]
    s = np.arange(c)[None, :]
    masks = []
    for m in _levels(c):
        masks.append((t // (2 * m) == s // (2 * m)) & ((t // m) % 2 == 1) & ((s // m) % 2 == 0))
    masks.append(t == s)
    return np.stack(masks).astype(np.float32)


def _block_tri(t, c):
    i = np.arange(t)[:, None]
    j = np.arange(t)[None, :]
    return ((i // c == j // c) & (j <= i)).astype(np.float32)


def _head_variants(a):
    lo = lax.broadcasted_iota(jnp.int32, a.shape, 1) < HEAD_DIM
    rot = pltpu.roll(a, HEAD_DIM, axis=1)
    return [_bf(jnp.where(lo, a, 0.0)), _bf(jnp.where(lo, 0.0, rot)),
            _bf(jnp.where(lo, rot, 0.0)), _bf(jnp.where(lo, 0.0, a))]


def _mixer_kernel(*refs, tile, chunk, carry, d_model, layer):
    n_chunks = tile // chunk
    keys = WINDOW + chunk
    key_pad = 2 * LANES
    assert keys < key_pad and tile >= WINDOW
    levels = _levels(chunk)
    n_hg = d_model // HG_DK
    att_w = N_HEADS * HEAD_DIM
    n_pairs = N_HEADS // 2
    ppk = n_pairs // N_KV_HEADS
    rows_g = ppk * chunk
    n_var = WINDOW // chunk + 1 if carry else 1
    f32 = jnp.float32

    refs = list(refs)
    (x_ref, nw_ref, win_ref, wout_ref, sinks_ref, table_ref, bucket_ref, lbp_ref, gn_ref,
     tri_ref, lmask_ref) = refs[:11]
    refs = refs[11:]
    if not carry:
        hk_ref, hv_ref, si_ref = refs[:3]
        refs = refs[3:]
    y_ref, ko_ref, vo_ref, so_ref = refs[:4]
    refs = refs[4:]
    (bias_s, q4_s, kv_s, kx_s, vx_s, hq_s, kk_s, hi_s, lf_s, bc_s, og_s, ga_s, gh_s, att_s, hg_s,
     u_s, sb_s) = refs[:17]
    if carry:
        st_s = refs[17]

    first = (pl.program_id(0) == 0) & (pl.program_id(1) == 0)
    t_idx = pl.program_id(1)

    @pl.when(first)
    def _build_bias():
        bucket = bucket_ref[...]
        lane = lax.broadcasted_iota(jnp.int32, bucket.shape, 1)
        for head in range(N_HEADS):
            acc = jnp.full(bucket.shape, -jnp.inf, f32)
            for b in range(N_BUCKETS):
                acc = jnp.where(bucket == b, table_ref[b, head], acc)
            acc = jnp.where(bucket == N_BUCKETS, sinks_ref[head], acc)
            pair, side = divmod(head, 2)
            kvh, j = divmod(pair, ppk)
            for var in range(n_var):
                n_invalid = WINDOW - var * chunk if carry else 0
                bias_s[var, 2 * kvh + side, j * chunk:(j + 1) * chunk, :] = jnp.where(lane < n_invalid, -jnp.inf, acc)

    if carry:
        @pl.when(t_idx == 0)
        def _reset():
            kx_s[:, 0, 0:WINDOW, :] = jnp.zeros((4, WINDOW, LANES), jnp.bfloat16)
            vx_s[:, 0, 0:WINDOW, :] = jnp.zeros((4, WINDOW, LANES), jnp.bfloat16)
            st_s[...] = jnp.zeros_like(st_s)

        @pl.when(t_idx > 0)
        def _shift():
            kx_s[:, 0, 0:WINDOW, :] = kx_s[:, 0, tile:tile + WINDOW, :]
            vx_s[:, 0, 0:WINDOW, :] = vx_s[:, 0, tile:tile + WINDOW, :]

    x = x_ref[0]
    h = _bf(_rms(x, nw_ref[...]))

    def proj(i0, width):
        return _dot(h, win_ref[:, i0:i0 + width])

    off = 0
    q = _bf(proj(off, att_w) * ATTN_SCALE)
    for c in range(n_chunks):
        for j in range(n_pairs):
            q4_s[c, j] = q[c * chunk:(c + 1) * chunk, j * LANES:(j + 1) * LANES]
    off += att_w
    kv = proj(off, 2 * KV_WIDTH)
    kv_s[...] = kv
    off += 2 * KV_WIDTH
    k_var = _head_variants(kv[:, 0:KV_WIDTH])
    v_var = _head_variants(kv[:, KV_WIDTH:2 * KV_WIDTH])
    if carry:
        for g in range(4):
            kx_s[g, 0, WINDOW:WINDOW + tile, :] = k_var[g]
            vx_s[g, 0, WINDOW:WINDOW + tile, :] = v_var[g]
    else:
        for c in range(n_chunks):
            hk = _head_variants(hk_ref[c])
            hv = _head_variants(hv_ref[c])
            for g in range(4):
                kx_s[g, c, 0:WINDOW, :] = hk[g]
                vx_s[g, c, 0:WINDOW, :] = hv[g]
                kx_s[g, c, WINDOW:keys, :] = k_var[g][c * chunk:(c + 1) * chunk]
                vx_s[g, c, WINDOW:keys, :] = v_var[g][c * chunk:(c + 1) * chunk]
    hq_s[...] = proj(off, d_model)
    off += d_model
    lbp = lbp_ref[...]
    e = jnp.exp(lbp - jnp.max(lbp, axis=0, keepdims=True))
    lb = jnp.sum(e[:layer + 1], axis=0, keepdims=True) / jnp.sum(e, axis=0, keepdims=True)
    f = lb + (1.0 - lb) * jax.nn.sigmoid(proj(off, d_model))
    off += d_model
    kk_s[...] = 1.0 - f
    lf = jnp.log(f)
    lf_s[...] = lf
    l_hi = _bf(lf)
    r1 = lf - l_hi.astype(f32)
    l_mid = _bf(r1)
    l_lo = _bf(r1 - l_mid.astype(f32))
    tri = tri_ref[...]
    bc_s[...] = _dot(tri, l_hi) + _dot(tri, l_mid) + _dot(tri, l_lo)
    hi_s[...] = _bf(proj(off, d_model))
    off += d_model
    og_s[...] = jax.nn.silu(proj(off, d_model))
    off += d_model
    ga_s[...] = jax.nn.sigmoid(proj(off, d_model))
    off += d_model
    gh_s[...] = jax.nn.sigmoid(proj(off, d_model))

    zpad = jnp.zeros((key_pad - keys, LANES), jnp.bfloat16)

    def window(ref, g, c):
        w = ref[g, 0, c * chunk:c * chunk + keys, :] if carry else ref[g, c]
        return jnp.concatenate([w, zpad], axis=0)

    for c in range(n_chunks):
        var = jnp.minimum(t_idx * n_chunks + c, n_var - 1) if carry else 0
        for kvh in range(N_KV_HEADS):
            qg = q4_s[c, kvh * ppk:(kvh + 1) * ppk].reshape(rows_g, LANES)
            probs = []
            for side in range(2):
                g = 2 * kvh + side
                s = _dot(qg, window(kx_s, g, c), _NT) + bias_s[var, g]
                ex = jnp.exp(s - jnp.max(s, axis=-1, keepdims=True))
                probs.append(_bf(ex * (1.0 / jnp.sum(ex, axis=-1, keepdims=True))))
            o = _dot(probs[0], window(vx_s, 2 * kvh, c)) + _dot(probs[1], window(vx_s, 2 * kvh + 1, c))
            for j in range(ppk):
                col = (kvh * ppk + j) * LANES
                att_s[c * chunk:(c + 1) * chunk, col:col + LANES] = o[j * chunk:(j + 1) * chunk]

    row_c = lax.broadcasted_iota(jnp.int32, (chunk, HG_DK), 0)
    uppers = [((row_c >> int(math.log2(m))) & 1) == 1 for m in levels]
    r4 = row_c & 3
    gn = gn_ref[...]

    def unit(c, hd):
        return slice(c * chunk, (c + 1) * chunk), slice(hd * HG_DK, (hd + 1) * HG_DK)

    for c in range(n_chunks):
        for hd in range(n_hg):
            rows, cols = unit(c, hd)
            bc = bc_s[rows, cols]
            k_dec = _bf(kk_s[rows, cols] * jnp.exp(bc[chunk - 1:chunk, :] - bc))
            u_s[c, hd] = _dot(hi_s[rows, cols], k_dec, _TN)

    for hd in range(n_hg):
        if carry:
            s = st_s[hd]
        for c in range(n_chunks):
            rows, cols = unit(c, hd)
            decay = jnp.exp(bc_s[(c + 1) * chunk - 1:(c + 1) * chunk, cols])
            if carry:
                sb_s[c, hd] = _bf(s)
                s = s * decay + u_s[c, hd]
            else:
                s0 = si_ref[c, hd].T
                sb_s[c, hd] = _bf(s0)
                so_ref[c, hd] = (s0 * decay + u_s[c, hd]).T
        if carry:
            st_s[hd] = s

    for c in range(n_chunks):
        for hd in range(n_hg):
            rows, cols = unit(c, hd)
            q = hq_s[rows, cols]
            kk = kk_s[rows, cols]
            bc = bc_s[rows, cols]
            lfh = lf_s[rows, cols]
            inter = _dot(_bf(q * jnp.exp(bc)), sb_s[c, hd], _NT)
            a = _dot(_bf(q), _bf(kk), _NT) * lmask_ref[len(levels)]
            for li, m in enumerate(levels):
                upper = uppers[li]
                if m == 1:
                    z = jnp.where(upper, q * (1.0 - kk), kk)
                else:
                    if 2 * m >= 8:
                        ng = chunk // (2 * m)
                        ref_row = bc.reshape(ng, 2 * m, HG_DK)[:, m - 1:m, :]
                        ref = jnp.broadcast_to(ref_row, (ng, 2 * m, HG_DK)).reshape(chunk, HG_DK)
                        arg = jnp.where(upper, bc - ref, ref - bc)
                    else:
                        assert m == 2
                        nxt = pltpu.roll(lfh, chunk - 1, axis=0)
                        prv = pltpu.roll(lfh, 1, axis=0)
                        arg = jnp.where(r4 == 0, nxt, jnp.where(r4 == 1, 0.0, jnp.where(r4 == 2, lfh, lfh + prv)))
                    z = jnp.where(upper, q, kk) * jnp.exp(arg)
                z = _bf(z)
                a = a + _dot(z, z, _NT) * lmask_ref[li]
            o = inter + _dot(_bf(a), hi_s[rows, cols])
            o = o * lax.rsqrt(jnp.mean(o * o, axis=-1, keepdims=True) + RMS_EPS) * gn
            hg_s[rows, cols] = o * og_s[rows, cols]

    merged = _bf(ga_s[...] * att_s[...] + gh_s[...] * hg_s[...])
    y_ref[0] = x + _dot(merged, wout_ref[...])

    if carry:
        @pl.when(t_idx == pl.num_programs(1) - 1)
        def _emit():
            ko_ref[0] = kv_s[tile - WINDOW:tile, 0:KV_WIDTH]
            vo_ref[0] = kv_s[tile - WINDOW:tile, KV_WIDTH:2 * KV_WIDTH]
            for hd in range(n_hg):
                so_ref[0, hd] = st_s[hd].T
    else:
        ko_ref[0] = kv_s[:, 0:KV_WIDTH]
        vo_ref[0] = kv_s[:, KV_WIDTH:2 * KV_WIDTH]


def _full_spec(shape):
    nd = len(shape)
    return pl.BlockSpec(shape, lambda *_: (0,) * nd)


def _weight_spec(shape):
    nd = len(shape)
    return pl.BlockSpec(shape, lambda *_: (0,) * nd, pipeline_mode=pl.Buffered(1))


def _mixer(x, hist, norm_w, w_in, w_out, sinks, table, lb_params, gnorm_w, *, layer, carry, tile, chunk):
    nb, s, d = x.shape
    n_t = s // tile
    n_chunks = tile // chunk
    keys = WINDOW + chunk
    key_pad = 2 * LANES
    n_hg = d // HG_DK
    n_streams = nb if carry else s // chunk
    rel = (jnp.arange(key_pad, dtype=jnp.int32)[None, :] - WINDOW
           - jnp.arange(chunk, dtype=jnp.int32)[:, None])
    lane = jnp.arange(key_pad, dtype=jnp.int32)[None, :]
    bucket = jnp.where(lane < keys, _t5_bucket(rel), jnp.where(lane == keys, N_BUCKETS, N_BUCKETS + 1))
    bucket = bucket.astype(jnp.int32)
    tri = jnp.asarray(_block_tri(tile, chunk), jnp.bfloat16)
    lmask = jnp.asarray(_level_masks(chunk))
    smem = pl.BlockSpec(memory_space=pltpu.SMEM)

    in_arrays = [x, norm_w.reshape(1, d), w_in, w_out, sinks, table, bucket, lb_params,
                 gnorm_w.reshape(1, HG_DK), tri, lmask]
    in_specs = [pl.BlockSpec((1, tile, d), lambda b, t: (b, t, 0)), _full_spec((1, d)),
                _weight_spec(w_in.shape), _weight_spec(w_out.shape), smem, smem, _full_spec(bucket.shape),
                _full_spec(lb_params.shape), _full_spec((1, HG_DK)), _full_spec(tri.shape),
                _full_spec(lmask.shape)]
    if carry:
        kv_rows = WINDOW
        st_block = (1, n_hg, HG_DK, HG_DK)
        n_win, win_rows, n_var = 1, WINDOW + tile, WINDOW // chunk + 1
    else:
        in_arrays += list(hist)
        in_specs += [_full_spec(a.shape) for a in hist]
        kv_rows = s
        st_block = (n_streams, n_hg, HG_DK, HG_DK)
        n_win, win_rows, n_var = n_chunks, keys, 1
    out_shape = [jax.ShapeDtypeStruct((nb, s, d), jnp.float32),
                 jax.ShapeDtypeStruct((nb, kv_rows, KV_WIDTH), jnp.float32),
                 jax.ShapeDtypeStruct((nb, kv_rows, KV_WIDTH), jnp.float32),
                 jax.ShapeDtypeStruct((n_streams, n_hg, HG_DK, HG_DK), jnp.float32)]
    out_specs = [pl.BlockSpec((1, tile, d), lambda b, t: (b, t, 0)),
                 pl.BlockSpec((1, kv_rows, KV_WIDTH), lambda b, t: (b, 0, 0)),
                 pl.BlockSpec((1, kv_rows, KV_WIDTH), lambda b, t: (b, 0, 0)),
                 pl.BlockSpec(st_block, lambda b, t: (b, 0, 0, 0))]
    f32, bf16 = jnp.float32, jnp.bfloat16
    rows_g = (N_HEADS // 2 // N_KV_HEADS) * chunk
    scratch = [pltpu.VMEM((n_var, 4, rows_g, key_pad), f32),
               pltpu.VMEM((n_chunks, N_HEADS // 2, chunk, LANES), bf16),
               pltpu.VMEM((tile, 2 * KV_WIDTH), f32),
               pltpu.VMEM((4, n_win, win_rows, LANES), bf16),
               pltpu.VMEM((4, n_win, win_rows, LANES), bf16),
               pltpu.VMEM((tile, d), f32), pltpu.VMEM((tile, d), f32),
               pltpu.VMEM((tile, d), bf16)]
    scratch += [pltpu.VMEM((tile, d), f32) for _ in range(7)]
    scratch += [pltpu.VMEM((n_chunks, n_hg, HG_DK, HG_DK), f32),
                pltpu.VMEM((n_chunks, n_hg, HG_DK, HG_DK), bf16)]
    if carry:
        scratch += [pltpu.VMEM((n_hg, HG_DK, HG_DK), f32)]
    kern = functools.partial(_mixer_kernel, tile=tile, chunk=chunk, carry=carry, d_model=d, layer=layer)
    return pl.pallas_call(
        kern,
        grid=(nb, n_t),
        in_specs=in_specs,
        out_specs=out_specs,
        out_shape=out_shape,
        scratch_shapes=scratch,
        compiler_params=pltpu.CompilerParams(
            dimension_semantics=("arbitrary", "arbitrary"), vmem_limit_bytes=VMEM_LIMIT_BYTES),
        name="mixer_prompt" if carry else "mixer_sample",
    )(*in_arrays)


def _ffn_kernel(x_ref, nw_ref, wgu_ref, wd_ref, fw_ref, y_ref, act_s, *, d_ff, col_tile, final_norm):
    x = x_ref[...]
    h = _bf(_rms(x, nw_ref[...]))
    for j in range(d_ff // col_tile):
        g = _dot(h, wgu_ref[:, j * col_tile:(j + 1) * col_tile])
        u = _dot(h, wgu_ref[:, d_ff + j * col_tile:d_ff + (j + 1) * col_tile])
        act_s[:, j * col_tile:(j + 1) * col_tile] = _bf(jax.nn.silu(g) * u)
    y = x + _dot(act_s[...], wd_ref[...])
    if final_norm:
        y = _rms(y, fw_ref[...])
    y_ref[...] = y


def _ffn(x, norm_w, w_gate_up, w_down, final_w, *, tile, final_norm):
    n, d = x.shape
    d_ff = w_down.shape[0]
    kern = functools.partial(_ffn_kernel, d_ff=d_ff, col_tile=2 * LANES, final_norm=final_norm)
    return pl.pallas_call(
        kern,
        grid=(n // tile,),
        in_specs=[pl.BlockSpec((tile, d), lambda i: (i, 0)), _full_spec((1, d)),
                  _weight_spec(w_gate_up.shape), _weight_spec(w_down.shape), _full_spec((1, d))],
        out_specs=pl.BlockSpec((tile, d), lambda i: (i, 0)),
        out_shape=jax.ShapeDtypeStruct((n, d), jnp.float32),
        scratch_shapes=[pltpu.VMEM((tile, d_ff), jnp.bfloat16)],
        compiler_params=pltpu.CompilerParams(
            dimension_semantics=("arbitrary",), vmem_limit_bytes=VMEM_LIMIT_BYTES),
        name="ffn",
    )(x, norm_w.reshape(1, d), w_gate_up, w_down, final_w.reshape(1, d))


def kernel(x_prompt, x_sample, cache_k, cache_v, state_hgrn, norm_mix, w_in, w_out, attn_sinks, rel_bias_table,
           hgrn_lb, hgrn_norm, norm_ffn, w_gate_up, w_down, norm_final):
    depth = w_in.shape[0]
    batch, seq, d = x_prompt.shape
    dec_batch, dec_seq, _ = x_sample.shape
    xp = x_prompt
    xs = x_sample.reshape(1, dec_batch * dec_seq, d)
    outs = [[] for _ in range(6)]
    for l in range(depth):
        w_in_l, w_out_l = _bf(w_in[l]), _bf(w_out[l])
        w_gu_l, w_d_l = _bf(w_gate_up[l]), _bf(w_down[l])
        shared = (norm_mix[l], w_in_l, w_out_l, attn_sinks[l], rel_bias_table, hgrn_lb, hgrn_norm[l])
        xp, kp, vp, sp = _mixer(xp, None, *shared, layer=l, carry=True, tile=4 * CHUNK, chunk=CHUNK)
        hist = (cache_k[l].reshape(dec_batch, WINDOW, KV_WIDTH), cache_v[l].reshape(dec_batch, WINDOW, KV_WIDTH),
                state_hgrn[l])
        xs, ks, vs, ss = _mixer(xs, hist, *shared, layer=l, carry=False, tile=dec_batch * dec_seq, chunk=dec_seq)
        last = l == depth - 1
        xp = _ffn(xp.reshape(batch * seq, d), norm_ffn[l], w_gu_l, w_d_l, norm_final,
                  tile=512, final_norm=last).reshape(batch, seq, d)
        xs = _ffn(xs.reshape(dec_batch * dec_seq, d), norm_ffn[l], w_gu_l, w_d_l, norm_final,
                  tile=dec_batch * dec_seq, final_norm=last).reshape(1, dec_batch * dec_seq, d)
        ks = jnp.concatenate([hist[0][:, dec_seq:], ks.reshape(dec_batch, dec_seq, KV_WIDTH)], axis=1)
        vs = jnp.concatenate([hist[1][:, dec_seq:], vs.reshape(dec_batch, dec_seq, KV_WIDTH)], axis=1)
        kv_shape = (-1, WINDOW, N_KV_HEADS, HEAD_DIM)
        for acc, val in zip(outs, (kp.reshape(kv_shape), vp.reshape(kv_shape), sp,
                                   ks.reshape(kv_shape), vs.reshape(kv_shape), ss)):
            acc.append(val)
    return (xp, xs.reshape(dec_batch, dec_seq, d)) + tuple(jnp.stack(o) for o in outs)
```

```python
import functools
import math

import numpy as np
import jax
import jax.numpy as jnp
from jax import lax
from jax.experimental import pallas as pl
from jax.experimental.pallas import tpu as pltpu

CHUNK = 64
N_HEADS = 16
N_KV_HEADS = 2
HEAD_DIM = 64
GROUP = N_HEADS // N_KV_HEADS
KV_WIDTH = N_KV_HEADS * HEAD_DIM
WINDOW = 128
ATTN_SCALE = HEAD_DIM ** -0.5
N_BUCKETS = 32
MAX_DISTANCE = 128
HG_DK = 128
RMS_EPS = 1e-6

LANES = 128
VMEM_LIMIT_BYTES = 56 * 1024 * 1024

_NT = (((1,), (1,)), ((), ()))
_TN = (((0,), (0,)), ((), ()))


def _bf(x):
    return x.astype(jnp.bfloat16)


def _dot(a, b, dims=None):
    if dims is None:
        return jnp.dot(a, b, preferred_element_type=jnp.float32)
    return lax.dot_general(a, b, dims, preferred_element_type=jnp.float32)


def _rms(x, w):
    return x * lax.rsqrt(jnp.mean(x * x, axis=-1, keepdims=True) + RMS_EPS) * w


def _t5_bucket(rel):
    nb = N_BUCKETS // 2
    max_exact = nb // 2
    ret = jnp.where(rel > 0, nb, 0)
    n = jnp.abs(rel)
    nf = jnp.maximum(n, 1).astype(jnp.float32)
    large = max_exact + (jnp.log(nf / max_exact) / math.log(MAX_DISTANCE / max_exact)
                         * (nb - max_exact)).astype(jnp.int32)
    large = jnp.minimum(large, nb - 1)
    return ret + jnp.where(n < max_exact, n, large)


def _levels(c):
    out, m = [], c // 2
    while m >= 1:
        out.append(m)
        m //= 2
    return out


def _level_masks(c):
    t = np.arange(c)[:, None]
    s = np.arange(c)[None, :]
    masks = []
    for m in _levels(c):
        masks.append((t // (2 * m) == s // (2 * m)) & ((t // m) % 2 == 1) & ((s // m) % 2 == 0))
    masks.append(t == s)
    return np.stack(masks).astype(np.float32)


def _block_tri(t, c):
    i = np.arange(t)[:, None]
    j = np.arange(t)[None, :]
    return ((i // c == j // c) & (j <= i)).astype(np.float32)


def _head_variants(a):
    lo = lax.broadcasted_iota(jnp.int32, a.shape, 1) < HEAD_DIM
    rot = pltpu.roll(a, HEAD_DIM, axis=1)
    return [_bf(jnp.where(lo, a, 0.0)), _bf(jnp.where(lo, 0.0, rot)),
            _bf(jnp.where(lo, rot, 0.0)), _bf(jnp.where(lo, 0.0, a))]


def _mixer_kernel(*refs, tile, chunk, carry, d_model, layer):
    n_chunks = tile // chunk
    keys = WINDOW + chunk
    key_pad = 2 * LANES
    assert keys < key_pad and tile >= WINDOW
    levels = _levels(chunk)
    n_hg = d_model // HG_DK
    att_w = N_HEADS * HEAD_DIM
    n_pairs = N_HEADS // 2
    ppk = n_pairs // N_KV_HEADS
    rows_g = ppk * chunk
    n_var = WINDOW // chunk + 1 if carry else 1
    f32 = jnp.float32

    refs = list(refs)
    (x_ref, nw_ref, win_ref, wout_ref, sinks_ref, table_ref, bucket_ref, lbp_ref, gn_ref,
     tri_ref, lmask_ref) = refs[:11]
    refs = refs[11:]
    if not carry:
        hk_ref, hv_ref, si_ref = refs[:3]
        refs = refs[3:]
    y_ref, ko_ref, vo_ref, so_ref = refs[:4]
    refs = refs[4:]
    (bias_s, q4_s, kv_s, kx_s, vx_s, hq_s, kk_s, hi_s, lf_s, bc_s, og_s, ga_s, gh_s, att_s, hg_s,
     u_s, sb_s, p_s, a_s) = refs[:19]
    if carry:
        st_s = refs[19]

    first = (pl.program_id(0) == 0) & (pl.program_id(1) == 0)
    t_idx = pl.program_id(1)

    @pl.when(first)
    def _build_bias():
        bucket = bucket_ref[...]
        lane = lax.broadcasted_iota(jnp.int32, bucket.shape, 1)
        for head in range(N_HEADS):
            acc = jnp.full(bucket.shape, -jnp.inf, f32)
            for b in range(N_BUCKETS):
                acc = jnp.where(bucket == b, table_ref[b, head], acc)
            acc = jnp.where(bucket == N_BUCKETS, sinks_ref[head], acc)
            pair, side = divmod(head, 2)
            kvh, j = divmod(pair, ppk)
            for var in range(n_var):
                n_invalid = WINDOW - var * chunk if carry else 0
                bias_s[var, 2 * kvh + side, j * chunk:(j + 1) * chunk, :] = jnp.where(lane < n_invalid, -jnp.inf, acc)

    if carry:
        @pl.when(t_idx == 0)
        def _reset():
            kx_s[:, 0, 0:WINDOW, :] = jnp.zeros((4, WINDOW, LANES), jnp.bfloat16)
            vx_s[:, 0, 0:WINDOW, :] = jnp.zeros((4, WINDOW, LANES), jnp.bfloat16)
            st_s[...] = jnp.zeros_like(st_s)

        @pl.when(t_idx > 0)
        def _shift():
            kx_s[:, 0, 0:WINDOW, :] = kx_s[:, 0, tile:tile + WINDOW, :]
            vx_s[:, 0, 0:WINDOW, :] = vx_s[:, 0, tile:tile + WINDOW, :]

    x = x_ref[0]
    h = _bf(_rms(x, nw_ref[...]))

    def proj(i0, width):
        return _dot(h, win_ref[:, i0:i0 + width])

    off = 0
    q = _bf(proj(off, att_w) * ATTN_SCALE)
    for c in range(n_chunks):
        for j in range(n_pairs):
            q4_s[c, j] = q[c * chunk:(c + 1) * chunk, j * LANES:(j + 1) * LANES]
    off += att_w
    kv = proj(off, 2 * KV_WIDTH)
    kv_s[...] = kv
    off += 2 * KV_WIDTH
    k_var = _head_variants(kv[:, 0:KV_WIDTH])
    v_var = _head_variants(kv[:, KV_WIDTH:2 * KV_WIDTH])
    if carry:
        for g in range(4):
            kx_s[g, 0, WINDOW:WINDOW + tile, :] = k_var[g]
            vx_s[g, 0, WINDOW:WINDOW + tile, :] = v_var[g]
    else:
        for c in range(n_chunks):
            hk = _head_variants(hk_ref[c])
            hv = _head_variants(hv_ref[c])
            for g in range(4):
                kx_s[g, c, 0:WINDOW, :] = hk[g]
                vx_s[g, c, 0:WINDOW, :] = hv[g]
                kx_s[g, c, WINDOW:keys, :] = k_var[g][c * chunk:(c + 1) * chunk]
                vx_s[g, c, WINDOW:keys, :] = v_var[g][c * chunk:(c + 1) * chunk]
    hq_s[...] = proj(off, d_model)
    off += d_model
    lbp = lbp_ref[...]
    e = jnp.exp(lbp - jnp.max(lbp, axis=0, keepdims=True))
    lb = jnp.sum(e[:layer + 1], axis=0, keepdims=True) / jnp.sum(e, axis=0, keepdims=True)
    f = lb + (1.0 - lb) * jax.nn.sigmoid(proj(off, d_model))
    off += d_model
    kk_s[...] = 1.0 - f
    lf = jnp.log(f)
    lf_s[...] = lf
    l_hi = _bf(lf)
    r1 = lf - l_hi.astype(f32)
    l_mid = _bf(r1)
    l_lo = _bf(r1 - l_mid.astype(f32))
    tri = tri_ref[...]
    bc_s[...] = _dot(tri, l_hi) + _dot(tri, l_mid) + _dot(tri, l_lo)
    hi_s[...] = _bf(proj(off, d_model))
    off += d_model
    og_s[...] = jax.nn.silu(proj(off, d_model))
    off += d_model
    ga_s[...] = jax.nn.sigmoid(proj(off, d_model))
    off += d_model
    gh_s[...] = jax.nn.sigmoid(proj(off, d_model))

    zpad = jnp.zeros((key_pad - keys, LANES), jnp.bfloat16)

    def window(ref, g, c):
        w = ref[g, 0, c * chunk:c * chunk + keys, :] if carry else ref[g, c]
        return jnp.concatenate([w, zpad], axis=0)

    for c in range(n_chunks):
        var = jnp.minimum(t_idx * n_chunks + c, n_var - 1) if carry else 0
        for kvh in range(N_KV_HEADS):
            qg = q4_s[c, kvh * ppk:(kvh + 1) * ppk].reshape(rows_g, LANES)
            for side in range(2):
                g = 2 * kvh + side
                s = _dot(qg, window(kx_s, g, c), _NT) + bias_s[var, g]
                ex = jnp.exp(s - jnp.max(s, axis=-1, keepdims=True))
                p_s[c, g] = _bf(ex * (1.0 / jnp.sum(ex, axis=-1, keepdims=True)))
    for c in range(n_chunks):
        for kvh in range(N_KV_HEADS):
            o = (_dot(p_s[c, 2 * kvh], window(vx_s, 2 * kvh, c))
                 + _dot(p_s[c, 2 * kvh + 1], window(vx_s, 2 * kvh + 1, c)))
            for j in range(ppk):
                col = (kvh * ppk + j) * LANES
                att_s[c * chunk:(c + 1) * chunk, col:col + LANES] = o[j * chunk:(j + 1) * chunk]

    row_c = lax.broadcasted_iota(jnp.int32, (chunk, HG_DK), 0)
    uppers = [((row_c >> int(math.log2(m))) & 1) == 1 for m in levels]
    r4 = row_c & 3
    gn = gn_ref[...]

    def unit(c, hd):
        return slice(c * chunk, (c + 1) * chunk), slice(hd * HG_DK, (hd + 1) * HG_DK)

    for c in range(n_chunks):
        for hd in range(n_hg):
            rows, cols = unit(c, hd)
            bc = bc_s[rows, cols]
            k_dec = _bf(kk_s[rows, cols] * jnp.exp(bc[chunk - 1:chunk, :] - bc))
            u_s[c, hd] = _dot(hi_s[rows, cols], k_dec, _TN)

    for hd in range(n_hg):
        if carry:
            s = st_s[hd]
        for c in range(n_chunks):
            rows, cols = unit(c, hd)
            decay = jnp.exp(bc_s[(c + 1) * chunk - 1:(c + 1) * chunk, cols])
            if carry:
                sb_s[c, hd] = _bf(s)
                s = s * decay + u_s[c, hd]
            else:
                s0 = si_ref[c, hd].T
                sb_s[c, hd] = _bf(s0)
                so_ref[c, hd] = (s0 * decay + u_s[c, hd]).T
        if carry:
            st_s[hd] = s

    for c in range(n_chunks):
        for hd in range(n_hg):
            rows, cols = unit(c, hd)
            q = hq_s[rows, cols]
            kk = kk_s[rows, cols]
            bc = bc_s[rows, cols]
            lfh = lf_s[rows, cols]
            a = _dot(_bf(q), _bf(kk), _NT) * lmask_ref[len(levels)]
            for li, m in enumerate(levels):
                upper = uppers[li]
                if m == 1:
                    z = jnp.where(upper, q * (1.0 - kk), kk)
                else:
                    if 2 * m >= 8:
                        ng = chunk // (2 * m)
                        ref_row = bc.reshape(ng, 2 * m, HG_DK)[:, m - 1:m, :]
                        ref = jnp.broadcast_to(ref_row, (ng, 2 * m, HG_DK)).reshape(chunk, HG_DK)
                        arg = jnp.where(upper, bc - ref, ref - bc)
                    else:
                        assert m == 2
                        nxt = pltpu.roll(lfh, chunk - 1, axis=0)
                        prv = pltpu.roll(lfh, 1, axis=0)
                        arg = jnp.where(r4 == 0, nxt, jnp.where(r4 == 1, 0.0, jnp.where(r4 == 2, lfh, lfh + prv)))
                    z = jnp.where(upper, q, kk) * jnp.exp(arg)
                z = _bf(z)
                a = a + _dot(z, z, _NT) * lmask_ref[li]
            a_s[c, hd] = _bf(a)

    for c in range(n_chunks):
        for hd in range(n_hg):
            rows, cols = unit(c, hd)
            inter = _dot(_bf(hq_s[rows, cols] * jnp.exp(bc_s[rows, cols])), sb_s[c, hd], _NT)
            o = inter + _dot(a_s[c, hd], hi_s[rows, cols])
            o = o * lax.rsqrt(jnp.mean(o * o, axis=-1, keepdims=True) + RMS_EPS) * gn
            hg_s[rows, cols] = o * og_s[rows, cols]

    merged = _bf(ga_s[...] * att_s[...] + gh_s[...] * hg_s[...])
    y_ref[0] = x + _dot(merged, wout_ref[...])

    if carry:
        @pl.when(t_idx == pl.num_programs(1) - 1)
        def _emit():
            ko_ref[0] = kv_s[tile - WINDOW:tile, 0:KV_WIDTH]
            vo_ref[0] = kv_s[tile - WINDOW:tile, KV_WIDTH:2 * KV_WIDTH]
            for hd in range(n_hg):
                so_ref[0, hd] = st_s[hd].T
    else:
        ko_ref[0] = kv_s[:, 0:KV_WIDTH]
        vo_ref[0] = kv_s[:, KV_WIDTH:2 * KV_WIDTH]


def _full_spec(shape):
    nd = len(shape)
    return pl.BlockSpec(shape, lambda *_: (0,) * nd)


def _weight_spec(shape):
    nd = len(shape)
    return pl.BlockSpec(shape, lambda *_: (0,) * nd, pipeline_mode=pl.Buffered(1))


def _mixer(x, hist, norm_w, w_in, w_out, sinks, table, lb_params, gnorm_w, *, layer, carry, tile, chunk):
    nb, s, d = x.shape
    n_t = s // tile
    n_chunks = tile // chunk
    keys = WINDOW + chunk
    key_pad = 2 * LANES
    n_hg = d // HG_DK
    n_streams = nb if carry else s // chunk
    rel = (jnp.arange(key_pad, dtype=jnp.int32)[None, :] - WINDOW
           - jnp.arange(chunk, dtype=jnp.int32)[:, None])
    lane = jnp.arange(key_pad, dtype=jnp.int32)[None, :]
    bucket = jnp.where(lane < keys, _t5_bucket(rel), jnp.where(lane == keys, N_BUCKETS, N_BUCKETS + 1))
    bucket = bucket.astype(jnp.int32)
    tri = jnp.asarray(_block_tri(tile, chunk), jnp.bfloat16)
    lmask = jnp.asarray(_level_masks(chunk))
    smem = pl.BlockSpec(memory_space=pltpu.SMEM)

    in_arrays = [x, norm_w.reshape(1, d), w_in, w_out, sinks, table, bucket, lb_params,
                 gnorm_w.reshape(1, HG_DK), tri, lmask]
    in_specs = [pl.BlockSpec((1, tile, d), lambda b, t: (b, t, 0)), _full_spec((1, d)),
                _weight_spec(w_in.shape), _weight_spec(w_out.shape), smem, smem, _full_spec(bucket.shape),
                _full_spec(lb_params.shape), _full_spec((1, HG_DK)), _full_spec(tri.shape),
                _full_spec(lmask.shape)]
    if carry:
        kv_rows = WINDOW
        st_block = (1, n_hg, HG_DK, HG_DK)
        n_win, win_rows, n_var = 1, WINDOW + tile, WINDOW // chunk + 1
    else:
        in_arrays += list(hist)
        in_specs += [_full_spec(a.shape) for a in hist]
        kv_rows = s
        st_block = (n_streams, n_hg, HG_DK, HG_DK)
        n_win, win_rows, n_var = n_chunks, keys, 1
    out_shape = [jax.ShapeDtypeStruct((nb, s, d), jnp.float32),
                 jax.ShapeDtypeStruct((nb, kv_rows, KV_WIDTH), jnp.float32),
                 jax.ShapeDtypeStruct((nb, kv_rows, KV_WIDTH), jnp.float32),
                 jax.ShapeDtypeStruct((n_streams, n_hg, HG_DK, HG_DK), jnp.float32)]
    out_specs = [pl.BlockSpec((1, tile, d), lambda b, t: (b, t, 0)),
                 pl.BlockSpec((1, kv_rows, KV_WIDTH), lambda b, t: (b, 0, 0)),
                 pl.BlockSpec((1, kv_rows, KV_WIDTH), lambda b, t: (b, 0, 0)),
                 pl.BlockSpec(st_block, lambda b, t: (b, 0, 0, 0))]
    f32, bf16 = jnp.float32, jnp.bfloat16
    rows_g = (N_HEADS // 2 // N_KV_HEADS) * chunk
    scratch = [pltpu.VMEM((n_var, 4, rows_g, key_pad), f32),
               pltpu.VMEM((n_chunks, N_HEADS // 2, chunk, LANES), bf16),
               pltpu.VMEM((tile, 2 * KV_WIDTH), f32),
               pltpu.VMEM((4, n_win, win_rows, LANES), bf16),
               pltpu.VMEM((4, n_win, win_rows, LANES), bf16),
               pltpu.VMEM((tile, d), f32), pltpu.VMEM((tile, d), f32),
               pltpu.VMEM((tile, d), bf16)]
    scratch += [pltpu.VMEM((tile, d), f32) for _ in range(7)]
    scratch += [pltpu.VMEM((n_chunks, n_hg, HG_DK, HG_DK), f32),
                pltpu.VMEM((n_chunks, n_hg, HG_DK, HG_DK), bf16),
                pltpu.VMEM((n_chunks, 4, rows_g, key_pad), bf16),
                pltpu.VMEM((n_chunks, n_hg, chunk, chunk), bf16)]
    if carry:
        scratch += [pltpu.VMEM((n_hg, HG_DK, HG_DK), f32)]
    kern = functools.partial(_mixer_kernel, tile=tile, chunk=chunk, carry=carry, d_model=d, layer=layer)
    return pl.pallas_call(
        kern,
        grid=(nb, n_t),
        in_specs=in_specs,
        out_specs=out_specs,
        out_shape=out_shape,
        scratch_shapes=scratch,
        compiler_params=pltpu.CompilerParams(
            dimension_semantics=("arbitrary", "arbitrary"), vmem_limit_bytes=VMEM_LIMIT_BYTES),
        name="mixer_prompt" if carry else "mixer_sample",
    )(*in_arrays)


def _ffn_kernel(x_ref, nw_ref, wgu_ref, wd_ref, fw_ref, y_ref, act_s, *, d_ff, col_tile, final_norm):
    x = x_ref[...]
    h = _bf(_rms(x, nw_ref[...]))
    for j in range(d_ff // col_tile):
        g = _dot(h, wgu_ref[:, j * col_tile:(j + 1) * col_tile])
        u = _dot(h, wgu_ref[:, d_ff + j * col_tile:d_ff + (j + 1) * col_tile])
        act_s[:, j * col_tile:(j + 1) * col_tile] = _bf(jax.nn.silu(g) * u)
    y = x + _dot(act_s[...], wd_ref[...])
    if final_norm:
        y = _rms(y, fw_ref[...])
    y_ref[...] = y


def _ffn(x, norm_w, w_gate_up, w_down, final_w, *, tile, final_norm):
    n, d = x.shape
    d_ff = w_down.shape[0]
    kern = functools.partial(_ffn_kernel, d_ff=d_ff, col_tile=2 * LANES, final_norm=final_norm)
    return pl.pallas_call(
        kern,
        grid=(n // tile,),
        in_specs=[pl.BlockSpec((tile, d), lambda i: (i, 0)), _full_spec((1, d)),
                  _weight_spec(w_gate_up.shape), _weight_spec(w_down.shape), _full_spec((1, d))],
        out_specs=pl.BlockSpec((tile, d), lambda i: (i, 0)),
        out_shape=jax.ShapeDtypeStruct((n, d), jnp.float32),
        scratch_shapes=[pltpu.VMEM((tile, d_ff), jnp.bfloat16)],
        compiler_params=pltpu.CompilerParams(
            dimension_semantics=("arbitrary",), vmem_limit_bytes=VMEM_LIMIT_BYTES),
        name="ffn",
    )(x, norm_w.reshape(1, d), w_gate_up, w_down, final_w.reshape(1, d))


def kernel(x_prompt, x_sample, cache_k, cache_v, state_hgrn, norm_mix, w_in, w_out, attn_sinks, rel_bias_table,
           hgrn_lb, hgrn_norm, norm_ffn, w_gate_up, w_down, norm_final):
    depth = w_in.shape[0]
    batch, seq, d = x_prompt.shape
    dec_batch, dec_seq, _ = x_sample.shape
    xp = x_prompt
    xs = x_sample.reshape(1, dec_batch * dec_seq, d)
    outs = [[] for _ in range(6)]
    for l in range(depth):
        w_in_l, w_out_l = _bf(w_in[l]), _bf(w_out[l])
        w_gu_l, w_d_l = _bf(w_gate_up[l]), _bf(w_down[l])
        shared = (norm_mix[l], w_in_l, w_out_l, attn_sinks[l], rel_bias_table, hgrn_lb, hgrn_norm[l])
        xp, kp, vp, sp = _mixer(xp, None, *shared, layer=l, carry=True, tile=4 * CHUNK, chunk=CHUNK)
        hist = (cache_k[l].reshape(dec_batch, WINDOW, KV_WIDTH), cache_v[l].reshape(dec_batch, WINDOW, KV_WIDTH),
                state_hgrn[l])
        xs, ks, vs, ss = _mixer(xs, hist, *shared, layer=l, carry=False, tile=dec_batch * dec_seq, chunk=dec_seq)
        last = l == depth - 1
        xp = _ffn(xp.reshape(batch * seq, d), norm_ffn[l], w_gu_l, w_d_l, norm_final,
                  tile=512, final_norm=last).reshape(batch, seq, d)
        xs = _ffn(xs.reshape(dec_batch * dec_seq, d), norm_ffn[l], w_gu_l, w_d_l, norm_final,
                  tile=dec_batch * dec_seq, final_norm=last).reshape(1, dec_batch * dec_seq, d)
        ks = jnp.concatenate([hist[0][:, dec_seq:], ks.reshape(dec_batch, dec_seq, KV_WIDTH)], axis=1)
        vs = jnp.concatenate([hist[1][:, dec_seq:], vs.reshape(dec_batch, dec_seq, KV_WIDTH)], axis=1)
        kv_shape = (-1, WINDOW, N_KV_HEADS, HEAD_DIM)
        for acc, val in zip(outs, (kp.reshape(kv_shape), vp.reshape(kv_shape), sp,
                                   ks.reshape(kv_shape), vs.reshape(kv_shape), ss)):
            acc.append(val)
    return (xp, xs.reshape(dec_batch, dec_seq, d)) + tuple(jnp.stack(o) for o in outs)
```

```python
import functools
import math

import numpy as np
import jax
import jax.numpy as jnp
from jax import lax
from jax.experimental import pallas as pl
from jax.experimental.pallas import tpu as pltpu

CHUNK = 64
N_HEADS = 16
N_KV_HEADS = 2
HEAD_DIM = 64
GROUP = N_HEADS // N_KV_HEADS
KV_WIDTH = N_KV_HEADS * HEAD_DIM
WINDOW = 128
ATTN_SCALE = HEAD_DIM ** -0.5
N_BUCKETS = 32
MAX_DISTANCE = 128
HG_DK = 128
RMS_EPS = 1e-6
DECAY_GUARD = -1.0

LANES = 128
VMEM_LIMIT_BYTES = 56 * 1024 * 1024

_NT = (((1,), (1,)), ((), ()))
_TN = (((0,), (0,)), ((), ()))


def _bf(x):
    return x.astype(jnp.bfloat16)


def _dot(a, b, dims=None):
    if dims is None:
        return jnp.dot(a, b, preferred_element_type=jnp.float32)
    return lax.dot_general(a, b, dims, preferred_element_type=jnp.float32)


def _rms(x, w):
    return x * lax.rsqrt(jnp.mean(x * x, axis=-1, keepdims=True) + RMS_EPS) * w


def _t5_bucket(rel):
    nb = N_BUCKETS // 2
    max_exact = nb // 2
    ret = jnp.where(rel > 0, nb, 0)
    n = jnp.abs(rel)
    nf = jnp.maximum(n, 1).astype(jnp.float32)
    large = max_exact + (jnp.log(nf / max_exact) / math.log(MAX_DISTANCE / max_exact)
                         * (nb - max_exact)).astype(jnp.int32)
    large = jnp.minimum(large, nb - 1)
    return ret + jnp.where(n < max_exact, n, large)


def _levels(c):
    out, m = [], c // 2
    while m >= 1:
        out.append(m)
        m //= 2
    return out


def _level_masks(c):
    t = np.arange(c)[:, None]
    s = np.arange(c)[None, :]
    masks = []
    for m in _levels(c):
        masks.append((t // (2 * m) == s // (2 * m)) & ((t // m) % 2 == 1) & ((s // m) % 2 == 0))
    masks += [t == s, s <= t]
    return np.stack(masks).astype(np.float32)


def _row_masks(c):
    r = np.arange(c)
    rows = [(r // m) % 2 == 1 for m in _levels(c) if m < 8]
    rows += [r % 4 == 0, r % 4 >= 2, r % 4 == 3]
    return np.repeat(np.stack(rows).astype(np.float32)[:, :, None], LANES, axis=2)


def _block_tri(t, c):
    i = np.arange(t)[:, None]
    j = np.arange(t)[None, :]
    return ((i // c == j // c) & (j <= i)).astype(np.float32)


def _head_variants(a):
    lo = lax.broadcasted_iota(jnp.int32, a.shape, 1) < HEAD_DIM
    rot = pltpu.roll(a, HEAD_DIM, axis=1)
    return [_bf(jnp.where(lo, a, 0.0)), _bf(jnp.where(lo, 0.0, rot)),
            _bf(jnp.where(lo, rot, 0.0)), _bf(jnp.where(lo, 0.0, a))]


def _mixer_kernel(*refs, tile, chunk, carry, d_model, layer):
    n_chunks = tile // chunk
    keys = WINDOW + chunk
    key_pad = 2 * LANES
    assert keys < key_pad and tile >= WINDOW
    levels = _levels(chunk)
    n_hg = d_model // HG_DK
    att_w = N_HEADS * HEAD_DIM
    n_pairs = N_HEADS // 2
    ppk = n_pairs // N_KV_HEADS
    rows_g = ppk * chunk
    n_var = WINDOW // chunk + 1 if carry else 1
    f32 = jnp.float32

    refs = list(refs)
    (x_ref, nw_ref, win_ref, wout_ref, sinks_ref, table_ref, bucket_ref, lbp_ref, gn_ref,
     tri_ref, lmask_ref, hmask_ref) = refs[:12]
    refs = refs[12:]
    if not carry:
        hk_ref, hv_ref, si_ref = refs[:3]
        refs = refs[3:]
    y_ref, ko_ref, vo_ref, so_ref = refs[:4]
    refs = refs[4:]
    (bias_s, q4_s, kv_s, kx_s, vx_s, hq_s, kk_s, hi_s, lf_s, bc_s, og_s, ga_s, gh_s, att_s, hg_s,
     u_s, sb_s, p_s, a_s, qe_s) = refs[:20]
    if carry:
        st_s = refs[20]

    first = (pl.program_id(0) == 0) & (pl.program_id(1) == 0)
    t_idx = pl.program_id(1)

    @pl.when(first)
    def _build_bias():
        bucket = bucket_ref[...]
        lane = lax.broadcasted_iota(jnp.int32, bucket.shape, 1)
        for head in range(N_HEADS):
            acc = jnp.full(bucket.shape, -jnp.inf, f32)
            for b in range(N_BUCKETS):
                acc = jnp.where(bucket == b, table_ref[b, head], acc)
            acc = jnp.where(bucket == N_BUCKETS, sinks_ref[head], acc)
            pair, side = divmod(head, 2)
            kvh, j = divmod(pair, ppk)
            for var in range(n_var):
                n_invalid = WINDOW - var * chunk if carry else 0
                bias_s[var, 2 * kvh + side, j * chunk:(j + 1) * chunk, :] = jnp.where(lane < n_invalid, -jnp.inf, acc)

    if carry:
        @pl.when(t_idx == 0)
        def _reset():
            kx_s[:, 0, 0:WINDOW, :] = jnp.zeros((4, WINDOW, LANES), jnp.bfloat16)
            vx_s[:, 0, 0:WINDOW, :] = jnp.zeros((4, WINDOW, LANES), jnp.bfloat16)
            st_s[...] = jnp.zeros_like(st_s)

        @pl.when(t_idx > 0)
        def _shift():
            kx_s[:, 0, 0:WINDOW, :] = kx_s[:, 0, tile:tile + WINDOW, :]
            vx_s[:, 0, 0:WINDOW, :] = vx_s[:, 0, tile:tile + WINDOW, :]

    x = x_ref[0]
    h = _bf(_rms(x, nw_ref[...]))

    def proj(i0, width):
        return _dot(h, win_ref[:, i0:i0 + width])

    off = 0
    q = _bf(proj(off, att_w) * ATTN_SCALE)
    for c in range(n_chunks):
        for j in range(n_pairs):
            q4_s[c, j] = q[c * chunk:(c + 1) * chunk, j * LANES:(j + 1) * LANES]
    off += att_w
    kv = proj(off, 2 * KV_WIDTH)
    kv_s[...] = kv
    off += 2 * KV_WIDTH
    k_var = _head_variants(kv[:, 0:KV_WIDTH])
    v_var = _head_variants(kv[:, KV_WIDTH:2 * KV_WIDTH])
    if carry:
        for g in range(4):
            kx_s[g, 0, WINDOW:WINDOW + tile, :] = k_var[g]
            vx_s[g, 0, WINDOW:WINDOW + tile, :] = v_var[g]
    else:
        for c in range(n_chunks):
            hk = _head_variants(hk_ref[c])
            hv = _head_variants(hv_ref[c])
            for g in range(4):
                kx_s[g, c, 0:WINDOW, :] = hk[g]
                vx_s[g, c, 0:WINDOW, :] = hv[g]
                kx_s[g, c, WINDOW:keys, :] = k_var[g][c * chunk:(c + 1) * chunk]
                vx_s[g, c, WINDOW:keys, :] = v_var[g][c * chunk:(c + 1) * chunk]
    hq_s[...] = proj(off, d_model)
    off += d_model
    lbp = lbp_ref[...]
    e = jnp.exp(lbp - jnp.max(lbp, axis=0, keepdims=True))
    lb = jnp.sum(e[:layer + 1], axis=0, keepdims=True) / jnp.sum(e, axis=0, keepdims=True)
    f = lb + (1.0 - lb) * jax.nn.sigmoid(proj(off, d_model))
    off += d_model
    kk_s[...] = 1.0 - f
    lf = jnp.log(f)
    lf_s[...] = lf
    l_hi = _bf(lf)
    r1 = lf - l_hi.astype(f32)
    l_mid = _bf(r1)
    l_lo = _bf(r1 - l_mid.astype(f32))
    tri = tri_ref[...]
    bc_s[...] = _dot(tri, l_hi) + _dot(tri, l_mid) + _dot(tri, l_lo)
    hi_s[...] = _bf(proj(off, d_model))
    off += d_model
    og_s[...] = jax.nn.silu(proj(off, d_model))
    off += d_model
    ga_s[...] = jax.nn.sigmoid(proj(off, d_model))
    off += d_model
    gh_s[...] = jax.nn.sigmoid(proj(off, d_model))

    zpad = jnp.zeros((key_pad - keys, LANES), jnp.bfloat16)

    def window(ref, g, c):
        w = ref[g, 0, c * chunk:c * chunk + keys, :] if carry else ref[g, c]
        return jnp.concatenate([w, zpad], axis=0)

    for c in range(n_chunks):
        var = jnp.minimum(t_idx * n_chunks + c, n_var - 1) if carry else 0
        for kvh in range(N_KV_HEADS):
            qg = q4_s[c, kvh * ppk:(kvh + 1) * ppk].reshape(rows_g, LANES)
            for side in range(2):
                g = 2 * kvh + side
                s = _dot(qg, window(kx_s, g, c), _NT) + bias_s[var, g]
                ex = jnp.exp(s - jnp.max(s, axis=-1, keepdims=True))
                p_s[c, g] = _bf(ex * (1.0 / jnp.sum(ex, axis=-1, keepdims=True)))
    for c in range(n_chunks):
        for kvh in range(N_KV_HEADS):
            o = (_dot(p_s[c, 2 * kvh], window(vx_s, 2 * kvh, c))
                 + _dot(p_s[c, 2 * kvh + 1], window(vx_s, 2 * kvh + 1, c)))
            for j in range(ppk):
                col = (kvh * ppk + j) * LANES
                att_s[c * chunk:(c + 1) * chunk, col:col + LANES] = o[j * chunk:(j + 1) * chunk]

    gn = gn_ref[...]
    small = [m for m in levels if m < 8]

    def unit(c, hd):
        return slice(c * chunk, (c + 1) * chunk), slice(hd * HG_DK, (hd + 1) * HG_DK)

    def bc_row(c, r, cols):
        return bc_s[c * chunk + r:c * chunk + r + 1, cols]

    def level_operand(m, c, hd):
        rows, cols = unit(c, hd)
        q = hq_s[rows, cols]
        kk = kk_s[rows, cols]
        if m >= 8:
            bc = bc_s[rows, cols]
            parts = []
            for base in range(0, chunk, 2 * m):
                ref = bc_row(c, base + m - 1, cols)
                lo, hi = slice(base, base + m), slice(base + m, base + 2 * m)
                parts += [kk[lo] * jnp.exp(ref - bc[lo]), q[hi] * jnp.exp(bc[hi] - ref)]
            return _bf(jnp.concatenate(parts, axis=0))
        upper = hmask_ref[small.index(m)] > 0.5
        if m == 1:
            return _bf(jnp.where(upper, q * (1.0 - kk), kk))
        if m == 4:
            bc = bc_s[rows, cols]
            ref_row = bc.reshape(chunk // 8, 8, HG_DK)[:, 3:4, :]
            ref = jnp.broadcast_to(ref_row, (chunk // 8, 8, HG_DK)).reshape(chunk, HG_DK)
            arg = -jnp.abs(bc - ref)
        else:
            lfh = lf_s[rows, cols]
            nxt = pltpu.roll(lfh, chunk - 1, axis=0)
            prv = pltpu.roll(lfh, 1, axis=0)
            n_small = len(small)
            arg = hmask_ref[n_small] * nxt + hmask_ref[n_small + 1] * lfh + hmask_ref[n_small + 2] * prv
        return _bf(jnp.where(upper, q, kk) * jnp.exp(arg))

    for c in range(n_chunks):
        for hd in range(n_hg):
            rows, cols = unit(c, hd)
            bc = bc_s[rows, cols]
            k_dec = _bf(kk_s[rows, cols] * jnp.exp(bc_row(c, chunk - 1, cols) - bc))
            u_s[c, hd] = _dot(hi_s[rows, cols], k_dec, _TN)
            qe_s[rows, cols] = _bf(hq_s[rows, cols] * jnp.exp(bc))

    for hd in range(n_hg):
        if carry:
            s = st_s[hd]
        for c in range(n_chunks):
            rows, cols = unit(c, hd)
            decay = jnp.exp(bc_row(c, chunk - 1, cols))
            if carry:
                sb_s[c, hd] = _bf(s)
                s = s * decay + u_s[c, hd]
            else:
                s0 = si_ref[c, hd].T
                sb_s[c, hd] = _bf(s0)
                so_ref[c, hd] = (s0 * decay + u_s[c, hd]).T
        if carry:
            st_s[hd] = s

    total_decay = jnp.concatenate([bc_row(c, chunk - 1, slice(None)) for c in range(n_chunks)], axis=0)
    mild = jnp.max(-total_decay) <= DECAY_GUARD

    @pl.when(mild)
    def _direct():
        tril = lmask_ref[len(levels) + 1] > 0.5
        for c in range(n_chunks):
            for hd in range(n_hg):
                rows, cols = unit(c, hd)
                k_grow = _bf(kk_s[rows, cols] * jnp.exp(-bc_s[rows, cols]))
                a_s[c, hd] = _bf(jnp.where(tril, _dot(qe_s[rows, cols], k_grow, _NT), 0.0))

    @pl.when(jnp.logical_not(mild))
    def _split():
        for c in range(n_chunks):
            for hd in range(n_hg):
                rows, cols = unit(c, hd)
                a = _dot(_bf(hq_s[rows, cols]), _bf(kk_s[rows, cols]), _NT) * lmask_ref[len(levels)]
                for li, m in enumerate(levels):
                    z = level_operand(m, c, hd)
                    a = a + _dot(z, z, _NT) * lmask_ref[li]
                a_s[c, hd] = _bf(a)

    for c in range(n_chunks):
        for hd in range(n_hg):
            rows, cols = unit(c, hd)
            o = _dot(qe_s[rows, cols], sb_s[c, hd], _NT) + _dot(a_s[c, hd], hi_s[rows, cols])
            o = o * lax.rsqrt(jnp.mean(o * o, axis=-1, keepdims=True) + RMS_EPS) * gn
            hg_s[rows, cols] = o * og_s[rows, cols]

    merged = _bf(ga_s[...] * att_s[...] + gh_s[...] * hg_s[...])
    y_ref[0] = x + _dot(merged, wout_ref[...])

    if carry:
        @pl.when(t_idx == pl.num_programs(1) - 1)
        def _emit():
            ko_ref[0] = kv_s[tile - WINDOW:tile, 0:KV_WIDTH]
            vo_ref[0] = kv_s[tile - WINDOW:tile, KV_WIDTH:2 * KV_WIDTH]
            for hd in range(n_hg):
                so_ref[0, hd] = st_s[hd].T
    else:
        ko_ref[0] = kv_s[:, 0:KV_WIDTH]
        vo_ref[0] = kv_s[:, KV_WIDTH:2 * KV_WIDTH]


def _full_spec(shape):
    nd = len(shape)
    return pl.BlockSpec(shape, lambda *_: (0,) * nd)


def _weight_spec(shape):
    nd = len(shape)
    return pl.BlockSpec(shape, lambda *_: (0,) * nd, pipeline_mode=pl.Buffered(1))


def _mixer(x, hist, norm_w, w_in, w_out, sinks, table, lb_params, gnorm_w, *, layer, carry, tile, chunk):
    nb, s, d = x.shape
    n_t = s // tile
    n_chunks = tile // chunk
    keys = WINDOW + chunk
    key_pad = 2 * LANES
    n_hg = d // HG_DK
    n_streams = nb if carry else s // chunk
    rel = (jnp.arange(key_pad, dtype=jnp.int32)[None, :] - WINDOW
           - jnp.arange(chunk, dtype=jnp.int32)[:, None])
    lane = jnp.arange(key_pad, dtype=jnp.int32)[None, :]
    bucket = jnp.where(lane < keys, _t5_bucket(rel), jnp.where(lane == keys, N_BUCKETS, N_BUCKETS + 1))
    bucket = bucket.astype(jnp.int32)
    tri = jnp.asarray(_block_tri(tile, chunk), jnp.bfloat16)
    lmask = jnp.asarray(_level_masks(chunk))
    hmask = jnp.asarray(_row_masks(chunk))
    smem =pl.BlockSpec(memory_space=pltpu.SMEM)

    in_arrays = [x, norm_w.reshape(1, d), w_in, w_out, sinks, table, bucket, lb_params,
                 gnorm_w.reshape(1, HG_DK), tri, lmask, hmask]
    in_specs = [pl.BlockSpec((1, tile, d), lambda b, t: (b, t, 0)), _full_spec((1, d)),
                _weight_spec(w_in.shape), _weight_spec(w_out.shape), smem, smem, _full_spec(bucket.shape),
                _full_spec(lb_params.shape), _full_spec((1, HG_DK)), _full_spec(tri.shape),
                _full_spec(lmask.shape), _full_spec(hmask.shape)]
    if carry:
        kv_rows = WINDOW
        st_block = (1, n_hg, HG_DK, HG_DK)
        n_win, win_rows, n_var = 1, WINDOW + tile, WINDOW // chunk + 1
    else:
        in_arrays += list(hist)
        in_specs += [_full_spec(a.shape) for a in hist]
        kv_rows = s
        st_block = (n_streams, n_hg, HG_DK, HG_DK)
        n_win, win_rows, n_var = n_chunks, keys, 1
    out_shape = [jax.ShapeDtypeStruct((nb, s, d), jnp.float32),
                 jax.ShapeDtypeStruct((nb, kv_rows, KV_WIDTH), jnp.float32),
                 jax.ShapeDtypeStruct((nb, kv_rows, KV_WIDTH), jnp.float32),
                 jax.ShapeDtypeStruct((n_streams, n_hg, HG_DK, HG_DK), jnp.float32)]
    out_specs = [pl.BlockSpec((1, tile, d), lambda b, t: (b, t, 0)),
                 pl.BlockSpec((1, kv_rows, KV_WIDTH), lambda b, t: (b, 0, 0)),
                 pl.BlockSpec((1, kv_rows, KV_WIDTH), lambda b, t: (b, 0, 0)),
                 pl.BlockSpec(st_block, lambda b, t: (b, 0, 0, 0))]
    f32, bf16 = jnp.float32, jnp.bfloat16
    rows_g = (N_HEADS // 2 // N_KV_HEADS) * chunk
    scratch = [pltpu.VMEM((n_var, 4, rows_g, key_pad), f32),
               pltpu.VMEM((n_chunks, N_HEADS // 2, chunk, LANES), bf16),
               pltpu.VMEM((tile, 2 * KV_WIDTH), f32),
               pltpu.VMEM((4, n_win, win_rows, LANES), bf16),
               pltpu.VMEM((4, n_win, win_rows, LANES), bf16),
               pltpu.VMEM((tile, d), f32), pltpu.VMEM((tile, d), f32),
               pltpu.VMEM((tile, d), bf16)]
    scratch += [pltpu.VMEM((tile, d), f32) for _ in range(7)]
    scratch += [pltpu.VMEM((n_chunks, n_hg, HG_DK, HG_DK), f32),
                pltpu.VMEM((n_chunks, n_hg, HG_DK, HG_DK), bf16),
                pltpu.VMEM((n_chunks, 4, rows_g, key_pad), bf16),
                pltpu.VMEM((n_chunks, n_hg, chunk, chunk), bf16),
                pltpu.VMEM((tile, d), bf16)]
    if carry:
        scratch += [pltpu.VMEM((n_hg, HG_DK, HG_DK), f32)]
    kern = functools.partial(_mixer_kernel, tile=tile, chunk=chunk, carry=carry, d_model=d, layer=layer)
    return pl.pallas_call(
        kern,
        grid=(nb, n_t),
        in_specs=in_specs,
        out_specs=out_specs,
        out_shape=out_shape,
        scratch_shapes=scratch,
        compiler_params=pltpu.CompilerParams(
            dimension_semantics=("arbitrary", "arbitrary"), vmem_limit_bytes=VMEM_LIMIT_BYTES),
        name="mixer_prompt" if carry else "mixer_sample",
    )(*in_arrays)


def _ffn_kernel(x_ref, nw_ref, wgu_ref, wd_ref, fw_ref, y_ref, act_s, *, d_ff, col_tile, final_norm):
    x = x_ref[...]
    h = _bf(_rms(x, nw_ref[...]))
    for j in range(d_ff // col_tile):
        g = _dot(h, wgu_ref[:, j * col_tile:(j + 1) * col_tile])
        u = _dot(h, wgu_ref[:, d_ff + j * col_tile:d_ff + (j + 1) * col_tile])
        act_s[:, j * col_tile:(j + 1) * col_tile] = _bf(jax.nn.silu(g) * u)
    y = x + _dot(act_s[...], wd_ref[...])
    if final_norm:
        y = _rms(y, fw_ref[...])
    y_ref[...] = y


def _ffn(x, norm_w, w_gate_up, w_down, final_w, *, tile, final_norm):
    n, d = x.shape
    d_ff = w_down.shape[0]
    kern = functools.partial(_ffn_kernel, d_ff=d_ff, col_tile=2 * LANES, final_norm=final_norm)
    return pl.pallas_call(
        kern,
        grid=(n // tile,),
        in_specs=[pl.BlockSpec((tile, d), lambda i: (i, 0)), _full_spec((1, d)),
                  _weight_spec(w_gate_up.shape), _weight_spec(w_down.shape), _full_spec((1, d))],
        out_specs=pl.BlockSpec((tile, d), lambda i: (i, 0)),
        out_shape=jax.ShapeDtypeStruct((n, d), jnp.float32),
        scratch_shapes=[pltpu.VMEM((tile, d_ff), jnp.bfloat16)],
        compiler_params=pltpu.CompilerParams(
            dimension_semantics=("arbitrary",), vmem_limit_bytes=VMEM_LIMIT_BYTES),
        name="ffn",
    )(x, norm_w.reshape(1, d), w_gate_up, w_down, final_w.reshape(1, d))


def kernel(x_prompt, x_sample, cache_k, cache_v, state_hgrn, norm_mix, w_in, w_out, attn_sinks, rel_bias_table,
           hgrn_lb, hgrn_norm, norm_ffn, w_gate_up, w_down, norm_final):
    depth = w_in.shape[0]
    batch, seq, d = x_prompt.shape
    dec_batch, dec_seq, _ = x_sample.shape
    xp = x_prompt
    xs = x_sample.reshape(1, dec_batch * dec_seq, d)
    outs = [[] for _ in range(6)]
    for l in range(depth):
        w_in_l, w_out_l = _bf(w_in[l]), _bf(w_out[l])
        w_gu_l, w_d_l = _bf(w_gate_up[l]), _bf(w_down[l])
        shared = (norm_mix[l], w_in_l, w_out_l, attn_sinks[l], rel_bias_table, hgrn_lb, hgrn_norm[l])
        xp, kp, vp, sp = _mixer(xp, None, *shared, layer=l, carry=True, tile=4 * CHUNK, chunk=CHUNK)
        hist = (cache_k[l].reshape(dec_batch, WINDOW, KV_WIDTH), cache_v[l].reshape(dec_batch, WINDOW, KV_WIDTH),
                state_hgrn[l])
        xs, ks, vs, ss = _mixer(xs, hist, *shared, layer=l, carry=False, tile=dec_batch * dec_seq, chunk=dec_seq)
        last = l == depth - 1
        xp = _ffn(xp.reshape(batch * seq, d), norm_ffn[l], w_gu_l, w_d_l, norm_final,
                  tile=512, final_norm=last).reshape(batch, seq, d)
        xs = _ffn(xs.reshape(dec_batch * dec_seq, d), norm_ffn[l], w_gu_l, w_d_l, norm_final,
                  tile=dec_batch * dec_seq, final_norm=last).reshape(1, dec_batch * dec_seq, d)
        ks = jnp.concatenate([hist[0][:, dec_seq:], ks.reshape(dec_batch, dec_seq, KV_WIDTH)], axis=1)
        vs = jnp.concatenate([hist[1][:, dec_seq:], vs.reshape(dec_batch, dec_seq, KV_WIDTH)], axis=1)
        kv_shape = (-1, WINDOW, N_KV_HEADS, HEAD_DIM)
        for acc, val in zip(outs, (kp.reshape(kv_shape), vp.reshape(kv_shape), sp,
                                   ks.reshape(kv_shape), vs.reshape(kv_shape), ss)):
            acc.append(val)
    return (xp, xs.reshape(dec_batch, dec_seq, d)) + tuple(jnp.stack(o) for o in outs)
```

```python
import functools
import math

import numpy as np
import jax
import jax.numpy as jnp
from jax import lax
from jax.experimental import pallas as pl
from jax.experimental.pallas import tpu as pltpu

CHUNK = 64
N_HEADS = 16
N_KV_HEADS = 2
HEAD_DIM = 64
GROUP = N_HEADS // N_KV_HEADS
KV_WIDTH = N_KV_HEADS * HEAD_DIM
WINDOW = 128
ATTN_SCALE = HEAD_DIM ** -0.5
N_BUCKETS = 32
MAX_DISTANCE = 128
HG_DK = 128
RMS_EPS = 1e-6
DECAY_GUARD = 60.0

LANES = 128
VMEM_LIMIT_BYTES = 56 * 1024 * 1024

_NT = (((1,), (1,)), ((), ()))
_TN = (((0,), (0,)), ((), ()))


def _bf(x):
    return x.astype(jnp.bfloat16)


def _dot(a, b, dims=None):
    if dims is None:
        return jnp.dot(a, b, preferred_element_type=jnp.float32)
    return lax.dot_general(a, b, dims, preferred_element_type=jnp.float32)


def _rms(x, w):
    return x * lax.rsqrt(jnp.mean(x * x, axis=-1, keepdims=True) + RMS_EPS) * w


def _t5_bucket(rel):
    nb = N_BUCKETS // 2
    max_exact = nb // 2
    ret = jnp.where(rel > 0, nb, 0)
    n = jnp.abs(rel)
    nf = jnp.maximum(n, 1).astype(jnp.float32)
    large = max_exact + (jnp.log(nf / max_exact) / math.log(MAX_DISTANCE / max_exact)
                         * (nb - max_exact)).astype(jnp.int32)
    large = jnp.minimum(large, nb - 1)
    return ret + jnp.where(n < max_exact, n, large)


def _levels(c):
    out, m = [], c // 2
    while m >= 1:
        out.append(m)
        m //= 2
    return out


def _level_masks(c):
    t = np.arange(c)[:, None]
    s = np.arange(c)[None, :]
    masks = []
    for m in _levels(c):
        masks.append((t // (2 * m) == s // (2 * m)) & ((t // m) % 2 == 1) & ((s // m) % 2 == 0))
    masks += [t == s, s <= t]
    return np.stack(masks).astype(np.float32)


def _row_masks(c):
    r = np.arange(c)
    rows = [(r // m) % 2 == 1 for m in _levels(c) if m < 8]
    rows += [r % 4 == 0, r % 4 >= 2, r % 4 == 3]
    return np.repeat(np.stack(rows).astype(np.float32)[:, :, None], LANES, axis=2)


def _block_tri(t, c):
    i = np.arange(t)[:, None]
    j = np.arange(t)[None, :]
    return ((i // c == j // c) & (j <= i)).astype(np.float32)


def _head_variants(a):
    lo = lax.broadcasted_iota(jnp.int32, a.shape, 1) < HEAD_DIM
    rot = pltpu.roll(a, HEAD_DIM, axis=1)
    return [_bf(jnp.where(lo, a, 0.0)), _bf(jnp.where(lo, 0.0, rot)),
            _bf(jnp.where(lo, rot, 0.0)), _bf(jnp.where(lo, 0.0, a))]


def _mixer_kernel(*refs, tile, chunk, carry, d_model, layer):
    n_chunks = tile // chunk
    keys = WINDOW + chunk
    key_pad = 2 * LANES
    assert keys < key_pad and tile >= WINDOW
    levels = _levels(chunk)
    n_hg = d_model // HG_DK
    att_w = N_HEADS * HEAD_DIM
    n_pairs = N_HEADS // 2
    ppk = n_pairs // N_KV_HEADS
    rows_g = ppk * chunk
    n_var = WINDOW // chunk + 1 if carry else 1
    f32 = jnp.float32

    refs = list(refs)
    (x_ref, nw_ref, win_ref, wout_ref, sinks_ref, table_ref, bucket_ref, lbp_ref, gn_ref,
     tri_ref, lmask_ref, hmask_ref) = refs[:12]
    refs = refs[12:]
    if not carry:
        hk_ref, hv_ref, si_ref = refs[:3]
        refs = refs[3:]
    y_ref, ko_ref, vo_ref, so_ref = refs[:4]
    refs = refs[4:]
    (bias_s, q4_s, kv_s, kx_s, vx_s, hq_s, kk_s, hi_s, lf_s, bc_s, og_s, ga_s, gh_s, att_s, hg_s,
     u_s, sb_s, p_s, a_s, qe_s) = refs[:20]
    if carry:
        st_s = refs[20]

    first = (pl.program_id(0) == 0) & (pl.program_id(1) == 0)
    t_idx = pl.program_id(1)

    @pl.when(first)
    def _build_bias():
        bucket = bucket_ref[...]
        lane = lax.broadcasted_iota(jnp.int32, bucket.shape, 1)
        for head in range(N_HEADS):
            acc = jnp.full(bucket.shape, -jnp.inf, f32)
            for b in range(N_BUCKETS):
                acc = jnp.where(bucket == b, table_ref[b, head], acc)
            acc = jnp.where(bucket == N_BUCKETS, sinks_ref[head], acc)
            pair, side = divmod(head, 2)
            kvh, j = divmod(pair, ppk)
            for var in range(n_var):
                n_invalid = WINDOW - var * chunk if carry else 0
                bias_s[var, 2 * kvh + side, j * chunk:(j + 1) * chunk, :] = jnp.where(lane < n_invalid, -jnp.inf, acc)

    if carry:
        @pl.when(t_idx == 0)
        def _reset():
            kx_s[:, 0, 0:WINDOW, :] = jnp.zeros((4, WINDOW, LANES), jnp.bfloat16)
            vx_s[:, 0, 0:WINDOW, :] = jnp.zeros((4, WINDOW, LANES), jnp.bfloat16)
            st_s[...] = jnp.zeros_like(st_s)

        @pl.when(t_idx > 0)
        def _shift():
            kx_s[:, 0, 0:WINDOW, :] = kx_s[:, 0, tile:tile + WINDOW, :]
            vx_s[:, 0, 0:WINDOW, :] = vx_s[:, 0, tile:tile + WINDOW, :]

    x = x_ref[0]
    h = _bf(_rms(x, nw_ref[...]))

    def proj(i0, width):
        return _dot(h, win_ref[:, i0:i0 + width])

    off = 0
    q = _bf(proj(off, att_w) * ATTN_SCALE)
    for c in range(n_chunks):
        for j in range(n_pairs):
            q4_s[c, j] = q[c * chunk:(c + 1) * chunk, j * LANES:(j + 1) * LANES]
    off += att_w
    kv = proj(off, 2 * KV_WIDTH)
    kv_s[...] = kv
    off += 2 * KV_WIDTH
    k_var = _head_variants(kv[:, 0:KV_WIDTH])
    v_var = _head_variants(kv[:, KV_WIDTH:2 * KV_WIDTH])
    if carry:
        for g in range(4):
            kx_s[g, 0, WINDOW:WINDOW + tile, :] = k_var[g]
            vx_s[g, 0, WINDOW:WINDOW + tile, :] = v_var[g]
    else:
        for c in range(n_chunks):
            hk = _head_variants(hk_ref[c])
            hv = _head_variants(hv_ref[c])
            for g in range(4):
                kx_s[g, c, 0:WINDOW, :] = hk[g]
                vx_s[g, c, 0:WINDOW, :] = hv[g]
                kx_s[g, c, WINDOW:keys, :] = k_var[g][c * chunk:(c + 1) * chunk]
                vx_s[g, c, WINDOW:keys, :] = v_var[g][c * chunk:(c + 1) * chunk]
    hq_s[...] = proj(off, d_model)
    off += d_model
    lbp = lbp_ref[...]
    e = jnp.exp(lbp - jnp.max(lbp, axis=0, keepdims=True))
    lb = jnp.sum(e[:layer + 1], axis=0, keepdims=True) / jnp.sum(e, axis=0, keepdims=True)
    f = lb + (1.0 - lb) * jax.nn.sigmoid(proj(off, d_model))
    off += d_model
    kk_s[...] = 1.0 - f
    lf = jnp.log(f)
    lf_s[...] = lf
    l_hi = _bf(lf)
    r1 = lf - l_hi.astype(f32)
    l_mid = _bf(r1)
    l_lo = _bf(r1 - l_mid.astype(f32))
    tri = tri_ref[...]
    bc_s[...] = _dot(tri, l_hi) + _dot(tri, l_mid) + _dot(tri, l_lo)
    hi_s[...] = _bf(proj(off, d_model))
    off += d_model
    og_s[...] = jax.nn.silu(proj(off, d_model))
    off += d_model
    ga_s[...] = jax.nn.sigmoid(proj(off, d_model))
    off += d_model
    gh_s[...] = jax.nn.sigmoid(proj(off, d_model))

    zpad = jnp.zeros((key_pad - keys, LANES), jnp.bfloat16)

    def window(ref, g, c):
        w = ref[g, 0, c * chunk:c * chunk + keys, :] if carry else ref[g, c]
        return jnp.concatenate([w, zpad], axis=0)

    for c in range(n_chunks):
        var = jnp.minimum(t_idx * n_chunks + c, n_var - 1) if carry else 0
        for kvh in range(N_KV_HEADS):
            qg = q4_s[c, kvh * ppk:(kvh + 1) * ppk].reshape(rows_g, LANES)
            for side in range(2):
                g = 2 * kvh + side
                s = _dot(qg, window(kx_s, g, c), _NT) + bias_s[var, g]
                ex = jnp.exp(s - jnp.max(s, axis=-1, keepdims=True))
                p_s[c, g] = _bf(ex * (1.0 / jnp.sum(ex, axis=-1, keepdims=True)))
    for c in range(n_chunks):
        for kvh in range(N_KV_HEADS):
            o = (_dot(p_s[c, 2 * kvh], window(vx_s, 2 * kvh, c))
                 + _dot(p_s[c, 2 * kvh + 1], window(vx_s, 2 * kvh + 1, c)))
            for j in range(ppk):
                col = (kvh * ppk + j) * LANES
                att_s[c * chunk:(c + 1) * chunk, col:col + LANES] = o[j * chunk:(j + 1) * chunk]

    gn = gn_ref[...]
    small = [m for m in levels if m < 8]

    def unit(c, hd):
        return slice(c * chunk, (c + 1) * chunk), slice(hd * HG_DK, (hd + 1) * HG_DK)

    def bc_row(c, r, cols):
        return bc_s[c * chunk + r:c * chunk + r + 1, cols]

    def level_operand(m, c, hd):
        rows, cols = unit(c, hd)
        q = hq_s[rows, cols]
        kk = kk_s[rows, cols]
        if m >= 8:
            bc = bc_s[rows, cols]
            parts = []
            for base in range(0, chunk, 2 * m):
                ref = bc_row(c, base + m - 1, cols)
                lo, hi = slice(base, base + m), slice(base + m, base + 2 * m)
                parts += [kk[lo] * jnp.exp(ref - bc[lo]), q[hi] * jnp.exp(bc[hi] - ref)]
            return _bf(jnp.concatenate(parts, axis=0))
        upper = hmask_ref[small.index(m)] > 0.5
        if m == 1:
            return _bf(jnp.where(upper, q * (1.0 - kk), kk))
        if m == 4:
            bc = bc_s[rows, cols]
            ref_row = bc.reshape(chunk // 8, 8, HG_DK)[:, 3:4, :]
            ref = jnp.broadcast_to(ref_row, (chunk // 8, 8, HG_DK)).reshape(chunk, HG_DK)
            arg = -jnp.abs(bc - ref)
        else:
            lfh = lf_s[rows, cols]
            nxt = pltpu.roll(lfh, chunk - 1, axis=0)
            prv = pltpu.roll(lfh, 1, axis=0)
            n_small = len(small)
            arg = hmask_ref[n_small] * nxt + hmask_ref[n_small + 1] * lfh + hmask_ref[n_small + 2] * prv
        return _bf(jnp.where(upper, q, kk) * jnp.exp(arg))

    for c in range(n_chunks):
        for hd in range(n_hg):
            rows, cols = unit(c, hd)
            bc = bc_s[rows, cols]
            k_dec = _bf(kk_s[rows, cols] * jnp.exp(bc_row(c, chunk - 1, cols) - bc))
            u_s[c, hd] = _dot(hi_s[rows, cols], k_dec, _TN)
            qe_s[rows, cols] = _bf(hq_s[rows, cols] * jnp.exp(bc))

    for hd in range(n_hg):
        if carry:
            s = st_s[hd]
        for c in range(n_chunks):
            rows, cols = unit(c, hd)
            decay = jnp.exp(bc_row(c, chunk - 1, cols))
            if carry:
                sb_s[c, hd] = _bf(s)
                s = s * decay + u_s[c, hd]
            else:
                s0 = si_ref[c, hd].T
                sb_s[c, hd] = _bf(s0)
                so_ref[c, hd] = (s0 * decay + u_s[c, hd]).T
        if carry:
            st_s[hd] = s

    total_decay = jnp.concatenate([bc_row(c, chunk - 1, slice(None)) for c in range(n_chunks)], axis=0)
    mild = jnp.max(-total_decay) <= DECAY_GUARD

    @pl.when(mild)
    def _direct():
        tril = lmask_ref[len(levels) + 1] > 0.5
        for c in range(n_chunks):
            for hd in range(n_hg):
                rows, cols = unit(c, hd)
                k_grow = _bf(kk_s[rows, cols] * jnp.exp(-bc_s[rows, cols]))
                a_s[c, hd] = _bf(jnp.where(tril, _dot(qe_s[rows, cols], k_grow, _NT), 0.0))

    @pl.when(jnp.logical_not(mild))
    def _split():
        for c in range(n_chunks):
            for hd in range(n_hg):
                rows, cols = unit(c, hd)
                a = _dot(_bf(hq_s[rows, cols]), _bf(kk_s[rows, cols]), _NT) * lmask_ref[len(levels)]
                for li, m in enumerate(levels):
                    z = level_operand(m, c, hd)
                    a = a + _dot(z, z, _NT) * lmask_ref[li]
                a_s[c, hd] = _bf(a)

    for c in range(n_chunks):
        for hd in range(n_hg):
            rows, cols = unit(c, hd)
            o = _dot(qe_s[rows, cols], sb_s[c, hd], _NT) + _dot(a_s[c, hd], hi_s[rows, cols])
            o = o * lax.rsqrt(jnp.mean(o * o, axis=-1, keepdims=True) + RMS_EPS) * gn
            hg_s[rows, cols] = o * og_s[rows, cols]

    merged = _bf(ga_s[...] * att_s[...] + gh_s[...] * hg_s[...])
    y_ref[0] = x + _dot(merged, wout_ref[...])

    if carry:
        @pl.when(t_idx == pl.num_programs(1) - 1)
        def _emit():
            ko_ref[0] = kv_s[tile - WINDOW:tile, 0:KV_WIDTH]
            vo_ref[0] = kv_s[tile - WINDOW:tile, KV_WIDTH:2 * KV_WIDTH]
            for hd in range(n_hg):
                so_ref[0, hd] = st_s[hd].T
    else:
        ko_ref[0] = kv_s[:, 0:KV_WIDTH]
        vo_ref[0] = kv_s[:, KV_WIDTH:2 * KV_WIDTH]


def _full_spec(shape):
    nd = len(shape)
    return pl.BlockSpec(shape, lambda *_: (0,) * nd)


def _weight_spec(shape):
    nd = len(shape)
    return pl.BlockSpec(shape, lambda *_: (0,) * nd, pipeline_mode=pl.Buffered(1))


def _mixer(x, hist, norm_w, w_in, w_out, sinks, table, lb_params, gnorm_w, *, layer, carry, tile, chunk):
    nb, s, d = x.shape
    n_t = s // tile
    n_chunks = tile // chunk
    keys = WINDOW + chunk
    key_pad = 2 * LANES
    n_hg = d // HG_DK
    n_streams = nb if carry else s // chunk
    rel = (jnp.arange(key_pad, dtype=jnp.int32)[None, :] - WINDOW
           - jnp.arange(chunk, dtype=jnp.int32)[:, None])
    lane = jnp.arange(key_pad, dtype=jnp.int32)[None, :]
    bucket = jnp.where(lane < keys, _t5_bucket(rel), jnp.where(lane == keys, N_BUCKETS, N_BUCKETS + 1))
    bucket = bucket.astype(jnp.int32)
    tri = jnp.asarray(_block_tri(tile, chunk), jnp.bfloat16)
    lmask = jnp.asarray(_level_masks(chunk))
    hmask = jnp.asarray(_row_masks(chunk))
    smem =pl.BlockSpec(memory_space=pltpu.SMEM)

    in_arrays = [x, norm_w.reshape(1, d), w_in, w_out, sinks, table, bucket, lb_params,
                 gnorm_w.reshape(1, HG_DK), tri, lmask, hmask]
    in_specs = [pl.BlockSpec((1, tile, d), lambda b, t: (b, t, 0)), _full_spec((1, d)),
                _weight_spec(w_in.shape), _weight_spec(w_out.shape), smem, smem, _full_spec(bucket.shape),
                _full_spec(lb_params.shape), _full_spec((1, HG_DK)), _full_spec(tri.shape),
                _full_spec(lmask.shape), _full_spec(hmask.shape)]
    if carry:
        kv_rows = WINDOW
        st_block = (1, n_hg, HG_DK, HG_DK)
        n_win, win_rows, n_var = 1, WINDOW + tile, WINDOW // chunk + 1
    else:
        in_arrays += list(hist)
        in_specs += [_full_spec(a.shape) for a in hist]
        kv_rows = s
        st_block = (n_streams, n_hg, HG_DK, HG_DK)
        n_win, win_rows, n_var = n_chunks, keys, 1
    out_shape = [jax.ShapeDtypeStruct((nb, s, d), jnp.float32),
                 jax.ShapeDtypeStruct((nb, kv_rows, KV_WIDTH), jnp.float32),
                 jax.ShapeDtypeStruct((nb, kv_rows, KV_WIDTH), jnp.float32),
                 jax.ShapeDtypeStruct((n_streams, n_hg, HG_DK, HG_DK), jnp.float32)]
    out_specs = [pl.BlockSpec((1, tile, d), lambda b, t: (b, t, 0)),
                 pl.BlockSpec((1, kv_rows, KV_WIDTH), lambda b, t: (b, 0, 0)),
                 pl.BlockSpec((1, kv_rows, KV_WIDTH), lambda b, t: (b, 0, 0)),
                 pl.BlockSpec(st_block, lambda b, t: (b, 0, 0, 0))]
    f32, bf16 = jnp.float32, jnp.bfloat16
    rows_g = (N_HEADS // 2 // N_KV_HEADS) * chunk
    scratch = [pltpu.VMEM((n_var, 4, rows_g, key_pad), f32),
               pltpu.VMEM((n_chunks, N_HEADS // 2, chunk, LANES), bf16),
               pltpu.VMEM((tile, 2 * KV_WIDTH), f32),
               pltpu.VMEM((4, n_win, win_rows, LANES), bf16),
               pltpu.VMEM((4, n_win, win_rows, LANES), bf16),
               pltpu.VMEM((tile, d), f32), pltpu.VMEM((tile, d), f32),
               pltpu.VMEM((tile, d), bf16)]
    scratch += [pltpu.VMEM((tile, d), f32) for _ in range(7)]
    scratch += [pltpu.VMEM((n_chunks, n_hg, HG_DK, HG_DK), f32),
                pltpu.VMEM((n_chunks, n_hg, HG_DK, HG_DK), bf16),
                pltpu.VMEM((n_chunks, 4, rows_g, key_pad), bf16),
                pltpu.VMEM((n_chunks, n_hg, chunk, chunk), bf16),
                pltpu.VMEM((tile, d), bf16)]
    if carry:
        scratch += [pltpu.VMEM((n_hg, HG_DK, HG_DK), f32)]
    kern = functools.partial(_mixer_kernel, tile=tile, chunk=chunk, carry=carry, d_model=d, layer=layer)
    return pl.pallas_call(
        kern,
        grid=(nb, n_t),
        in_specs=in_specs,
        out_specs=out_specs,
        out_shape=out_shape,
        scratch_shapes=scratch,
        compiler_params=pltpu.CompilerParams(
            dimension_semantics=("arbitrary", "arbitrary"), vmem_limit_bytes=VMEM_LIMIT_BYTES),
        name="mixer_prompt" if carry else "mixer_sample",
    )(*in_arrays)


def _ffn_kernel(x_ref, nw_ref, wgu_ref, wd_ref, fw_ref, y_ref, act_s, *, d_ff, col_tile, final_norm):
    x = x_ref[...]
    h = _bf(_rms(x, nw_ref[...]))
    for j in range(d_ff // col_tile):
        g = _dot(h, wgu_ref[:, j * col_tile:(j + 1) * col_tile])
        u = _dot(h, wgu_ref[:, d_ff + j * col_tile:d_ff + (j + 1) * col_tile])
        act_s[:, j * col_tile:(j + 1) * col_tile] = _bf(jax.nn.silu(g) * u)
    y = x + _dot(act_s[...], wd_ref[...])
    if final_norm:
        y = _rms(y, fw_ref[...])
    y_ref[...] = y


def _ffn(x, norm_w, w_gate_up, w_down, final_w, *, tile, final_norm):
    n, d = x.shape
    d_ff = w_down.shape[0]
    kern = functools.partial(_ffn_kernel, d_ff=d_ff, col_tile=2 * LANES, final_norm=final_norm)
    return pl.pallas_call(
        kern,
        grid=(n // tile,),
        in_specs=[pl.BlockSpec((tile, d), lambda i: (i, 0)), _full_spec((1, d)),
                  _weight_spec(w_gate_up.shape), _weight_spec(w_down.shape), _full_spec((1, d))],
        out_specs=pl.BlockSpec((tile, d), lambda i: (i, 0)),
        out_shape=jax.ShapeDtypeStruct((n, d), jnp.float32),
        scratch_shapes=[pltpu.VMEM((tile, d_ff), jnp.bfloat16)],
        compiler_params=pltpu.CompilerParams(
            dimension_semantics=("arbitrary",), vmem_limit_bytes=VMEM_LIMIT_BYTES),
        name="ffn",
    )(x, norm_w.reshape(1, d), w_gate_up, w_down, final_w.reshape(1, d))


def kernel(x_prompt, x_sample, cache_k, cache_v, state_hgrn, norm_mix, w_in, w_out, attn_sinks, rel_bias_table,
           hgrn_lb, hgrn_norm, norm_ffn, w_gate_up, w_down, norm_final):
    depth = w_in.shape[0]
    batch, seq, d = x_prompt.shape
    dec_batch, dec_seq, _ = x_sample.shape
    xp = x_prompt
    xs = x_sample.reshape(1, dec_batch * dec_seq, d)
    outs = [[] for _ in range(6)]
    for l in range(depth):
        w_in_l, w_out_l = _bf(w_in[l]), _bf(w_out[l])
        w_gu_l, w_d_l = _bf(w_gate_up[l]), _bf(w_down[l])
        shared = (norm_mix[l], w_in_l, w_out_l, attn_sinks[l], rel_bias_table, hgrn_lb, hgrn_norm[l])
        xp, kp, vp, sp = _mixer(xp, None, *shared, layer=l, carry=True, tile=4 * CHUNK, chunk=CHUNK)
        hist = (cache_k[l].reshape(dec_batch, WINDOW, KV_WIDTH), cache_v[l].reshape(dec_batch, WINDOW, KV_WIDTH),
                state_hgrn[l])
        xs, ks, vs, ss = _mixer(xs, hist, *shared, layer=l, carry=False, tile=dec_batch * dec_seq, chunk=dec_seq)
        last = l == depth - 1
        xp = _ffn(xp.reshape(batch * seq, d), norm_ffn[l], w_gu_l, w_d_l, norm_final,
                  tile=512, final_norm=last).reshape(batch, seq, d)
        xs = _ffn(xs.reshape(dec_batch * dec_seq, d), norm_ffn[l], w_gu_l, w_d_l, norm_final,
                  tile=dec_batch * dec_seq, final_norm=last).reshape(1, dec_batch * dec_seq, d)
        ks = jnp.concatenate([hist[0][:, dec_seq:], ks.reshape(dec_batch, dec_seq, KV_WIDTH)], axis=1)
        vs = jnp.concatenate([hist[1][:, dec_seq:], vs.reshape(dec_batch, dec_seq, KV_WIDTH)], axis=1)
        kv_shape = (-1, WINDOW, N_KV_HEADS, HEAD_DIM)
        for acc, val in zip(outs, (kp.reshape(kv_shape), vp.reshape(kv_shape), sp,
                                   ks.reshape(kv_shape), vs.reshape(kv_shape), ss)):
            acc.append(val)
    return (xp, xs.reshape(dec_batch, dec_seq, d)) + tuple(jnp.stack(o) for o in outs)
```

```python
import functools
import math

import numpy as np
import jax
import jax.numpy as jnp
from jax import lax
from jax.experimental import pallas as pl
from jax.experimental.pallas import tpu as pltpu

CHUNK = 64
N_HEADS = 16
N_KV_HEADS = 2
HEAD_DIM = 64
GROUP = N_HEADS // N_KV_HEADS
KV_WIDTH = N_KV_HEADS * HEAD_DIM
WINDOW = 128
ATTN_SCALE = HEAD_DIM ** -0.5
N_BUCKETS = 32
MAX_DISTANCE = 128
HG_DK = 128
RMS_EPS = 1e-6
DECAY_GUARD = 60.0

LANES = 128
VMEM_LIMIT_BYTES = 56 * 1024 * 1024

_NT = (((1,), (1,)), ((), ()))
_TN = (((0,), (0,)), ((), ()))


def _bf(x):
    return x.astype(jnp.bfloat16)


def _dot(a, b, dims=None):
    if dims is None:
        return jnp.dot(a, b, preferred_element_type=jnp.float32)
    return lax.dot_general(a, b, dims, preferred_element_type=jnp.float32)


def _rms(x, w):
    return x * lax.rsqrt(jnp.mean(x * x, axis=-1, keepdims=True) + RMS_EPS) * w


def _t5_bucket(rel):
    nb = N_BUCKETS // 2
    max_exact = nb // 2
    ret = jnp.where(rel > 0, nb, 0)
    n = jnp.abs(rel)
    nf = jnp.maximum(n, 1).astype(jnp.float32)
    large = max_exact + (jnp.log(nf / max_exact) / math.log(MAX_DISTANCE / max_exact)
                         * (nb - max_exact)).astype(jnp.int32)
    large = jnp.minimum(large, nb - 1)
    return ret + jnp.where(n < max_exact, n, large)


def _levels(c):
    out, m = [], c // 2
    while m >= 1:
        out.append(m)
        m //= 2
    return out


def _level_masks(c):
    t = np.arange(c)[:, None]
    s = np.arange(c)[None, :]
    masks = []
    for m in _levels(c):
        masks.append((t // (2 * m) == s // (2 * m)) & ((t // m) % 2 == 1) & ((s // m) % 2 == 0))
    masks += [t == s, s <= t]
    return np.stack(masks).astype(np.float32)


def _row_masks(c):
    r = np.arange(c)
    rows = [(r // m) % 2 == 1 for m in _levels(c) if m < 8]
    rows += [r % 4 == 0, r % 4 >= 2, r % 4 == 3]
    return np.repeat(np.stack(rows).astype(np.float32)[:, :, None], LANES, axis=2)


def _block_tri(t, c):
    i = np.arange(t)[:, None]
    j = np.arange(t)[None, :]
    return ((i // c == j // c) & (j <= i)).astype(np.float32)


def _head_variants(a):
    lo = lax.broadcasted_iota(jnp.int32, a.shape, 1) < HEAD_DIM
    rot = pltpu.roll(a, HEAD_DIM, axis=1)
    return [_bf(jnp.where(lo, a, 0.0)), _bf(jnp.where(lo, 0.0, rot)),
            _bf(jnp.where(lo, rot, 0.0)), _bf(jnp.where(lo, 0.0, a))]


def _mixer_kernel(*refs, tile, chunk, carry, d_model, layer):
    n_chunks = tile // chunk
    keys = WINDOW + chunk
    key_pad = 2 * LANES
    assert keys < key_pad and tile >= WINDOW
    levels = _levels(chunk)
    n_hg = d_model // HG_DK
    att_w = N_HEADS * HEAD_DIM
    n_pairs = N_HEADS // 2
    ppk = n_pairs // N_KV_HEADS
    rows_g = ppk * chunk
    n_var = WINDOW // chunk + 1 if carry else 1
    f32 = jnp.float32

    refs = list(refs)
    (x_ref, nw_ref, win_ref, wout_ref, sinks_ref, table_ref, bucket_ref, lbp_ref, gn_ref,
     tri_ref, lmask_ref, hmask_ref) = refs[:12]
    refs = refs[12:]
    if not carry:
        hk_ref, hv_ref, si_ref = refs[:3]
        refs = refs[3:]
    y_ref, ko_ref, vo_ref, so_ref = refs[:4]
    refs = refs[4:]
    (bias_s, q4_s, kv_s, kx_s, vx_s, hq_s, kk_s, hi_s, lf_s, bc_s, og_s, ga_s, gh_s, att_s, hg_s,
     u_s, sb_s, p_s, a_s, qe_s) = refs[:20]
    if carry:
        st_s = refs[20]

    first = (pl.program_id(0) == 0) & (pl.program_id(1) == 0)
    t_idx = pl.program_id(1)

    @pl.when(first)
    def _build_bias():
        bucket = bucket_ref[...]
        lane = lax.broadcasted_iota(jnp.int32, bucket.shape, 1)
        for head in range(N_HEADS):
            acc = jnp.full(bucket.shape, -jnp.inf, f32)
            for b in range(N_BUCKETS):
                acc = jnp.where(bucket == b, table_ref[b, head], acc)
            acc = jnp.where(bucket == N_BUCKETS, sinks_ref[head], acc)
            pair, side = divmod(head, 2)
            kvh, j = divmod(pair, ppk)
            for var in range(n_var):
                n_invalid = WINDOW - var * chunk if carry else 0
                bias_s[var, 2 * kvh + side, j * chunk:(j + 1) * chunk, :] = jnp.where(lane < n_invalid, -jnp.inf, acc)

    if carry:
        @pl.when(t_idx == 0)
        def _reset():
            kx_s[:, 0, 0:WINDOW, :] = jnp.zeros((4, WINDOW, LANES), jnp.bfloat16)
            vx_s[:, 0, 0:WINDOW, :] = jnp.zeros((4, WINDOW, LANES), jnp.bfloat16)
            st_s[...] = jnp.zeros_like(st_s)

        @pl.when(t_idx > 0)
        def _shift():
            kx_s[:, 0, 0:WINDOW, :] = kx_s[:, 0, tile:tile + WINDOW, :]
            vx_s[:, 0, 0:WINDOW, :] = vx_s[:, 0, tile:tile + WINDOW, :]

    x = x_ref[0]
    h = _bf(_rms(x, nw_ref[...]))

    def proj(i0, width):
        return _dot(h, win_ref[:, i0:i0 + width])

    off = 0
    q = _bf(proj(off, att_w) * ATTN_SCALE)
    for c in range(n_chunks):
        for j in range(n_pairs):
            q4_s[c, j] = q[c * chunk:(c + 1) * chunk, j * LANES:(j + 1) * LANES]
    off += att_w
    kv = proj(off, 2 * KV_WIDTH)
    kv_s[...] = kv
    off += 2 * KV_WIDTH
    k_var = _head_variants(kv[:, 0:KV_WIDTH])
    v_var = _head_variants(kv[:, KV_WIDTH:2 * KV_WIDTH])
    if carry:
        for g in range(4):
            kx_s[g, 0, WINDOW:WINDOW + tile, :] = k_var[g]
            vx_s[g, 0, WINDOW:WINDOW + tile, :] = v_var[g]
    else:
        for c in range(n_chunks):
            hk = _head_variants(hk_ref[c])
            hv = _head_variants(hv_ref[c])
            for g in range(4):
                kx_s[g, c, 0:WINDOW, :] = hk[g]
                vx_s[g, c, 0:WINDOW, :] = hv[g]
                kx_s[g, c, WINDOW:keys, :] = k_var[g][c * chunk:(c + 1) * chunk]
                vx_s[g, c, WINDOW:keys, :] = v_var[g][c * chunk:(c + 1) * chunk]
    hq_s[...] = proj(off, d_model)
    off += d_model
    lbp = lbp_ref[...]
    e = jnp.exp(lbp - jnp.max(lbp, axis=0, keepdims=True))
    lb = jnp.sum(e[:layer + 1], axis=0, keepdims=True) / jnp.sum(e, axis=0, keepdims=True)
    f = lb + (1.0 - lb) * jax.nn.sigmoid(proj(off, d_model))
    off += d_model
    kk_s[...] = 1.0 - f
    lf = jnp.log(f)
    lf_s[...] = lf
    l_hi = _bf(lf)
    r1 = lf - l_hi.astype(f32)
    l_mid = _bf(r1)
    l_lo = _bf(r1 - l_mid.astype(f32))
    tri = tri_ref[...]
    bc_s[...] = _dot(tri, l_hi) + _dot(tri, l_mid) + _dot(tri, l_lo)
    hi_s[...] = _bf(proj(off, d_model))
    off += d_model
    og_s[...] = jax.nn.silu(proj(off, d_model))
    off += d_model
    ga_s[...] = jax.nn.sigmoid(proj(off, d_model))
    off += d_model
    gh_s[...] = jax.nn.sigmoid(proj(off, d_model))

    zpad = jnp.zeros((key_pad - keys, LANES), jnp.bfloat16)

    def window(ref, g, c):
        w = ref[g, 0, c * chunk:c * chunk + keys, :] if carry else ref[g, c]
        return jnp.concatenate([w, zpad], axis=0)

    for c in range(n_chunks):
        var = jnp.minimum(t_idx * n_chunks + c, n_var - 1) if carry else 0
        for kvh in range(N_KV_HEADS):
            qg = q4_s[c, kvh * ppk:(kvh + 1) * ppk].reshape(rows_g, LANES)
            for side in range(2):
                g = 2 * kvh + side
                s = _dot(qg, window(kx_s, g, c), _NT) + bias_s[var, g]
                ex = jnp.exp(s - jnp.max(s, axis=-1, keepdims=True))
                p_s[c, g] = _bf(ex * (1.0 / jnp.sum(ex, axis=-1, keepdims=True)))
    for c in range(n_chunks):
        for kvh in range(N_KV_HEADS):
            o = (_dot(p_s[c, 2 * kvh], window(vx_s, 2 * kvh, c))
                 + _dot(p_s[c, 2 * kvh + 1], window(vx_s, 2 * kvh + 1, c)))
            for j in range(ppk):
                col = (kvh * ppk + j) * LANES
                att_s[c * chunk:(c + 1) * chunk, col:col + LANES] = o[j * chunk:(j + 1) * chunk]

    gn = gn_ref[...]
    small = [m for m in levels if m < 8]

    def unit(c, hd):
        return slice(c * chunk, (c + 1) * chunk), slice(hd * HG_DK, (hd + 1) * HG_DK)

    def bc_row(c, r, cols):
        return bc_s[c * chunk + r:c * chunk + r + 1, cols]

    def level_operand(m, c, hd):
        rows, cols = unit(c, hd)
        q = hq_s[rows, cols]
        kk = kk_s[rows, cols]
        if m >= 8:
            bc = bc_s[rows, cols]
            parts = []
            for base in range(0, chunk, 2 * m):
                ref = bc_row(c, base + m - 1, cols)
                lo, hi = slice(base, base + m), slice(base + m, base + 2 * m)
                parts += [kk[lo] * jnp.exp(ref - bc[lo]), q[hi] * jnp.exp(bc[hi] - ref)]
            return _bf(jnp.concatenate(parts, axis=0))
        upper = hmask_ref[small.index(m)] > 0.5
        if m == 1:
            return _bf(jnp.where(upper, q * (1.0 - kk), kk))
        if m == 4:
            bc = bc_s[rows, cols]
            ref_row = bc.reshape(chunk // 8, 8, HG_DK)[:, 3:4, :]
            ref = jnp.broadcast_to(ref_row, (chunk // 8, 8, HG_DK)).reshape(chunk, HG_DK)
            arg = -jnp.abs(bc - ref)
        else:
            lfh = lf_s[rows, cols]
            nxt = pltpu.roll(lfh, chunk - 1, axis=0)
            prv = pltpu.roll(lfh, 1, axis=0)
            n_small = len(small)
            arg = hmask_ref[n_small] * nxt + hmask_ref[n_small + 1] * lfh + hmask_ref[n_small + 2] * prv
        return _bf(jnp.where(upper, q, kk) * jnp.exp(arg))

    for c in range(n_chunks):
        for hd in range(n_hg):
            rows, cols = unit(c, hd)
            bc = bc_s[rows, cols]
            k_dec = _bf(kk_s[rows, cols] * jnp.exp(bc_row(c, chunk - 1, cols) - bc))
            u_s[c, hd] = _dot(hi_s[rows, cols], k_dec, _TN)
            qe_s[rows, cols] = _bf(hq_s[rows, cols] * jnp.exp(bc))

    for hd in range(n_hg):
        if carry:
            s = st_s[hd]
        for c in range(n_chunks):
            rows, cols = unit(c, hd)
            decay = jnp.exp(bc_row(c, chunk - 1, cols))
            if carry:
                sb_s[c, hd] = _bf(s)
                s = s * decay + u_s[c, hd]
            else:
                s0 = si_ref[c, hd].T
                sb_s[c, hd] = _bf(s0)
                so_ref[c, hd] = (s0 * decay + u_s[c, hd]).T
        if carry:
            st_s[hd] = s

    total_decay = jnp.concatenate([bc_row(c, chunk - 1, slice(None)) for c in range(n_chunks)], axis=0)
    mild = jnp.max(-total_decay) <= DECAY_GUARD

    @pl.when(mild)
    def _direct():
        tril = lmask_ref[len(levels) + 1] > 0.5
        for c in range(n_chunks):
            for hd in range(n_hg):
                rows, cols = unit(c, hd)
                k_grow = _bf(kk_s[rows, cols] * jnp.exp(-bc_s[rows, cols]))
                a_s[c, hd] = _bf(jnp.where(tril, _dot(qe_s[rows, cols], k_grow, _NT), 0.0))

    @pl.when(jnp.logical_not(mild))
    def _split():
        for c in range(n_chunks):
            for hd in range(n_hg):
                rows, cols = unit(c, hd)
                a = _dot(_bf(hq_s[rows, cols]), _bf(kk_s[rows, cols]), _NT) * lmask_ref[len(levels)]
                for li, m in enumerate(levels):
                    z = level_operand(m, c, hd)
                    a = a + _dot(z, z, _NT) * lmask_ref[li]
                a_s[c, hd] = _bf(a)

    for c in range(n_chunks):
        for hd in range(n_hg):
            rows, cols = unit(c, hd)
            o = _dot(qe_s[rows, cols], sb_s[c, hd], _NT) + _dot(a_s[c, hd], hi_s[rows, cols])
            o = o * lax.rsqrt(jnp.mean(o * o, axis=-1, keepdims=True) + RMS_EPS) * gn
            hg_s[rows, cols] = o * og_s[rows, cols]

    merged = _bf(ga_s[...] * att_s[...] + gh_s[...] * hg_s[...])
    y_ref[0] = x + _dot(merged, wout_ref[...])

    if carry:
        @pl.when(t_idx == pl.num_programs(1) - 1)
        def _emit():
            ko_ref[0] = kv_s[tile - WINDOW:tile, 0:KV_WIDTH]
            vo_ref[0] = kv_s[tile - WINDOW:tile, KV_WIDTH:2 * KV_WIDTH]
            for hd in range(n_hg):
                so_ref[0, hd] = st_s[hd].T
    else:
        ko_ref[0] = kv_s[:, 0:KV_WIDTH]
        vo_ref[0] = kv_s[:, KV_WIDTH:2 * KV_WIDTH]


def _full_spec(shape):
    nd = len(shape)
    return pl.BlockSpec(shape, lambda *_: (0,) * nd)


def _weight_spec(shape):
    nd = len(shape)
    return pl.BlockSpec(shape, lambda *_: (0,) * nd, pipeline_mode=pl.Buffered(1))


def _mixer(x, hist, norm_w, w_in, w_out, sinks, table, lb_params, gnorm_w, *, layer, carry, tile, chunk):
    nb, s, d = x.shape
    n_t = s // tile
    n_chunks = tile // chunk
    keys = WINDOW + chunk
    key_pad = 2 * LANES
    n_hg = d // HG_DK
    n_streams = nb if carry else s // chunk
    rel = (jnp.arange(key_pad, dtype=jnp.int32)[None, :] - WINDOW
           - jnp.arange(chunk, dtype=jnp.int32)[:, None])
    lane = jnp.arange(key_pad, dtype=jnp.int32)[None, :]
    bucket = jnp.where(lane < keys, _t5_bucket(rel), jnp.where(lane == keys, N_BUCKETS, N_BUCKETS + 1))
    bucket = bucket.astype(jnp.int32)
    tri = jnp.asarray(_block_tri(tile, chunk), jnp.bfloat16)
    lmask = jnp.asarray(_level_masks(chunk))
    hmask = jnp.asarray(_row_masks(chunk))
    smem =pl.BlockSpec(memory_space=pltpu.SMEM)

    in_arrays = [x, norm_w.reshape(1, d), w_in, w_out, sinks, table, bucket, lb_params,
                 gnorm_w.reshape(1, HG_DK), tri, lmask, hmask]
    in_specs = [pl.BlockSpec((1, tile, d), lambda b, t: (b, t, 0)), _full_spec((1, d)),
                _weight_spec(w_in.shape), _weight_spec(w_out.shape), smem, smem, _full_spec(bucket.shape),
                _full_spec(lb_params.shape), _full_spec((1, HG_DK)), _full_spec(tri.shape),
                _full_spec(lmask.shape), _full_spec(hmask.shape)]
    if carry:
        kv_rows = WINDOW
        st_block = (1, n_hg, HG_DK, HG_DK)
        n_win, win_rows, n_var = 1, WINDOW + tile, WINDOW // chunk + 1
    else:
        in_arrays += list(hist)
        in_specs += [_full_spec(a.shape) for a in hist]
        kv_rows = s
        st_block = (n_streams, n_hg, HG_DK, HG_DK)
        n_win, win_rows, n_var = n_chunks, keys, 1
    out_shape = [jax.ShapeDtypeStruct((nb, s, d), jnp.float32),
                 jax.ShapeDtypeStruct((nb, kv_rows, KV_WIDTH), jnp.float32),
                 jax.ShapeDtypeStruct((nb, kv_rows, KV_WIDTH), jnp.float32),
                 jax.ShapeDtypeStruct((n_streams, n_hg, HG_DK, HG_DK), jnp.float32)]
    out_specs = [pl.BlockSpec((1, tile, d), lambda b, t: (b, t, 0)),
                 pl.BlockSpec((1, kv_rows, KV_WIDTH), lambda b, t: (b, 0, 0)),
                 pl.BlockSpec((1, kv_rows, KV_WIDTH), lambda b, t: (b, 0, 0)),
                 pl.BlockSpec(st_block, lambda b, t: (b, 0, 0, 0))]
    f32, bf16 = jnp.float32, jnp.bfloat16
    rows_g = (N_HEADS // 2 // N_KV_HEADS) * chunk
    scratch = [pltpu.VMEM((n_var, 4, rows_g, key_pad), f32),
               pltpu.VMEM((n_chunks, N_HEADS // 2, chunk, LANES), bf16),
               pltpu.VMEM((tile, 2 * KV_WIDTH), f32),
               pltpu.VMEM((4, n_win, win_rows, LANES), bf16),
               pltpu.VMEM((4, n_win, win_rows, LANES), bf16),
               pltpu.VMEM((tile, d), f32), pltpu.VMEM((tile, d), f32),
               pltpu.VMEM((tile, d), bf16)]
    scratch += [pltpu.VMEM((tile, d), f32) for _ in range(7)]
    scratch += [pltpu.VMEM((n_chunks, n_hg, HG_DK, HG_DK), f32),
                pltpu.VMEM((n_chunks, n_hg, HG_DK, HG_DK), bf16),
                pltpu.VMEM((n_chunks, 4, rows_g, key_pad), bf16),
                pltpu.VMEM((n_chunks, n_hg, chunk, chunk), bf16),
                pltpu.VMEM((tile, d), bf16)]
    if carry:
        scratch += [pltpu.VMEM((n_hg, HG_DK, HG_DK), f32)]
    kern = functools.partial(_mixer_kernel, tile=tile, chunk=chunk, carry=carry, d_model=d, layer=layer)
    return pl.pallas_call(
        kern,
        grid=(nb, n_t),
        in_specs=in_specs,
        out_specs=out_specs,
        out_shape=out_shape,
        scratch_shapes=scratch,
        compiler_params=pltpu.CompilerParams(
            dimension_semantics=("arbitrary", "arbitrary"), vmem_limit_bytes=VMEM_LIMIT_BYTES),
        name="mixer_prompt" if carry else "mixer_sample",
    )(*in_arrays)


def _ffn_kernel(x_ref, nw_ref, wgu_ref, wd_ref, fw_ref, y_ref, act_s, *, d_ff, col_tile, final_norm):
    x = x_ref[...]
    h = _bf(_rms(x, nw_ref[...]))
    for j in range(d_ff // col_tile):
        g = _dot(h, _bf(wgu_ref[:, j * col_tile:(j + 1) * col_tile]))
        u = _dot(h, _bf(wgu_ref[:, d_ff + j * col_tile:d_ff + (j + 1) * col_tile]))
        act_s[:, j * col_tile:(j + 1) * col_tile] = _bf(jax.nn.silu(g) * u)
    y = x + _dot(act_s[...], _bf(wd_ref[...]))
    if final_norm:
        y = _rms(y, fw_ref[...])
    y_ref[...] = y


def _ffn(x, norm_w, w_gate_up, w_down, final_w, *, tile, final_norm):
    n, d = x.shape
    d_ff = w_down.shape[0]
    kern = functools.partial(_ffn_kernel, d_ff=d_ff, col_tile=2 * LANES, final_norm=final_norm)
    return pl.pallas_call(
        kern,
        grid=(n // tile,),
        in_specs=[pl.BlockSpec((tile, d), lambda i: (i, 0)), _full_spec((1, d)),
                  _weight_spec(w_gate_up.shape), _weight_spec(w_down.shape), _full_spec((1, d))],
        out_specs=pl.BlockSpec((tile, d), lambda i: (i, 0)),
        out_shape=jax.ShapeDtypeStruct((n, d), jnp.float32),
        scratch_shapes=[pltpu.VMEM((tile, d_ff), jnp.bfloat16)],
        compiler_params=pltpu.CompilerParams(
            dimension_semantics=("arbitrary",), vmem_limit_bytes=VMEM_LIMIT_BYTES),
        name="ffn",
    )(x, norm_w.reshape(1, d), w_gate_up, w_down, final_w.reshape(1, d))


def kernel(x_prompt, x_sample, cache_k, cache_v, state_hgrn, norm_mix, w_in, w_out, attn_sinks, rel_bias_table,
           hgrn_lb, hgrn_norm, norm_ffn, w_gate_up, w_down, norm_final):
    depth = w_in.shape[0]
    batch, seq, d = x_prompt.shape
    dec_batch, dec_seq, _ = x_sample.shape
    xp = x_prompt
    xs = x_sample.reshape(1, dec_batch * dec_seq, d)
    outs = [[] for _ in range(6)]
    for l in range(depth):
        w_in_l, w_out_l = _bf(w_in[l]), _bf(w_out[l])
        w_gu_l, w_d_l = w_gate_up[l], w_down[l]
        shared = (norm_mix[l], w_in_l, w_out_l, attn_sinks[l], rel_bias_table, hgrn_lb, hgrn_norm[l])
        xp, kp, vp, sp = _mixer(xp, None, *shared, layer=l, carry=True, tile=4 * CHUNK, chunk=CHUNK)
        hist = (cache_k[l].reshape(dec_batch, WINDOW, KV_WIDTH), cache_v[l].reshape(dec_batch, WINDOW, KV_WIDTH),
                state_hgrn[l])
        xs, ks, vs, ss = _mixer(xs, hist, *shared, layer=l, carry=False, tile=dec_batch * dec_seq, chunk=dec_seq)
        last = l == depth - 1
        xp = _ffn(xp.reshape(batch * seq, d), norm_ffn[l], w_gu_l, w_d_l, norm_final,
                  tile=512, final_norm=last).reshape(batch, seq, d)
        xs = _ffn(xs.reshape(dec_batch * dec_seq, d), norm_ffn[l], w_gu_l, w_d_l, norm_final,
                  tile=dec_batch * dec_seq, final_norm=last).reshape(1, dec_batch * dec_seq, d)
        ks = jnp.concatenate([hist[0][:, dec_seq:], ks.reshape(dec_batch, dec_seq, KV_WIDTH)], axis=1)
        vs = jnp.concatenate([hist[1][:, dec_seq:], vs.reshape(dec_batch, dec_seq, KV_WIDTH)], axis=1)
        kv_shape = (-1, WINDOW, N_KV_HEADS, HEAD_DIM)
        for acc, val in zip(outs, (kp.reshape(kv_shape), vp.reshape(kv_shape), sp,
                                   ks.reshape(kv_shape), vs.reshape(kv_shape), ss)):
            acc.append(val)
    return (xp, xs.reshape(dec_batch, dec_seq, d)) + tuple(jnp.stack(o) for o in outs)
```

```python
import functools
import math

import numpy as np
import jax
import jax.numpy as jnp
from jax import lax
from jax.experimental import pallas as pl
from jax.experimental.pallas import tpu as pltpu

CHUNK = 64
N_HEADS = 16
N_KV_HEADS = 2
HEAD_DIM = 64
GROUP = N_HEADS // N_KV_HEADS
KV_WIDTH = N_KV_HEADS * HEAD_DIM
WINDOW = 128
ATTN_SCALE = HEAD_DIM ** -0.5
N_BUCKETS = 32
MAX_DISTANCE = 128
HG_DK = 128
RMS_EPS = 1e-6
DECAY_GUARD = 60.0

LANES = 128
SUBLANES = 8
VMEM_LIMIT_BYTES = 56 * 1024 * 1024

_NT = (((1,), (1,)), ((), ()))
_TN = (((0,), (0,)), ((), ()))


def _bf(x):
    return x.astype(jnp.bfloat16)


def _dot(a, b, dims=None):
    if dims is None:
        return jnp.dot(a, b, preferred_element_type=jnp.float32)
    return lax.dot_general(a, b, dims, preferred_element_type=jnp.float32)


def _rms(x, w):
    return x * lax.rsqrt(jnp.mean(x * x, axis=-1, keepdims=True) + RMS_EPS) * w


def _t5_bucket(rel):
    nb = N_BUCKETS // 2
    max_exact = nb // 2
    ret = jnp.where(rel > 0, nb, 0)
    n = jnp.abs(rel)
    nf = jnp.maximum(n, 1).astype(jnp.float32)
    large = max_exact + (jnp.log(nf / max_exact) / math.log(MAX_DISTANCE / max_exact)
                         * (nb - max_exact)).astype(jnp.int32)
    large = jnp.minimum(large, nb - 1)
    return ret + jnp.where(n < max_exact, n, large)


def _levels(c):
    out, m = [], c // 2
    while m >= 1:
        out.append(m)
        m //= 2
    return out


def _level_masks(c):
    t = np.arange(c)[:, None]
    s = np.arange(c)[None, :]
    masks = []
    for m in _levels(c):
        masks.append((t // (2 * m) == s // (2 * m)) & ((t // m) % 2 == 1) & ((s // m) % 2 == 0))
    masks += [t == s, s <= t]
    return np.stack(masks).astype(np.float32)


def _row_masks(c):
    r = np.arange(c)
    rows = [(r // m) % 2 == 1 for m in _levels(c) if m < 8]
    rows += [r % 4 == 0, r % 4 >= 2, r % 4 == 3]
    return np.repeat(np.stack(rows).astype(np.float32)[:, :, None], LANES, axis=2)


def _block_cumsum(x, block):
    r, n = x.shape
    tiles = x.reshape(r // SUBLANES, SUBLANES, n)
    row = lax.broadcasted_iota(jnp.int32, (1, SUBLANES, n), 1)
    for shift in (1, 2, 4):
        tiles = tiles + jnp.where(row >= shift, pltpu.roll(tiles, shift, axis=1), 0.0)
    out, before = [], None
    for i in range(r // SUBLANES):
        t = tiles[i] if before is None or i % (block // SUBLANES) == 0 else tiles[i] + before
        out.append(t)
        before = jnp.broadcast_to(t[SUBLANES - 1:SUBLANES, :], (SUBLANES, n))
    return jnp.concatenate(out, axis=0)


def _head_variants(a):
    lo = lax.broadcasted_iota(jnp.int32, a.shape, 1) < HEAD_DIM
    rot = pltpu.roll(a, HEAD_DIM, axis=1)
    return [_bf(jnp.where(lo, a, 0.0)), _bf(jnp.where(lo, 0.0, rot)),
            _bf(jnp.where(lo, rot, 0.0)), _bf(jnp.where(lo, 0.0, a))]


def _mixer_kernel(*refs, tile, chunk, carry, d_model, layer):
    n_chunks = tile // chunk
    keys = WINDOW + chunk
    key_pad = 2 * LANES
    assert keys < key_pad and tile >= WINDOW
    levels = _levels(chunk)
    n_hg = d_model // HG_DK
    att_w = N_HEADS * HEAD_DIM
    n_pairs = N_HEADS // 2
    ppk = n_pairs // N_KV_HEADS
    rows_g = ppk * chunk
    n_var = WINDOW // chunk + 1 if carry else 1
    f32 = jnp.float32

    refs = list(refs)
    (x_ref, nw_ref, win_ref, wout_ref, sinks_ref, relb_ref, lbp_ref, gn_ref,
     lmask_ref, hmask_ref) = refs[:10]
    refs = refs[10:]
    if not carry:
        hk_ref, hv_ref, si_ref = refs[:3]
        refs = refs[3:]
    y_ref, ko_ref, vo_ref, so_ref = refs[:4]
    refs = refs[4:]
    (bias_s, q4_s, kv_s, kx_s, vx_s, hq_s, kk_s, hi_s, lf_s, bc_s, og_s, ga_s, gh_s, att_s, hg_s,
     u_s, sb_s, p_s, a_s, qe_s) = refs[:20]
    if carry:
        st_s = refs[20]

    first = (pl.program_id(0) == 0) & (pl.program_id(1) == 0)
    t_idx = pl.program_id(1)

    @pl.when(first)
    def _build_bias():
        lane = lax.broadcasted_iota(jnp.int32, (chunk, key_pad), 1)
        for head in range(N_HEADS):
            acc = jnp.where(lane == keys, sinks_ref[head], jnp.where(lane > keys, -jnp.inf, relb_ref[head]))
            pair, side = divmod(head, 2)
            kvh, j = divmod(pair, ppk)
            for var in range(n_var):
                n_invalid = WINDOW - var * chunk if carry else 0
                bias_s[var, 2 * kvh + side, j * chunk:(j + 1) * chunk, :] = jnp.where(lane < n_invalid, -jnp.inf, acc)

    if carry:
        @pl.when(t_idx == 0)
        def _reset():
            kx_s[:, 0, 0:WINDOW, :] = jnp.zeros((4, WINDOW, LANES), jnp.bfloat16)
            vx_s[:, 0, 0:WINDOW, :] = jnp.zeros((4, WINDOW, LANES), jnp.bfloat16)
            st_s[...] = jnp.zeros_like(st_s)

        @pl.when(t_idx > 0)
        def _shift():
            kx_s[:, 0, 0:WINDOW, :] = kx_s[:, 0, tile:tile + WINDOW, :]
            vx_s[:, 0, 0:WINDOW, :] = vx_s[:, 0, tile:tile + WINDOW, :]

    x = x_ref[0]
    h = _bf(_rms(x, nw_ref[...]))

    def proj(i0, width):
        return _dot(h, win_ref[:, i0:i0 + width])

    off = 0
    q = _bf(proj(off, att_w) * ATTN_SCALE)
    for c in range(n_chunks):
        for j in range(n_pairs):
            q4_s[c, j] = q[c * chunk:(c + 1) * chunk, j * LANES:(j + 1) * LANES]
    off += att_w
    kv = proj(off, 2 * KV_WIDTH)
    kv_s[...] = kv
    off += 2 * KV_WIDTH
    k_var = _head_variants(kv[:, 0:KV_WIDTH])
    v_var = _head_variants(kv[:, KV_WIDTH:2 * KV_WIDTH])
    if carry:
        for g in range(4):
            kx_s[g, 0, WINDOW:WINDOW + tile, :] = k_var[g]
            vx_s[g, 0, WINDOW:WINDOW + tile, :] = v_var[g]
    else:
        for c in range(n_chunks):
            hk = _head_variants(hk_ref[c])
            hv = _head_variants(hv_ref[c])
            for g in range(4):
                kx_s[g, c, 0:WINDOW, :] = hk[g]
                vx_s[g, c, 0:WINDOW, :] = hv[g]
                kx_s[g, c, WINDOW:keys, :] = k_var[g][c * chunk:(c + 1) * chunk]
                vx_s[g, c, WINDOW:keys, :] = v_var[g][c * chunk:(c + 1) * chunk]
    hq_s[...] = proj(off, d_model)
    off += d_model
    lbp = lbp_ref[...]
    e = jnp.exp(lbp - jnp.max(lbp, axis=0, keepdims=True))
    lb = jnp.sum(e[:layer + 1], axis=0, keepdims=True) / jnp.sum(e, axis=0, keepdims=True)
    f = lb + (1.0 - lb) * jax.nn.sigmoid(proj(off, d_model))
    off += d_model
    kk_s[...] = 1.0 - f
    lf = jnp.log(f)
    lf_s[...] = lf
    bc_s[...] = _block_cumsum(lf, chunk)
    hi_s[...] = _bf(proj(off, d_model))
    off += d_model
    og_s[...] = jax.nn.silu(proj(off, d_model))
    off += d_model
    ga_s[...] = jax.nn.sigmoid(proj(off, d_model))
    off += d_model
    gh_s[...] = jax.nn.sigmoid(proj(off, d_model))

    zpad = jnp.zeros((key_pad - keys, LANES), jnp.bfloat16)

    def window(ref, g, c):
        w = ref[g, 0, c * chunk:c * chunk + keys, :] if carry else ref[g, c]
        return jnp.concatenate([w, zpad], axis=0)

    for c in range(n_chunks):
        var = jnp.minimum(t_idx * n_chunks + c, n_var - 1) if carry else 0
        for kvh in range(N_KV_HEADS):
            qg = q4_s[c, kvh * ppk:(kvh + 1) * ppk].reshape(rows_g, LANES)
            for side in range(2):
                g = 2 * kvh + side
                s = _dot(qg, window(kx_s, g, c), _NT) + bias_s[var, g]
                ex = jnp.exp(s - jnp.max(s, axis=-1, keepdims=True))
                p_s[c, g] = _bf(ex * (1.0 / jnp.sum(ex, axis=-1, keepdims=True)))
    for c in range(n_chunks):
        for kvh in range(N_KV_HEADS):
            o = (_dot(p_s[c, 2 * kvh], window(vx_s, 2 * kvh, c))
                 + _dot(p_s[c, 2 * kvh + 1], window(vx_s, 2 * kvh + 1, c)))
            for j in range(ppk):
                col = (kvh * ppk + j) * LANES
                att_s[c * chunk:(c + 1) * chunk, col:col + LANES] = o[j * chunk:(j + 1) * chunk]

    gn = gn_ref[...]
    small = [m for m in levels if m < 8]

    def unit(c, hd):
        return slice(c * chunk, (c + 1) * chunk), slice(hd * HG_DK, (hd + 1) * HG_DK)

    def bc_row(c, r, cols):
        return bc_s[c * chunk + r:c * chunk + r + 1, cols]

    def level_operand(m, c, hd):
        rows, cols = unit(c, hd)
        q = hq_s[rows, cols]
        kk = kk_s[rows, cols]
        if m >= 8:
            bc = bc_s[rows, cols]
            parts = []
            for base in range(0, chunk, 2 * m):
                ref = bc_row(c, base + m - 1, cols)
                lo, hi = slice(base, base + m), slice(base + m, base + 2 * m)
                parts += [kk[lo] * jnp.exp(ref - bc[lo]), q[hi] * jnp.exp(bc[hi] - ref)]
            return _bf(jnp.concatenate(parts, axis=0))
        upper = hmask_ref[small.index(m)] > 0.5
        if m == 1:
            return _bf(jnp.where(upper, q * (1.0 - kk), kk))
        if m == 4:
            bc = bc_s[rows, cols]
            ref_row = bc.reshape(chunk // 8, 8, HG_DK)[:, 3:4, :]
            ref = jnp.broadcast_to(ref_row, (chunk // 8, 8, HG_DK)).reshape(chunk, HG_DK)
            arg = -jnp.abs(bc - ref)
        else:
            lfh = lf_s[rows, cols]
            nxt = pltpu.roll(lfh, chunk - 1, axis=0)
            prv = pltpu.roll(lfh, 1, axis=0)
            n_small = len(small)
            arg = hmask_ref[n_small] * nxt + hmask_ref[n_small + 1] * lfh + hmask_ref[n_small + 2] * prv
        return _bf(jnp.where(upper, q, kk) * jnp.exp(arg))

    for c in range(n_chunks):
        for hd in range(n_hg):
            rows, cols = unit(c, hd)
            bc = bc_s[rows, cols]
            k_dec = _bf(kk_s[rows, cols] * jnp.exp(bc_row(c, chunk - 1, cols) - bc))
            u_s[c, hd] = _dot(hi_s[rows, cols], k_dec, _TN)
            qe_s[rows, cols] = _bf(hq_s[rows, cols] * jnp.exp(bc))

    for hd in range(n_hg):
        if carry:
            s = st_s[hd]
        for c in range(n_chunks):
            rows, cols = unit(c, hd)
            decay = jnp.exp(bc_row(c, chunk - 1, cols))
            if carry:
                sb_s[c, hd] = _bf(s)
                s = s * decay + u_s[c, hd]
            else:
                s0 = si_ref[c, hd].T
                sb_s[c, hd] = _bf(s0)
                so_ref[c, hd] = (s0 * decay + u_s[c, hd]).T
        if carry:
            st_s[hd] = s

    total_decay = jnp.concatenate([bc_row(c, chunk - 1, slice(None)) for c in range(n_chunks)], axis=0)
    mild = jnp.max(-total_decay) <= DECAY_GUARD

    @pl.when(mild)
    def _direct():
        tril = lmask_ref[len(levels) + 1] > 0.5
        for c in range(n_chunks):
            for hd in range(n_hg):
                rows, cols = unit(c, hd)
                k_grow = _bf(kk_s[rows, cols] * jnp.exp(-bc_s[rows, cols]))
                a_s[c, hd] = _bf(jnp.where(tril, _dot(qe_s[rows, cols], k_grow, _NT), 0.0))

    @pl.when(jnp.logical_not(mild))
    def _split():
        for c in range(n_chunks):
            for hd in range(n_hg):
                rows, cols = unit(c, hd)
                a = _dot(_bf(hq_s[rows, cols]), _bf(kk_s[rows, cols]), _NT) * lmask_ref[len(levels)]
                for li, m in enumerate(levels):
                    z = level_operand(m, c, hd)
                    a = a + _dot(z, z, _NT) * lmask_ref[li]
                a_s[c, hd] = _bf(a)

    for c in range(n_chunks):
        for hd in range(n_hg):
            rows, cols = unit(c, hd)
            o = _dot(qe_s[rows, cols], sb_s[c, hd], _NT) + _dot(a_s[c, hd], hi_s[rows, cols])
            o = o * lax.rsqrt(jnp.mean(o * o, axis=-1, keepdims=True) + RMS_EPS) * gn
            hg_s[rows, cols] = o * og_s[rows, cols]

    merged = _bf(ga_s[...] * att_s[...] + gh_s[...] * hg_s[...])
    y_ref[0] = x + _dot(merged, wout_ref[...])

    if carry:
        @pl.when(t_idx == pl.num_programs(1) - 1)
        def _emit():
            ko_ref[0] = kv_s[tile - WINDOW:tile, 0:KV_WIDTH]
            vo_ref[0] = kv_s[tile - WINDOW:tile, KV_WIDTH:2 * KV_WIDTH]
            for hd in range(n_hg):
                so_ref[0, hd] = st_s[hd].T
    else:
        ko_ref[0] = kv_s[:, 0:KV_WIDTH]
        vo_ref[0] = kv_s[:, KV_WIDTH:2 * KV_WIDTH]


def _full_spec(shape):
    nd = len(shape)
    return pl.BlockSpec(shape, lambda *_: (0,) * nd)


def _weight_spec(shape):
    nd = len(shape)
    return pl.BlockSpec(shape, lambda *_: (0,) * nd, pipeline_mode=pl.Buffered(1))


def _mixer(x, hist, norm_w, w_in, w_out, sinks, table, lb_params, gnorm_w, *, layer, carry, tile, chunk):
    nb, s, d = x.shape
    n_t = s // tile
    n_chunks = tile // chunk
    keys = WINDOW + chunk
    key_pad = 2 * LANES
    n_hg = d // HG_DK
    n_streams = nb if carry else s // chunk
    rel = (jnp.arange(key_pad, dtype=jnp.int32)[None, :] - WINDOW
           - jnp.arange(chunk, dtype=jnp.int32)[:, None])
    rel_bias = jnp.transpose(table[_t5_bucket(rel)], (2, 0, 1)).astype(jnp.float32)
    lmask = jnp.asarray(_level_masks(chunk))
    hmask = jnp.asarray(_row_masks(chunk))
    smem = pl.BlockSpec(memory_space=pltpu.SMEM)

    in_arrays = [x, norm_w.reshape(1, d), w_in, w_out, sinks, rel_bias, lb_params,
                 gnorm_w.reshape(1, HG_DK), lmask, hmask]
    in_specs = [pl.BlockSpec((1, tile, d), lambda b, t: (b, t, 0)), _full_spec((1, d)),
                _weight_spec(w_in.shape), _weight_spec(w_out.shape), smem, _full_spec(rel_bias.shape),
                _full_spec(lb_params.shape), _full_spec((1, HG_DK)),
                _full_spec(lmask.shape), _full_spec(hmask.shape)]
    if carry:
        kv_rows = WINDOW
        st_block = (1, n_hg, HG_DK, HG_DK)
        n_win, win_rows, n_var = 1, WINDOW + tile, WINDOW // chunk + 1
    else:
        in_arrays += list(hist)
        in_specs += [_full_spec(a.shape) for a in hist]
        kv_rows = s
        st_block = (n_streams, n_hg, HG_DK, HG_DK)
        n_win, win_rows, n_var = n_chunks, keys, 1
    out_shape = [jax.ShapeDtypeStruct((nb, s, d), jnp.float32),
                 jax.ShapeDtypeStruct((nb, kv_rows, KV_WIDTH), jnp.float32),
                 jax.ShapeDtypeStruct((nb, kv_rows, KV_WIDTH), jnp.float32),
                 jax.ShapeDtypeStruct((n_streams, n_hg, HG_DK, HG_DK), jnp.float32)]
    out_specs = [pl.BlockSpec((1, tile, d), lambda b, t: (b, t, 0)),
                 pl.BlockSpec((1, kv_rows, KV_WIDTH), lambda b, t: (b, 0, 0)),
                 pl.BlockSpec((1, kv_rows, KV_WIDTH), lambda b, t: (b, 0, 0)),
                 pl.BlockSpec(st_block, lambda b, t: (b, 0, 0, 0))]
    f32, bf16 = jnp.float32, jnp.bfloat16
    rows_g = (N_HEADS // 2 // N_KV_HEADS) * chunk
    scratch = [pltpu.VMEM((n_var, 4, rows_g, key_pad), f32),
               pltpu.VMEM((n_chunks, N_HEADS // 2, chunk, LANES), bf16),
               pltpu.VMEM((tile, 2 * KV_WIDTH), f32),
               pltpu.VMEM((4, n_win, win_rows, LANES), bf16),
               pltpu.VMEM((4, n_win, win_rows, LANES), bf16),
               pltpu.VMEM((tile, d), f32), pltpu.VMEM((tile, d), f32),
               pltpu.VMEM((tile, d), bf16)]
    scratch += [pltpu.VMEM((tile, d), f32) for _ in range(7)]
    scratch += [pltpu.VMEM((n_chunks, n_hg, HG_DK, HG_DK), f32),
                pltpu.VMEM((n_chunks, n_hg, HG_DK, HG_DK), bf16),
                pltpu.VMEM((n_chunks, 4, rows_g, key_pad), bf16),
                pltpu.VMEM((n_chunks, n_hg, chunk, chunk), bf16),
                pltpu.VMEM((tile, d), bf16)]
    if carry:
        scratch += [pltpu.VMEM((n_hg, HG_DK, HG_DK), f32)]
    kern = functools.partial(_mixer_kernel, tile=tile, chunk=chunk, carry=carry, d_model=d, layer=layer)
    return pl.pallas_call(
        kern,
        grid=(nb, n_t),
        in_specs=in_specs,
        out_specs=out_specs,
        out_shape=out_shape,
        scratch_shapes=scratch,
        compiler_params=pltpu.CompilerParams(
            dimension_semantics=("arbitrary", "arbitrary"), vmem_limit_bytes=VMEM_LIMIT_BYTES),
        name="mixer_prompt" if carry else "mixer_sample",
    )(*in_arrays)


def _ffn_kernel(xa_ref, xb_ref, nw_ref, wgu_ref, wd_ref, fw_ref, ya_ref, yb_ref, act_s, *, d_ff, col_tile,
                final_norm):
    n_a = pl.num_programs(0) - 1

    def ffn(x_ref, y_ref):
        rows = x_ref.shape[0]
        x = x_ref[...]
        h = _bf(_rms(x, nw_ref[...]))
        for j in range(d_ff // col_tile):
            g = _dot(h, _bf(wgu_ref[:, j * col_tile:(j + 1) * col_tile]))
            u = _dot(h, _bf(wgu_ref[:, d_ff + j * col_tile:d_ff + (j + 1) * col_tile]))
            act_s[0:rows, j * col_tile:(j + 1) * col_tile] = _bf(jax.nn.silu(g) * u)
        y = x + _dot(act_s[0:rows, :], _bf(wd_ref[...]))
        if final_norm:
            y = _rms(y, fw_ref[...])
        y_ref[...] = y

    @pl.when(pl.program_id(0) < n_a)
    def _first_set():
        ffn(xa_ref, ya_ref)

    @pl.when(pl.program_id(0) == n_a)
    def _second_set():
        ffn(xb_ref, yb_ref)


def _ffn(xa, xb, norm_w, w_gate_up, w_down, final_w, *, tile, final_norm):
    n, d = xa.shape
    m = xb.shape[0]
    n_a = n // tile
    assert n == n_a * tile and m <= tile
    d_ff = w_down.shape[0]
    kern = functools.partial(_ffn_kernel, d_ff=d_ff, col_tile=2 * LANES, final_norm=final_norm)
    a_spec = pl.BlockSpec((tile, d), lambda i: (jnp.minimum(i, n_a - 1), 0))
    return pl.pallas_call(
        kern,
        grid=(n_a + 1,),
        in_specs=[a_spec, _full_spec((m, d)), _full_spec((1, d)),
                  _weight_spec(w_gate_up.shape), _weight_spec(w_down.shape), _full_spec((1, d))],
        out_specs=[a_spec, _full_spec((m, d))],
        out_shape=[jax.ShapeDtypeStruct((n, d), jnp.float32), jax.ShapeDtypeStruct((m, d), jnp.float32)],
        scratch_shapes=[pltpu.VMEM((tile, d_ff), jnp.bfloat16)],
        compiler_params=pltpu.CompilerParams(
            dimension_semantics=("arbitrary",), vmem_limit_bytes=VMEM_LIMIT_BYTES),
        name="ffn",
    )(xa, xb, norm_w.reshape(1, d), w_gate_up, w_down, final_w.reshape(1, d))


def kernel(x_prompt, x_sample, cache_k, cache_v, state_hgrn, norm_mix, w_in, w_out, attn_sinks, rel_bias_table,
           hgrn_lb, hgrn_norm, norm_ffn, w_gate_up, w_down, norm_final):
    depth = w_in.shape[0]
    batch, seq, d = x_prompt.shape
    dec_batch, dec_seq, _ = x_sample.shape
    xp = x_prompt
    xs = x_sample.reshape(1, dec_batch * dec_seq, d)
    outs = [[] for _ in range(6)]
    for l in range(depth):
        w_in_l, w_out_l = _bf(w_in[l]), _bf(w_out[l])
        shared = (norm_mix[l], w_in_l, w_out_l, attn_sinks[l], rel_bias_table, hgrn_lb, hgrn_norm[l])
        xp, kp, vp, sp = _mixer(xp, None, *shared, layer=l, carry=True, tile=4 * CHUNK, chunk=CHUNK)
        hist = (cache_k[l].reshape(dec_batch, WINDOW, KV_WIDTH), cache_v[l].reshape(dec_batch, WINDOW, KV_WIDTH),
                state_hgrn[l])
        xs, ks, vs, ss = _mixer(xs, hist, *shared, layer=l, carry=False, tile=dec_batch * dec_seq, chunk=dec_seq)
        last = l == depth - 1
        xp, xs = _ffn(xp.reshape(batch * seq, d), xs.reshape(dec_batch * dec_seq, d), norm_ffn[l], w_gate_up[l],
                      w_down[l], norm_final, tile=512, final_norm=last)
        xp = xp.reshape(batch, seq, d)
        xs = xs.reshape(1, dec_batch * dec_seq, d)
        ks = jnp.concatenate([hist[0][:, dec_seq:], ks.reshape(dec_batch, dec_seq, KV_WIDTH)], axis=1)
        vs = jnp.concatenate([hist[1][:, dec_seq:], vs.reshape(dec_batch, dec_seq, KV_WIDTH)], axis=1)
        kv_shape = (-1, WINDOW, N_KV_HEADS, HEAD_DIM)
        for acc, val in zip(outs, (kp.reshape(kv_shape), vp.reshape(kv_shape), sp,
                                   ks.reshape(kv_shape), vs.reshape(kv_shape), ss)):
            acc.append(val)
    return (xp, xs.reshape(dec_batch, dec_seq, d)) + tuple(jnp.stack(o) for o in outs)
```

```python
import functools
import math

import numpy as np
import jax
import jax.numpy as jnp
from jax import lax
from jax.experimental import pallas as pl
from jax.experimental.pallas import tpu as pltpu

CHUNK = 64
N_HEADS = 16
N_KV_HEADS = 2
HEAD_DIM = 64
GROUP = N_HEADS // N_KV_HEADS
KV_WIDTH = N_KV_HEADS * HEAD_DIM
WINDOW = 128
ATTN_SCALE = HEAD_DIM ** -0.5
N_BUCKETS = 32
MAX_DISTANCE = 128
HG_DK = 128
RMS_EPS = 1e-6
DECAY_GUARD = 60.0

LANES = 128
SUBLANES = 8
VMEM_LIMIT_BYTES = 56 * 1024 * 1024

_NT = (((1,), (1,)), ((), ()))
_TN = (((0,), (0,)), ((), ()))


def _bf(x):
    return x.astype(jnp.bfloat16)


def _dot(a, b, dims=None):
    if dims is None:
        return jnp.dot(a, b, preferred_element_type=jnp.float32)
    return lax.dot_general(a, b, dims, preferred_element_type=jnp.float32)


def _rms(x, w):
    return x * lax.rsqrt(jnp.mean(x * x, axis=-1, keepdims=True) + RMS_EPS) * w


def _t5_bucket(rel):
    nb = N_BUCKETS // 2
    max_exact = nb // 2
    ret = jnp.where(rel > 0, nb, 0)
    n = jnp.abs(rel)
    nf = jnp.maximum(n, 1).astype(jnp.float32)
    large = max_exact + (jnp.log(nf / max_exact) / math.log(MAX_DISTANCE / max_exact)
                         * (nb - max_exact)).astype(jnp.int32)
    large = jnp.minimum(large, nb - 1)
    return ret + jnp.where(n < max_exact, n, large)


def _levels(c):
    out, m = [], c // 2
    while m >= 1:
        out.append(m)
        m //= 2
    return out


def _level_masks(c):
    t = np.arange(c)[:, None]
    s = np.arange(c)[None, :]
    masks = []
    for m in _levels(c):
        masks.append((t // (2 * m) == s // (2 * m)) & ((t // m) % 2 == 1) & ((s // m) % 2 == 0))
    masks += [t == s, s <= t]
    return np.stack(masks).astype(np.float32)


def _row_masks(c):
    r = np.arange(c)
    rows = [(r // m) % 2 == 1 for m in _levels(c) if m < 8]
    rows += [r % 4 == 0, r % 4 >= 2, r % 4 == 3]
    return np.repeat(np.stack(rows).astype(np.float32)[:, :, None], LANES, axis=2)


def _block_cumsum(x, block):
    r, n = x.shape
    tiles = x.reshape(r // SUBLANES, SUBLANES, n)
    row = lax.broadcasted_iota(jnp.int32, (1, SUBLANES, n), 1)
    for shift in (1, 2, 4):
        tiles = tiles + jnp.where(row >= shift, pltpu.roll(tiles, shift, axis=1), 0.0)
    out, before = [], None
    for i in range(r // SUBLANES):
        t = tiles[i] if before is None or i % (block // SUBLANES) == 0 else tiles[i] + before
        out.append(t)
        before = jnp.broadcast_to(t[SUBLANES - 1:SUBLANES, :], (SUBLANES, n))
    return jnp.concatenate(out, axis=0)


def _head_variants(a):
    lo = lax.broadcasted_iota(jnp.int32, a.shape, 1) < HEAD_DIM
    rot = pltpu.roll(a, HEAD_DIM, axis=1)
    return [_bf(jnp.where(lo, a, 0.0)), _bf(jnp.where(lo, 0.0, rot)),
            _bf(jnp.where(lo, rot, 0.0)), _bf(jnp.where(lo, 0.0, a))]


def _mixer_kernel(*refs, tile, chunk, carry, d_model, layer):
    n_chunks = tile // chunk
    keys = WINDOW + chunk
    key_pad = 2 * LANES
    assert keys < key_pad and tile >= WINDOW
    levels = _levels(chunk)
    n_hg = d_model // HG_DK
    att_w = N_HEADS * HEAD_DIM
    n_pairs = N_HEADS // 2
    ppk = n_pairs // N_KV_HEADS
    rows_g = ppk * chunk
    n_var = WINDOW // chunk + 1 if carry else 1
    f32 = jnp.float32

    refs = list(refs)
    (x_ref, nw_ref, win_ref, wout_ref, sinks_ref, relb_ref, lbp_ref, gn_ref,
     lmask_ref, hmask_ref) = refs[:10]
    refs = refs[10:]
    if not carry:
        hk_ref, hv_ref, si_ref = refs[:3]
        refs = refs[3:]
    y_ref, ko_ref, vo_ref, so_ref = refs[:4]
    refs = refs[4:]
    (bias_s, q4_s, kv_s, kx_s, vx_s, hq_s, kk_s, hi_s, lf_s, bc_s, og_s, ga_s, gh_s, att_s, hg_s,
     u_s, sb_s, p_s, a_s, qe_s) = refs[:20]
    if carry:
        st_s = refs[20]

    first = (pl.program_id(0) == 0) & (pl.program_id(1) == 0)
    t_idx = pl.program_id(1)

    @pl.when(first)
    def _build_bias():
        lane = lax.broadcasted_iota(jnp.int32, (chunk, key_pad), 1)
        for head in range(N_HEADS):
            acc = jnp.where(lane == keys, sinks_ref[head], jnp.where(lane > keys, -jnp.inf, relb_ref[head]))
            pair, side = divmod(head, 2)
            kvh, j = divmod(pair, ppk)
            for var in range(n_var):
                n_invalid = WINDOW - var * chunk if carry else 0
                bias_s[var, 2 * kvh + side, j * chunk:(j + 1) * chunk, :] = jnp.where(lane < n_invalid, -jnp.inf, acc)

    if carry:
        @pl.when(t_idx == 0)
        def _reset():
            kx_s[:, 0, 0:WINDOW, :] = jnp.zeros((4, WINDOW, LANES), jnp.bfloat16)
            vx_s[:, 0, 0:WINDOW, :] = jnp.zeros((4, WINDOW, LANES), jnp.bfloat16)
            st_s[...] = jnp.zeros_like(st_s)

        @pl.when(t_idx > 0)
        def _shift():
            kx_s[:, 0, 0:WINDOW, :] = kx_s[:, 0, tile:tile + WINDOW, :]
            vx_s[:, 0, 0:WINDOW, :] = vx_s[:, 0, tile:tile + WINDOW, :]

    x = x_ref[0]
    h = _bf(_rms(x, nw_ref[...]))

    def proj(i0, width):
        return _dot(h, win_ref[:, i0:i0 + width])

    off = 0
    q = _bf(proj(off, att_w) * ATTN_SCALE)
    for c in range(n_chunks):
        for j in range(n_pairs):
            q4_s[c, j] = q[c * chunk:(c + 1) * chunk, j * LANES:(j + 1) * LANES]
    off += att_w
    kv = proj(off, 2 * KV_WIDTH)
    kv_s[...] = kv
    off += 2 * KV_WIDTH
    k_var = _head_variants(kv[:, 0:KV_WIDTH])
    v_var = _head_variants(kv[:, KV_WIDTH:2 * KV_WIDTH])
    if carry:
        for g in range(4):
            kx_s[g, 0, WINDOW:WINDOW + tile, :] = k_var[g]
            vx_s[g, 0, WINDOW:WINDOW + tile, :] = v_var[g]
    else:
        for c in range(n_chunks):
            hk = _head_variants(hk_ref[c])
            hv = _head_variants(hv_ref[c])
            for g in range(4):
                kx_s[g, c, 0:WINDOW, :] = hk[g]
                vx_s[g, c, 0:WINDOW, :] = hv[g]
                kx_s[g, c, WINDOW:keys, :] = k_var[g][c * chunk:(c + 1) * chunk]
                vx_s[g, c, WINDOW:keys, :] = v_var[g][c * chunk:(c + 1) * chunk]
    hq_s[...] = proj(off, d_model)
    off += d_model
    lbp = lbp_ref[...]
    e = jnp.exp(lbp - jnp.max(lbp, axis=0, keepdims=True))
    lb = jnp.sum(e[:layer + 1], axis=0, keepdims=True) / jnp.sum(e, axis=0, keepdims=True)
    f = lb + (1.0 - lb) * jax.nn.sigmoid(proj(off, d_model))
    off += d_model
    kk_s[...] = 1.0 - f
    lf = jnp.log(f)
    lf_s[...] = lf
    bc_s[...] = _block_cumsum(lf, chunk)
    hi_s[...] = _bf(proj(off, d_model))
    off += d_model
    og_s[...] = jax.nn.silu(proj(off, d_model))
    off += d_model
    ga_s[...] = jax.nn.sigmoid(proj(off, d_model))
    off += d_model
    gh_s[...] = jax.nn.sigmoid(proj(off, d_model))

    zpad = jnp.zeros((key_pad - keys, LANES), jnp.bfloat16)

    def window(ref, g, c):
        w = ref[g, 0, c * chunk:c * chunk + keys, :] if carry else ref[g, c]
        return jnp.concatenate([w, zpad], axis=0)

    for c in range(n_chunks):
        var = jnp.minimum(t_idx * n_chunks + c, n_var - 1) if carry else 0
        for kvh in range(N_KV_HEADS):
            qg = q4_s[c, kvh * ppk:(kvh + 1) * ppk].reshape(rows_g, LANES)
            for side in range(2):
                g = 2 * kvh + side
                s = _dot(qg, window(kx_s, g, c), _NT) + bias_s[var, g]
                ex = jnp.exp(s - jnp.max(s, axis=-1, keepdims=True))
                p_s[c, g] = _bf(ex * (1.0 / jnp.sum(ex, axis=-1, keepdims=True)))
    for c in range(n_chunks):
        for kvh in range(N_KV_HEADS):
            o = (_dot(p_s[c, 2 * kvh], window(vx_s, 2 * kvh, c))
                 + _dot(p_s[c, 2 * kvh + 1], window(vx_s, 2 * kvh + 1, c)))
            for j in range(ppk):
                col = (kvh * ppk + j) * LANES
                att_s[c * chunk:(c + 1) * chunk, col:col + LANES] = o[j * chunk:(j + 1) * chunk]

    gn = gn_ref[...]
    small = [m for m in levels if m < 8]

    def unit(c, hd):
        return slice(c * chunk, (c + 1) * chunk), slice(hd * HG_DK, (hd + 1) * HG_DK)

    def bc_row(c, r, cols):
        return bc_s[c * chunk + r:c * chunk + r + 1, cols]

    def level_operand(m, c, hd):
        rows, cols = unit(c, hd)
        q = hq_s[rows, cols]
        kk = kk_s[rows, cols]
        if m >= 8:
            bc = bc_s[rows, cols]
            parts = []
            for base in range(0, chunk, 2 * m):
                ref = bc_row(c, base + m - 1, cols)
                lo, hi = slice(base, base + m), slice(base + m, base + 2 * m)
                parts += [kk[lo] * jnp.exp(ref - bc[lo]), q[hi] * jnp.exp(bc[hi] - ref)]
            return _bf(jnp.concatenate(parts, axis=0))
        upper = hmask_ref[small.index(m)] > 0.5
        if m == 1:
            return _bf(jnp.where(upper, q * (1.0 - kk), kk))
        if m == 4:
            bc = bc_s[rows, cols]
            ref_row = bc.reshape(chunk // 8, 8, HG_DK)[:, 3:4, :]
            ref = jnp.broadcast_to(ref_row, (chunk // 8, 8, HG_DK)).reshape(chunk, HG_DK)
            arg = -jnp.abs(bc - ref)
        else:
            lfh = lf_s[rows, cols]
            nxt = pltpu.roll(lfh, chunk - 1, axis=0)
            prv = pltpu.roll(lfh, 1, axis=0)
            n_small = len(small)
            arg = hmask_ref[n_small] * nxt + hmask_ref[n_small + 1] * lfh + hmask_ref[n_small + 2] * prv
        return _bf(jnp.where(upper, q, kk) * jnp.exp(arg))

    for c in range(n_chunks):
        for hd in range(n_hg):
            rows, cols = unit(c, hd)
            bc = bc_s[rows, cols]
            k_dec = _bf(kk_s[rows, cols] * jnp.exp(bc_row(c, chunk - 1, cols) - bc))
            u_s[c, hd] = _dot(hi_s[rows, cols], k_dec, _TN)
            qe_s[rows, cols] = _bf(hq_s[rows, cols] * jnp.exp(bc))

    for hd in range(n_hg):
        if carry:
            s = st_s[hd]
        for c in range(n_chunks):
            rows, cols = unit(c, hd)
            decay = jnp.exp(bc_row(c, chunk - 1, cols))
            if carry:
                sb_s[c, hd] = _bf(s)
                s = s * decay + u_s[c, hd]
            else:
                s0 = si_ref[c, hd].T
                sb_s[c, hd] = _bf(s0)
                so_ref[c, hd] = (s0 * decay + u_s[c, hd]).T
        if carry:
            st_s[hd] = s

    total_decay = jnp.concatenate([bc_row(c, chunk - 1, slice(None)) for c in range(n_chunks)], axis=0)
    mild = jnp.max(-total_decay) <= DECAY_GUARD

    @pl.when(mild)
    def _direct():
        tril = lmask_ref[len(levels) + 1] > 0.5
        for c in range(n_chunks):
            for hd in range(n_hg):
                rows, cols = unit(c, hd)
                k_grow = _bf(kk_s[rows, cols] * jnp.exp(-bc_s[rows, cols]))
                a_s[c, hd] = _bf(jnp.where(tril, _dot(qe_s[rows, cols], k_grow, _NT), 0.0))

    @pl.when(jnp.logical_not(mild))
    def _split():
        for c in range(n_chunks):
            for hd in range(n_hg):
                rows, cols = unit(c, hd)
                a = _dot(_bf(hq_s[rows, cols]), _bf(kk_s[rows, cols]), _NT) * lmask_ref[len(levels)]
                for li, m in enumerate(levels):
                    z = level_operand(m, c, hd)
                    a = a + _dot(z, z, _NT) * lmask_ref[li]
                a_s[c, hd] = _bf(a)

    for c in range(n_chunks):
        for hd in range(n_hg):
            rows, cols = unit(c, hd)
            o = _dot(qe_s[rows, cols], sb_s[c, hd], _NT) + _dot(a_s[c, hd], hi_s[rows, cols])
            o = o * lax.rsqrt(jnp.mean(o * o, axis=-1, keepdims=True) + RMS_EPS) * gn
            hg_s[rows, cols] = o * og_s[rows, cols]

    merged = _bf(ga_s[...] * att_s[...] + gh_s[...] * hg_s[...])
    y_ref[0] = x + _dot(merged, wout_ref[...])

    if carry:
        @pl.when(t_idx == pl.num_programs(1) - 1)
        def _emit():
            ko_ref[0] = kv_s[tile - WINDOW:tile, 0:KV_WIDTH]
            vo_ref[0] = kv_s[tile - WINDOW:tile, KV_WIDTH:2 * KV_WIDTH]
            for hd in range(n_hg):
                so_ref[0, hd] = st_s[hd].T
    else:
        ko_ref[0] = kv_s[:, 0:KV_WIDTH]
        vo_ref[0] = kv_s[:, KV_WIDTH:2 * KV_WIDTH]


def _full_spec(shape):
    nd = len(shape)
    return pl.BlockSpec(shape, lambda *_: (0,) * nd)


def _weight_spec(shape):
    nd = len(shape)
    return pl.BlockSpec(shape, lambda *_: (0,) * nd, pipeline_mode=pl.Buffered(1))


def _mixer(x, hist, norm_w, w_in, w_out, sinks, table, lb_params, gnorm_w, *, layer, carry, tile, chunk):
    nb, s, d = x.shape
    n_t = s // tile
    n_chunks = tile // chunk
    keys = WINDOW + chunk
    key_pad = 2 * LANES
    n_hg = d // HG_DK
    n_streams = nb if carry else s // chunk
    offsets = jnp.arange(chunk + key_pad - 1, dtype=jnp.int32) - (chunk - 1) - WINDOW
    by_offset = table[_t5_bucket(offsets)].astype(jnp.float32).T
    rel_bias = jnp.stack([by_offset[:, chunk - 1 - i:chunk - 1 - i + key_pad] for i in range(chunk)], axis=1)
    lmask = jnp.asarray(_level_masks(chunk))
    hmask = jnp.asarray(_row_masks(chunk))
    smem = pl.BlockSpec(memory_space=pltpu.SMEM)

    in_arrays = [x, norm_w.reshape(1, d), w_in, w_out, sinks, rel_bias, lb_params,
                 gnorm_w.reshape(1, HG_DK), lmask, hmask]
    in_specs = [pl.BlockSpec((1, tile, d), lambda b, t: (b, t, 0)), _full_spec((1, d)),
                _weight_spec(w_in.shape), _weight_spec(w_out.shape), smem, _full_spec(rel_bias.shape),
                _full_spec(lb_params.shape), _full_spec((1, HG_DK)),
                _full_spec(lmask.shape), _full_spec(hmask.shape)]
    if carry:
        kv_rows = WINDOW
        st_block = (1, n_hg, HG_DK, HG_DK)
        n_win, win_rows, n_var = 1, WINDOW + tile, WINDOW // chunk + 1
    else:
        in_arrays += list(hist)
        in_specs += [_full_spec(a.shape) for a in hist]
        kv_rows = s
        st_block = (n_streams, n_hg, HG_DK, HG_DK)
        n_win, win_rows, n_var = n_chunks, keys, 1
    out_shape = [jax.ShapeDtypeStruct((nb, s, d), jnp.float32),
                 jax.ShapeDtypeStruct((nb, kv_rows, KV_WIDTH), jnp.float32),
                 jax.ShapeDtypeStruct((nb, kv_rows, KV_WIDTH), jnp.float32),
                 jax.ShapeDtypeStruct((n_streams, n_hg, HG_DK, HG_DK), jnp.float32)]
    out_specs = [pl.BlockSpec((1, tile, d), lambda b, t: (b, t, 0)),
                 pl.BlockSpec((1, kv_rows, KV_WIDTH), lambda b, t: (b, 0, 0)),
                 pl.BlockSpec((1, kv_rows, KV_WIDTH), lambda b, t: (b, 0, 0)),
                 pl.BlockSpec(st_block, lambda b, t: (b, 0, 0, 0))]
    f32, bf16 = jnp.float32, jnp.bfloat16
    rows_g = (N_HEADS // 2 // N_KV_HEADS) * chunk
    scratch = [pltpu.VMEM((n_var, 4, rows_g, key_pad), f32),
               pltpu.VMEM((n_chunks, N_HEADS // 2, chunk, LANES), bf16),
               pltpu.VMEM((tile, 2 * KV_WIDTH), f32),
               pltpu.VMEM((4, n_win, win_rows, LANES), bf16),
               pltpu.VMEM((4, n_win, win_rows, LANES), bf16),
               pltpu.VMEM((tile, d), f32), pltpu.VMEM((tile, d), f32),
               pltpu.VMEM((tile, d), bf16)]
    scratch += [pltpu.VMEM((tile, d), f32) for _ in range(7)]
    scratch += [pltpu.VMEM((n_chunks, n_hg, HG_DK, HG_DK), f32),
                pltpu.VMEM((n_chunks, n_hg, HG_DK, HG_DK), bf16),
                pltpu.VMEM((n_chunks, 4, rows_g, key_pad), bf16),
                pltpu.VMEM((n_chunks, n_hg, chunk, chunk), bf16),
                pltpu.VMEM((tile, d), bf16)]
    if carry:
        scratch += [pltpu.VMEM((n_hg, HG_DK, HG_DK), f32)]
    kern = functools.partial(_mixer_kernel, tile=tile, chunk=chunk, carry=carry, d_model=d, layer=layer)
    return pl.pallas_call(
        kern,
        grid=(nb, n_t),
        in_specs=in_specs,
        out_specs=out_specs,
        out_shape=out_shape,
        scratch_shapes=scratch,
        compiler_params=pltpu.CompilerParams(
            dimension_semantics=("arbitrary", "arbitrary"), vmem_limit_bytes=VMEM_LIMIT_BYTES),
        name="mixer_prompt" if carry else "mixer_sample",
    )(*in_arrays)


def _ffn_kernel(xa_ref, xb_ref, nw_ref, wgu_ref, wd_ref, fw_ref, ya_ref, yb_ref, act_s, *, d_ff, col_tile,
                final_norm):
    n_a = pl.num_programs(0) - 1

    def ffn(x_ref, y_ref):
        rows = x_ref.shape[0]
        x = x_ref[...]
        h = _bf(_rms(x, nw_ref[...]))
        for j in range(d_ff // col_tile):
            g = _dot(h, _bf(wgu_ref[:, j * col_tile:(j + 1) * col_tile]))
            u = _dot(h, _bf(wgu_ref[:, d_ff + j * col_tile:d_ff + (j + 1) * col_tile]))
            act_s[0:rows, j * col_tile:(j + 1) * col_tile] = _bf(jax.nn.silu(g) * u)
        y = x + _dot(act_s[0:rows, :], _bf(wd_ref[...]))
        if final_norm:
            y = _rms(y, fw_ref[...])
        y_ref[...] = y

    @pl.when(pl.program_id(0) < n_a)
    def _first_set():
        ffn(xa_ref, ya_ref)

    @pl.when(pl.program_id(0) == n_a)
    def _second_set():
        ffn(xb_ref, yb_ref)


def _ffn(xa, xb, norm_w, w_gate_up, w_down, final_w, *, tile, final_norm):
    n, d = xa.shape
    m = xb.shape[0]
    n_a = n // tile
    assert n == n_a * tile and m <= tile
    d_ff = w_down.shape[0]
    kern = functools.partial(_ffn_kernel, d_ff=d_ff, col_tile=2 * LANES, final_norm=final_norm)
    a_spec = pl.BlockSpec((tile, d), lambda i: (jnp.minimum(i, n_a - 1), 0))
    return pl.pallas_call(
        kern,
        grid=(n_a + 1,),
        in_specs=[a_spec, _full_spec((m, d)), _full_spec((1, d)),
                  _weight_spec(w_gate_up.shape), _weight_spec(w_down.shape), _full_spec((1, d))],
        out_specs=[a_spec, _full_spec((m, d))],
        out_shape=[jax.ShapeDtypeStruct((n, d), jnp.float32), jax.ShapeDtypeStruct((m, d), jnp.float32)],
        scratch_shapes=[pltpu.VMEM((tile, d_ff), jnp.bfloat16)],
        compiler_params=pltpu.CompilerParams(
            dimension_semantics=("arbitrary",), vmem_limit_bytes=VMEM_LIMIT_BYTES),
        name="ffn",
    )(xa, xb, norm_w.reshape(1, d), w_gate_up, w_down, final_w.reshape(1, d))


def kernel(x_prompt, x_sample, cache_k, cache_v, state_hgrn, norm_mix, w_in, w_out, attn_sinks, rel_bias_table,
           hgrn_lb, hgrn_norm, norm_ffn, w_gate_up, w_down, norm_final):
    depth = w_in.shape[0]
    batch, seq, d = x_prompt.shape
    dec_batch, dec_seq, _ = x_sample.shape
    xp = x_prompt
    xs = x_sample.reshape(1, dec_batch * dec_seq, d)
    outs = [[] for _ in range(6)]
    for l in range(depth):
        w_in_l, w_out_l = _bf(w_in[l]), _bf(w_out[l])
        shared = (norm_mix[l], w_in_l, w_out_l, attn_sinks[l], rel_bias_table, hgrn_lb, hgrn_norm[l])
        xp, kp, vp, sp = _mixer(xp, None, *shared, layer=l, carry=True, tile=4 * CHUNK, chunk=CHUNK)
        hist = (cache_k[l].reshape(dec_batch, WINDOW, KV_WIDTH), cache_v[l].reshape(dec_batch, WINDOW, KV_WIDTH),
                state_hgrn[l])
        xs, ks, vs, ss = _mixer(xs, hist, *shared, layer=l, carry=False, tile=dec_batch * dec_seq, chunk=dec_seq)
        last = l == depth - 1
        xp, xs = _ffn(xp.reshape(batch * seq, d), xs.reshape(dec_batch * dec_seq, d), norm_ffn[l], w_gate_up[l],
                      w_down[l], norm_final, tile=512, final_norm=last)
        xp = xp.reshape(batch, seq, d)
        xs = xs.reshape(1, dec_batch * dec_seq, d)
        ks = jnp.concatenate([hist[0][:, dec_seq:], ks.reshape(dec_batch, dec_seq, KV_WIDTH)], axis=1)
        vs = jnp.concatenate([hist[1][:, dec_seq:], vs.reshape(dec_batch, dec_seq, KV_WIDTH)], axis=1)
        kv_shape = (-1, WINDOW, N_KV_HEADS, HEAD_DIM)
        for acc, val in zip(outs, (kp.reshape(kv_shape), vp.reshape(kv_shape), sp,
                                   ks.reshape(kv_shape), vs.reshape(kv_shape), ss)):
            acc.append(val)
    return (xp, xs.reshape(dec_batch, dec_seq, d)) + tuple(jnp.stack(o) for o in outs)
```

```python
import functools
import math

import numpy as np
import jax
import jax.numpy as jnp
from jax import lax
from jax.experimental import pallas as pl
from jax.experimental.pallas import tpu as pltpu

CHUNK = 64
N_HEADS = 16
N_KV_HEADS = 2
HEAD_DIM = 64
GROUP = N_HEADS // N_KV_HEADS
KV_WIDTH = N_KV_HEADS * HEAD_DIM
WINDOW = 128
ATTN_SCALE = HEAD_DIM ** -0.5
N_BUCKETS = 32
MAX_DISTANCE = 128
HG_DK = 128
RMS_EPS = 1e-6
DECAY_GUARD = 60.0

LANES = 128
SUBLANES = 8
VMEM_LIMIT_BYTES = 56 * 1024 * 1024

_NT = (((1,), (1,)), ((), ()))
_TN = (((0,), (0,)), ((), ()))


def _bf(x):
    return x.astype(jnp.bfloat16)


def _dot(a, b, dims=None):
    if dims is None:
        return jnp.dot(a, b, preferred_element_type=jnp.float32)
    return lax.dot_general(a, b, dims, preferred_element_type=jnp.float32)


def _rms(x, w):
    return x * lax.rsqrt(jnp.mean(x * x, axis=-1, keepdims=True) + RMS_EPS) * w


def _t5_bucket(rel):
    nb = N_BUCKETS // 2
    max_exact = nb // 2
    ret = jnp.where(rel > 0, nb, 0)
    n = jnp.abs(rel)
    nf = jnp.maximum(n, 1).astype(jnp.float32)
    large = max_exact + (jnp.log(nf / max_exact) / math.log(MAX_DISTANCE / max_exact)
                         * (nb - max_exact)).astype(jnp.int32)
    large = jnp.minimum(large, nb - 1)
    return ret + jnp.where(n < max_exact, n, large)


def _levels(c):
    out, m = [], c // 2
    while m >= 1:
        out.append(m)
        m //= 2
    return out


def _level_masks(c):
    t = np.arange(c)[:, None]
    s = np.arange(c)[None, :]
    masks = []
    for m in _levels(c):
        masks.append((t // (2 * m) == s // (2 * m)) & ((t // m) % 2 == 1) & ((s // m) % 2 == 0))
    masks += [t == s, s <= t]
    return np.stack(masks).astype(np.float32)


def _row_masks(c):
    r = np.arange(c)
    rows = [(r // m) % 2 == 1 for m in _levels(c) if m < 8]
    rows += [r % 4 == 0, r % 4 >= 2, r % 4 == 3]
    return np.repeat(np.stack(rows).astype(np.float32)[:, :, None], LANES, axis=2)


def _block_cumsum(x, block):
    r, n = x.shape
    tiles = x.reshape(r // SUBLANES, SUBLANES, n)
    row = lax.broadcasted_iota(jnp.int32, (1, SUBLANES, n), 1)
    for shift in (1, 2, 4):
        tiles = tiles + jnp.where(row >= shift, pltpu.roll(tiles, shift, axis=1), 0.0)
    out, before = [], None
    for i in range(r // SUBLANES):
        t = tiles[i] if before is None or i % (block // SUBLANES) == 0 else tiles[i] + before
        out.append(t)
        before = jnp.broadcast_to(t[SUBLANES - 1:SUBLANES, :], (SUBLANES, n))
    return jnp.concatenate(out, axis=0)


def _head_variants(a):
    lo = lax.broadcasted_iota(jnp.int32, a.shape, 1) < HEAD_DIM
    rot = pltpu.roll(a, HEAD_DIM, axis=1)
    return [_bf(jnp.where(lo, a, 0.0)), _bf(jnp.where(lo, 0.0, rot)),
            _bf(jnp.where(lo, rot, 0.0)), _bf(jnp.where(lo, 0.0, a))]


def _mixer_kernel(*refs, tile, chunk, carry, d_model, layer):
    n_chunks = tile // chunk
    keys = WINDOW + chunk
    key_pad = 2 * LANES
    assert keys < key_pad and tile >= WINDOW
    levels = _levels(chunk)
    n_hg = d_model // HG_DK
    att_w = N_HEADS * HEAD_DIM
    n_pairs = N_HEADS // 2
    ppk = n_pairs // N_KV_HEADS
    rows_g = ppk * chunk
    n_var = WINDOW // chunk + 1 if carry else 1
    f32 = jnp.float32

    refs = list(refs)
    (x_ref, nw_ref, win_ref, wout_ref, sinks_ref, relb_ref, lbp_ref, gn_ref,
     lmask_ref, hmask_ref) = refs[:10]
    refs = refs[10:]
    if not carry:
        hk_ref, hv_ref, si_ref = refs[:3]
        refs = refs[3:]
    y_ref, ko_ref, vo_ref, so_ref = refs[:4]
    refs = refs[4:]
    (bias_s, q4_s, kv_s, kx_s, vx_s, hq_s, kk_s, hi_s, lf_s, bc_s, og_s, ga_s, gh_s, att_s, hg_s,
     u_s, sb_s, p_s, a_s, qe_s) = refs[:20]
    if carry:
        st_s = refs[20]

    first = (pl.program_id(0) == 0) & (pl.program_id(1) == 0)
    t_idx = pl.program_id(1)

    @pl.when(first)
    def _build_bias():
        lane = lax.broadcasted_iota(jnp.int32, (chunk, key_pad), 1)
        for head in range(N_HEADS):
            acc = jnp.where(lane == keys, sinks_ref[head], jnp.where(lane > keys, -jnp.inf, relb_ref[head]))
            pair, side = divmod(head, 2)
            kvh, j = divmod(pair, ppk)
            for var in range(n_var):
                n_invalid = WINDOW - var * chunk if carry else 0
                bias_s[var, 2 * kvh + side, j * chunk:(j + 1) * chunk, :] = jnp.where(lane < n_invalid, -jnp.inf, acc)

    if carry:
        @pl.when(t_idx == 0)
        def _reset():
            kx_s[:, 0, 0:WINDOW, :] = jnp.zeros((4, WINDOW, LANES), jnp.bfloat16)
            vx_s[:, 0, 0:WINDOW, :] = jnp.zeros((4, WINDOW, LANES), jnp.bfloat16)
            st_s[...] = jnp.zeros_like(st_s)

        @pl.when(t_idx > 0)
        def _shift():
            kx_s[:, 0, 0:WINDOW, :] = kx_s[:, 0, tile:tile + WINDOW, :]
            vx_s[:, 0, 0:WINDOW, :] = vx_s[:, 0, tile:tile + WINDOW, :]

    x = x_ref[0]
    h = _bf(_rms(x, nw_ref[...]))

    def proj(i0, width):
        return _dot(h, win_ref[:, i0:i0 + width])

    off = 0
    q = _bf(proj(off, att_w) * ATTN_SCALE)
    for c in range(n_chunks):
        for j in range(n_pairs):
            q4_s[c, j] = q[c * chunk:(c + 1) * chunk, j * LANES:(j + 1) * LANES]
    off += att_w
    kv = proj(off, 2 * KV_WIDTH)
    kv_s[...] = kv
    off += 2 * KV_WIDTH
    k_var = _head_variants(kv[:, 0:KV_WIDTH])
    v_var = _head_variants(kv[:, KV_WIDTH:2 * KV_WIDTH])
    if carry:
        for g in range(4):
            kx_s[g, 0, WINDOW:WINDOW + tile, :] = k_var[g]
            vx_s[g, 0, WINDOW:WINDOW + tile, :] = v_var[g]
    else:
        for c in range(n_chunks):
            hk = _head_variants(hk_ref[c])
            hv = _head_variants(hv_ref[c])
            for g in range(4):
                kx_s[g, c, 0:WINDOW, :] = hk[g]
                vx_s[g, c, 0:WINDOW, :] = hv[g]
                kx_s[g, c, WINDOW:keys, :] = k_var[g][c * chunk:(c + 1) * chunk]
                vx_s[g, c, WINDOW:keys, :] = v_var[g][c * chunk:(c + 1) * chunk]
    hq_s[...] = proj(off, d_model)
    off += d_model
    lbp = lbp_ref[...]
    e = jnp.exp(lbp - jnp.max(lbp, axis=0, keepdims=True))
    lb = jnp.sum(e[:layer + 1], axis=0, keepdims=True) / jnp.sum(e, axis=0, keepdims=True)
    f = lb + (1.0 - lb) * jax.nn.sigmoid(proj(off, d_model))
    off += d_model
    kk_s[...] = 1.0 - f
    lf = jnp.log(f)
    lf_s[...] = lf
    bc_s[...] = _block_cumsum(lf, chunk)
    hi_s[...] = _bf(proj(off, d_model))
    off += d_model
    og_s[...] = jax.nn.silu(proj(off, d_model))
    off += d_model
    ga_s[...] = jax.nn.sigmoid(proj(off, d_model))
    off += d_model
    gh_s[...] = jax.nn.sigmoid(proj(off, d_model))

    zpad = jnp.zeros((key_pad - keys, LANES), jnp.bfloat16)

    def window(ref, g, c):
        w = ref[g, 0, c * chunk:c * chunk + keys, :] if carry else ref[g, c]
        return jnp.concatenate([w, zpad], axis=0)

    for c in range(n_chunks):
        var = jnp.minimum(t_idx * n_chunks + c, n_var - 1) if carry else 0
        for kvh in range(N_KV_HEADS):
            qg = q4_s[c, kvh * ppk:(kvh + 1) * ppk].reshape(rows_g, LANES)
            for side in range(2):
                g = 2 * kvh + side
                s = _dot(qg, window(kx_s, g, c), _NT) + bias_s[var, g]
                ex = jnp.exp(s - jnp.max(s, axis=-1, keepdims=True))
                p_s[c, g] = _bf(ex * (1.0 / jnp.sum(ex, axis=-1, keepdims=True)))
    for c in range(n_chunks):
        for kvh in range(N_KV_HEADS):
            o = (_dot(p_s[c, 2 * kvh], window(vx_s, 2 * kvh, c))
                 + _dot(p_s[c, 2 * kvh + 1], window(vx_s, 2 * kvh + 1, c)))
            for j in range(ppk):
                col = (kvh * ppk + j) * LANES
                att_s[c * chunk:(c + 1) * chunk, col:col + LANES] = o[j * chunk:(j + 1) * chunk]

    gn = gn_ref[...]
    small = [m for m in levels if m < 8]

    def unit(c, hd):
        return slice(c * chunk, (c + 1) * chunk), slice(hd * HG_DK, (hd + 1) * HG_DK)

    def bc_row(c, r, cols):
        return bc_s[c * chunk + r:c * chunk + r + 1, cols]

    def level_operand(m, c, hd):
        rows, cols = unit(c, hd)
        q = hq_s[rows, cols]
        kk = kk_s[rows, cols]
        if m >= 8:
            bc = bc_s[rows, cols]
            parts = []
            for base in range(0, chunk, 2 * m):
                ref = bc_row(c, base + m - 1, cols)
                lo, hi = slice(base, base + m), slice(base + m, base + 2 * m)
                parts += [kk[lo] * jnp.exp(ref - bc[lo]), q[hi] * jnp.exp(bc[hi] - ref)]
            return _bf(jnp.concatenate(parts, axis=0))
        upper = hmask_ref[small.index(m)] > 0.5
        if m == 1:
            return _bf(jnp.where(upper, q * (1.0 - kk), kk))
        if m == 4:
            bc = bc_s[rows, cols]
            ref_row = bc.reshape(chunk // 8, 8, HG_DK)[:, 3:4, :]
            ref = jnp.broadcast_to(ref_row, (chunk // 8, 8, HG_DK)).reshape(chunk, HG_DK)
            arg = -jnp.abs(bc - ref)
        else:
            lfh = lf_s[rows, cols]
            nxt = pltpu.roll(lfh, chunk - 1, axis=0)
            prv = pltpu.roll(lfh, 1, axis=0)
            n_small = len(small)
            arg = hmask_ref[n_small] * nxt + hmask_ref[n_small + 1] * lfh + hmask_ref[n_small + 2] * prv
        return _bf(jnp.where(upper, q, kk) * jnp.exp(arg))

    for c in range(n_chunks):
        for hd in range(n_hg):
            rows, cols = unit(c, hd)
            bc = bc_s[rows, cols]
            k_dec = _bf(kk_s[rows, cols] * jnp.exp(bc_row(c, chunk - 1, cols) - bc))
            u_s[c, hd] = _dot(hi_s[rows, cols], k_dec, _TN)
            qe_s[rows, cols] = _bf(hq_s[rows, cols] * jnp.exp(bc))

    for hd in range(n_hg):
        if carry:
            s = st_s[hd]
        for c in range(n_chunks):
            rows, cols = unit(c, hd)
            decay = jnp.exp(bc_row(c, chunk - 1, cols))
            if carry:
                sb_s[c, hd] = _bf(s)
                s = s * decay + u_s[c, hd]
            else:
                s0 = si_ref[c, hd].T
                sb_s[c, hd] = _bf(s0)
                so_ref[c, hd] = (s0 * decay + u_s[c, hd]).T
        if carry:
            st_s[hd] = s

    total_decay = jnp.concatenate([bc_row(c, chunk - 1, slice(None)) for c in range(n_chunks)], axis=0)
    mild = jnp.max(-total_decay) <= DECAY_GUARD

    @pl.when(mild)
    def _direct():
        tril = lmask_ref[len(levels) + 1] > 0.5
        for c in range(n_chunks):
            for hd in range(n_hg):
                rows, cols = unit(c, hd)
                k_grow = _bf(kk_s[rows, cols] * jnp.exp(-bc_s[rows, cols]))
                a_s[c, hd] = _bf(jnp.where(tril, _dot(qe_s[rows, cols], k_grow, _NT), 0.0))

    @pl.when(jnp.logical_not(mild))
    def _split():
        for c in range(n_chunks):
            for hd in range(n_hg):
                rows, cols = unit(c, hd)
                a = _dot(_bf(hq_s[rows, cols]), _bf(kk_s[rows, cols]), _NT) * lmask_ref[len(levels)]
                for li, m in enumerate(levels):
                    z = level_operand(m, c, hd)
                    a = a + _dot(z, z, _NT) * lmask_ref[li]
                a_s[c, hd] = _bf(a)

    for c in range(n_chunks):
        for hd in range(n_hg):
            rows, cols = unit(c, hd)
            o = _dot(qe_s[rows, cols], sb_s[c, hd], _NT) + _dot(a_s[c, hd], hi_s[rows, cols])
            o = o * lax.rsqrt(jnp.mean(o * o, axis=-1, keepdims=True) + RMS_EPS) * gn
            hg_s[rows, cols] = o * og_s[rows, cols]

    merged = _bf(ga_s[...] * att_s[...] + gh_s[...] * hg_s[...])
    y_ref[0] = x + _dot(merged, wout_ref[...])

    if carry:
        @pl.when(t_idx == pl.num_programs(1) - 1)
        def _emit():
            ko_ref[0] = kv_s[tile - WINDOW:tile, 0:KV_WIDTH]
            vo_ref[0] = kv_s[tile - WINDOW:tile, KV_WIDTH:2 * KV_WIDTH]
            for hd in range(n_hg):
                so_ref[0, hd] = st_s[hd].T
    else:
        ko_ref[0] = kv_s[:, 0:KV_WIDTH]
        vo_ref[0] = kv_s[:, KV_WIDTH:2 * KV_WIDTH]


def _full_spec(shape):
    nd = len(shape)
    return pl.BlockSpec(shape, lambda *_: (0,) * nd)


def _weight_spec(shape):
    nd = len(shape)
    return pl.BlockSpec(shape, lambda *_: (0,) * nd, pipeline_mode=pl.Buffered(1))


def _mixer(x, hist, norm_w, w_in, w_out, sinks, table, lb_params, gnorm_w, *, layer, carry, tile, chunk):
    nb, s, d = x.shape
    n_t = s // tile
    n_chunks = tile // chunk
    keys = WINDOW + chunk
    key_pad = 2 * LANES
    n_hg = d // HG_DK
    n_streams = nb if carry else s // chunk
    n_off = chunk + key_pad - 1
    offsets = jnp.arange(n_off, dtype=jnp.int32) - (chunk - 1) - WINDOW
    by_offset = table[_t5_bucket(offsets)].astype(jnp.float32).T
    period = jnp.roll(jnp.pad(by_offset, ((0, 0), (0, 1))), -(chunk - 1), axis=1)
    skewed = jnp.tile(period, (1, chunk))[:, :chunk * n_off].reshape(N_HEADS, chunk, n_off)
    rel_bias = skewed[:, :, :key_pad]
    lmask = jnp.asarray(_level_masks(chunk))
    hmask = jnp.asarray(_row_masks(chunk))
    smem = pl.BlockSpec(memory_space=pltpu.SMEM)

    in_arrays = [x, norm_w.reshape(1, d), w_in, w_out, sinks, rel_bias, lb_params,
                 gnorm_w.reshape(1, HG_DK), lmask, hmask]
    in_specs = [pl.BlockSpec((1, tile, d), lambda b, t: (b, t, 0)), _full_spec((1, d)),
                _weight_spec(w_in.shape), _weight_spec(w_out.shape), smem, _full_spec(rel_bias.shape),
                _full_spec(lb_params.shape), _full_spec((1, HG_DK)),
                _full_spec(lmask.shape), _full_spec(hmask.shape)]
    if carry:
        kv_rows = WINDOW
        st_block = (1, n_hg, HG_DK, HG_DK)
        n_win, win_rows, n_var = 1, WINDOW + tile, WINDOW // chunk + 1
    else:
        in_arrays += list(hist)
        in_specs += [_full_spec(a.shape) for a in hist]
        kv_rows = s
        st_block = (n_streams, n_hg, HG_DK, HG_DK)
        n_win, win_rows, n_var = n_chunks, keys, 1
    out_shape = [jax.ShapeDtypeStruct((nb, s, d), jnp.float32),
                 jax.ShapeDtypeStruct((nb, kv_rows, KV_WIDTH), jnp.float32),
                 jax.ShapeDtypeStruct((nb, kv_rows, KV_WIDTH), jnp.float32),
                 jax.ShapeDtypeStruct((n_streams, n_hg, HG_DK, HG_DK), jnp.float32)]
    out_specs = [pl.BlockSpec((1, tile, d), lambda b, t: (b, t, 0)),
                 pl.BlockSpec((1, kv_rows, KV_WIDTH), lambda b, t: (b, 0, 0)),
                 pl.BlockSpec((1, kv_rows, KV_WIDTH), lambda b, t: (b, 0, 0)),
                 pl.BlockSpec(st_block, lambda b, t: (b, 0, 0, 0))]
    f32, bf16 = jnp.float32, jnp.bfloat16
    rows_g = (N_HEADS // 2 // N_KV_HEADS) * chunk
    scratch = [pltpu.VMEM((n_var, 4, rows_g, key_pad), f32),
               pltpu.VMEM((n_chunks, N_HEADS // 2, chunk, LANES), bf16),
               pltpu.VMEM((tile, 2 * KV_WIDTH), f32),
               pltpu.VMEM((4, n_win, win_rows, LANES), bf16),
               pltpu.VMEM((4, n_win, win_rows, LANES), bf16),
               pltpu.VMEM((tile, d), f32), pltpu.VMEM((tile, d), f32),
               pltpu.VMEM((tile, d), bf16)]
    scratch += [pltpu.VMEM((tile, d), f32) for _ in range(7)]
    scratch += [pltpu.VMEM((n_chunks, n_hg, HG_DK, HG_DK), f32),
                pltpu.VMEM((n_chunks, n_hg, HG_DK, HG_DK), bf16),
                pltpu.VMEM((n_chunks, 4, rows_g, key_pad), bf16),
                pltpu.VMEM((n_chunks, n_hg, chunk, chunk), bf16),
                pltpu.VMEM((tile, d), bf16)]
    if carry:
        scratch += [pltpu.VMEM((n_hg, HG_DK, HG_DK), f32)]
    kern = functools.partial(_mixer_kernel, tile=tile, chunk=chunk, carry=carry, d_model=d, layer=layer)
    return pl.pallas_call(
        kern,
        grid=(nb, n_t),
        in_specs=in_specs,
        out_specs=out_specs,
        out_shape=out_shape,
        scratch_shapes=scratch,
        compiler_params=pltpu.CompilerParams(
            dimension_semantics=("arbitrary", "arbitrary"), vmem_limit_bytes=VMEM_LIMIT_BYTES),
        name="mixer_prompt" if carry else "mixer_sample",
    )(*in_arrays)


def _ffn_kernel(xa_ref, xb_ref, nw_ref, wgu_ref, wd_ref, fw_ref, ya_ref, yb_ref, act_s, *, d_ff, col_tile,
                final_norm):
    n_a = pl.num_programs(0) - 1

    def ffn(x_ref, y_ref):
        rows = x_ref.shape[0]
        x = x_ref[...]
        h = _bf(_rms(x, nw_ref[...]))
        for j in range(d_ff // col_tile):
            g = _dot(h, _bf(wgu_ref[:, j * col_tile:(j + 1) * col_tile]))
            u = _dot(h, _bf(wgu_ref[:, d_ff + j * col_tile:d_ff + (j + 1) * col_tile]))
            act_s[0:rows, j * col_tile:(j + 1) * col_tile] = _bf(jax.nn.silu(g) * u)
        y = x + _dot(act_s[0:rows, :], _bf(wd_ref[...]))
        if final_norm:
            y = _rms(y, fw_ref[...])
        y_ref[...] = y

    @pl.when(pl.program_id(0) < n_a)
    def _first_set():
        ffn(xa_ref, ya_ref)

    @pl.when(pl.program_id(0) == n_a)
    def _second_set():
        ffn(xb_ref, yb_ref)


def _ffn(xa, xb, norm_w, w_gate_up, w_down, final_w, *, tile, final_norm):
    n, d = xa.shape
    m = xb.shape[0]
    n_a = n // tile
    assert n == n_a * tile and m <= tile
    d_ff = w_down.shape[0]
    kern = functools.partial(_ffn_kernel, d_ff=d_ff, col_tile=2 * LANES, final_norm=final_norm)
    a_spec = pl.BlockSpec((tile, d), lambda i: (jnp.minimum(i, n_a - 1), 0))
    return pl.pallas_call(
        kern,
        grid=(n_a + 1,),
        in_specs=[a_spec, _full_spec((m, d)), _full_spec((1, d)),
                  _weight_spec(w_gate_up.shape), _weight_spec(w_down.shape), _full_spec((1, d))],
        out_specs=[a_spec, _full_spec((m, d))],
        out_shape=[jax.ShapeDtypeStruct((n, d), jnp.float32), jax.ShapeDtypeStruct((m, d), jnp.float32)],
        scratch_shapes=[pltpu.VMEM((tile, d_ff), jnp.bfloat16)],
        compiler_params=pltpu.CompilerParams(
            dimension_semantics=("arbitrary",), vmem_limit_bytes=VMEM_LIMIT_BYTES),
        name="ffn",
    )(xa, xb, norm_w.reshape(1, d), w_gate_up, w_down, final_w.reshape(1, d))


def kernel(x_prompt, x_sample, cache_k, cache_v, state_hgrn, norm_mix, w_in, w_out, attn_sinks, rel_bias_table,
           hgrn_lb, hgrn_norm, norm_ffn, w_gate_up, w_down, norm_final):
    depth = w_in.shape[0]
    batch, seq, d = x_prompt.shape
    dec_batch, dec_seq, _ = x_sample.shape
    xp = x_prompt
    xs = x_sample.reshape(1, dec_batch * dec_seq, d)
    outs = [[] for _ in range(6)]
    for l in range(depth):
        w_in_l, w_out_l = _bf(w_in[l]), _bf(w_out[l])
        shared = (norm_mix[l], w_in_l, w_out_l, attn_sinks[l], rel_bias_table, hgrn_lb, hgrn_norm[l])
        xp, kp, vp, sp = _mixer(xp, None, *shared, layer=l, carry=True, tile=4 * CHUNK, chunk=CHUNK)
        hist = (cache_k[l].reshape(dec_batch, WINDOW, KV_WIDTH), cache_v[l].reshape(dec_batch, WINDOW, KV_WIDTH),
                state_hgrn[l])
        xs, ks, vs, ss = _mixer(xs, hist, *shared, layer=l, carry=False, tile=dec_batch * dec_seq, chunk=dec_seq)
        last = l == depth - 1
        xp, xs = _ffn(xp.reshape(batch * seq, d), xs.reshape(dec_batch * dec_seq, d), norm_ffn[l], w_gate_up[l],
                      w_down[l], norm_final, tile=512, final_norm=last)
        xp = xp.reshape(batch, seq, d)
        xs = xs.reshape(1, dec_batch * dec_seq, d)
        ks = jnp.concatenate([hist[0][:, dec_seq:], ks.reshape(dec_batch, dec_seq, KV_WIDTH)], axis=1)
        vs = jnp.concatenate([hist[1][:, dec_seq:], vs.reshape(dec_batch, dec_seq, KV_WIDTH)], axis=1)
        kv_shape = (-1, WINDOW, N_KV_HEADS, HEAD_DIM)
        for acc, val in zip(outs, (kp.reshape(kv_shape), vp.reshape(kv_shape), sp,
                                   ks.reshape(kv_shape), vs.reshape(kv_shape), ss)):
            acc.append(val)
    return (xp, xs.reshape(dec_batch, dec_seq, d)) + tuple(jnp.stack(o) for o in outs)
```

```python
import functools
import math

import numpy as np
import jax
import jax.numpy as jnp
from jax import lax
from jax.experimental import pallas as pl
from jax.experimental.pallas import tpu as pltpu

CHUNK = 64
N_HEADS = 16
N_KV_HEADS = 2
HEAD_DIM = 64
GROUP = N_HEADS // N_KV_HEADS
KV_WIDTH = N_KV_HEADS * HEAD_DIM
WINDOW = 128
ATTN_SCALE = HEAD_DIM ** -0.5
N_BUCKETS = 32
MAX_DISTANCE = 128
HG_DK = 128
RMS_EPS = 1e-6
DECAY_GUARD = 75.0

LANES = 128
SUBLANES = 8
VMEM_LIMIT_BYTES = 56 * 1024 * 1024

_NT = (((1,), (1,)), ((), ()))
_TN = (((0,), (0,)), ((), ()))


def _bf(x):
    return x.astype(jnp.bfloat16)


def _dot(a, b, dims=None):
    if dims is None:
        return jnp.dot(a, b, preferred_element_type=jnp.float32)
    return lax.dot_general(a, b, dims, preferred_element_type=jnp.float32)


def _rms(x, w):
    return x * lax.rsqrt(jnp.mean(x * x, axis=-1, keepdims=True) + RMS_EPS) * w


def _t5_bucket(rel):
    nb = N_BUCKETS // 2
    max_exact = nb // 2
    ret = jnp.where(rel > 0, nb, 0)
    n = jnp.abs(rel)
    nf = jnp.maximum(n, 1).astype(jnp.float32)
    large = max_exact + (jnp.log(nf / max_exact) / math.log(MAX_DISTANCE / max_exact)
                         * (nb - max_exact)).astype(jnp.int32)
    large = jnp.minimum(large, nb - 1)
    return ret + jnp.where(n < max_exact, n, large)


def _levels(c):
    out, m = [], c // 2
    while m >= 1:
        out.append(m)
        m //= 2
    return out


def _level_masks(c):
    t = np.arange(c)[:, None]
    s = np.arange(c)[None, :]
    masks = []
    for m in _levels(c):
        masks.append((t // (2 * m) == s // (2 * m)) & ((t // m) % 2 == 1) & ((s // m) % 2 == 0))
    masks += [t == s, s <= t]
    return np.stack(masks).astype(np.float32)


def _row_masks(c):
    r = np.arange(c)
    rows = [(r // m) % 2 == 1 for m in _levels(c) if m < 8]
    rows += [r % 4 == 0, r % 4 >= 2, r % 4 == 3]
    return np.repeat(np.stack(rows).astype(np.float32)[:, :, None], LANES, axis=2)


def _block_cumsum(x, block):
    r, n = x.shape
    tiles = x.reshape(r // SUBLANES, SUBLANES, n)
    row = lax.broadcasted_iota(jnp.int32, (1, SUBLANES, n), 1)
    for shift in (1, 2, 4):
        tiles = tiles + jnp.where(row >= shift, pltpu.roll(tiles, shift, axis=1), 0.0)
    out, before = [], None
    for i in range(r // SUBLANES):
        t = tiles[i] if before is None or i % (block // SUBLANES) == 0 else tiles[i] + before
        out.append(t)
        before = jnp.broadcast_to(t[SUBLANES - 1:SUBLANES, :], (SUBLANES, n))
    return jnp.concatenate(out, axis=0)


def _head_variants(a):
    lo = lax.broadcasted_iota(jnp.int32, a.shape, 1) < HEAD_DIM
    rot = pltpu.roll(a, HEAD_DIM, axis=1)
    return [_bf(jnp.where(lo, a, 0.0)), _bf(jnp.where(lo, 0.0, rot)),
            _bf(jnp.where(lo, rot, 0.0)), _bf(jnp.where(lo, 0.0, a))]


def _mixer_kernel(*refs, tile, chunk, hg_block, carry, d_model, layer):
    n_chunks = tile // chunk
    keys = WINDOW + chunk
    key_pad = 2 * LANES
    assert keys < key_pad and tile >= WINDOW
    n_hg = d_model // HG_DK
    att_w = N_HEADS * HEAD_DIM
    n_pairs = N_HEADS // 2
    ppk = n_pairs // N_KV_HEADS
    rows_g = ppk * chunk
    n_var = WINDOW // chunk + 1 if carry else 1
    f32 = jnp.float32

    refs = list(refs)
    (x_ref, nw_ref, win_ref, wout_ref, sinks_ref, relb_ref, lbp_ref, gn_ref,
     lmask_ref, hmask_ref) = refs[:10]
    refs = refs[10:]
    if not carry:
        hk_ref, hv_ref, si_ref = refs[:3]
        refs = refs[3:]
    y_ref, ko_ref, vo_ref, so_ref = refs[:4]
    refs = refs[4:]
    (bias_s, q4_s, kv_s, kx_s, vx_s, hq_s, kk_s, hi_s, lf_s, bc_s, og_s, ga_s, gh_s, att_s, hg_s,
     u_s, sb_s, p_s, a_s, qe_s) = refs[:20]
    if carry:
        st_s = refs[20]

    first = (pl.program_id(0) == 0) & (pl.program_id(1) == 0)
    t_idx = pl.program_id(1)

    @pl.when(first)
    def _build_bias():
        lane = lax.broadcasted_iota(jnp.int32, (chunk, key_pad), 1)
        for head in range(N_HEADS):
            acc = jnp.where(lane == keys, sinks_ref[head], jnp.where(lane > keys, -jnp.inf, relb_ref[head]))
            pair, side = divmod(head, 2)
            kvh, j = divmod(pair, ppk)
            for var in range(n_var):
                n_invalid = WINDOW - var * chunk if carry else 0
                bias_s[var, 2 * kvh + side, j * chunk:(j + 1) * chunk, :] = jnp.where(lane < n_invalid, -jnp.inf, acc)

    if carry:
        @pl.when(t_idx == 0)
        def _reset():
            kx_s[:, 0, 0:WINDOW, :] = jnp.zeros((4, WINDOW, LANES), jnp.bfloat16)
            vx_s[:, 0, 0:WINDOW, :] = jnp.zeros((4, WINDOW, LANES), jnp.bfloat16)
            st_s[...] = jnp.zeros_like(st_s)

        @pl.when(t_idx > 0)
        def _shift():
            kx_s[:, 0, 0:WINDOW, :] = kx_s[:, 0, tile:tile + WINDOW, :]
            vx_s[:, 0, 0:WINDOW, :] = vx_s[:, 0, tile:tile + WINDOW, :]

    x = x_ref[0]
    h = _bf(_rms(x, nw_ref[...]))

    def proj(i0, width):
        return _dot(h, win_ref[:, i0:i0 + width])

    off = 0
    q = _bf(proj(off, att_w) * ATTN_SCALE)
    for c in range(n_chunks):
        for j in range(n_pairs):
            q4_s[c, j] = q[c * chunk:(c + 1) * chunk, j * LANES:(j + 1) * LANES]
    off += att_w
    kv = proj(off, 2 * KV_WIDTH)
    kv_s[...] = kv
    off += 2 * KV_WIDTH
    k_var = _head_variants(kv[:, 0:KV_WIDTH])
    v_var = _head_variants(kv[:, KV_WIDTH:2 * KV_WIDTH])
    if carry:
        for g in range(4):
            kx_s[g, 0, WINDOW:WINDOW + tile, :] = k_var[g]
            vx_s[g, 0, WINDOW:WINDOW + tile, :] = v_var[g]
    else:
        for c in range(n_chunks):
            hk = _head_variants(hk_ref[c])
            hv = _head_variants(hv_ref[c])
            for g in range(4):
                kx_s[g, c, 0:WINDOW, :] = hk[g]
                vx_s[g, c, 0:WINDOW, :] = hv[g]
                kx_s[g, c, WINDOW:keys, :] = k_var[g][c * chunk:(c + 1) * chunk]
                vx_s[g, c, WINDOW:keys, :] = v_var[g][c * chunk:(c + 1) * chunk]
    hq_s[...] = proj(off, d_model)
    off += d_model
    lbp = lbp_ref[...]
    e = jnp.exp(lbp - jnp.max(lbp, axis=0, keepdims=True))
    lb = jnp.sum(e[:layer + 1], axis=0, keepdims=True) / jnp.sum(e, axis=0, keepdims=True)
    f = lb + (1.0 - lb) * jax.nn.sigmoid(proj(off, d_model))
    off += d_model
    kk_s[...] = 1.0 - f
    lf = jnp.log(f)
    lf_s[...] = lf
    bc_s[...] = _block_cumsum(lf, hg_block)
    hi_s[...] = _bf(proj(off, d_model))
    off += d_model
    og_s[...] = jax.nn.silu(proj(off, d_model))
    off += d_model
    ga_s[...] = jax.nn.sigmoid(proj(off, d_model))
    off += d_model
    gh_s[...] = jax.nn.sigmoid(proj(off, d_model))

    zpad = jnp.zeros((key_pad - keys, LANES), jnp.bfloat16)

    def window(ref, g, c):
        w = ref[g, 0, c * chunk:c * chunk + keys, :] if carry else ref[g, c]
        return jnp.concatenate([w, zpad], axis=0)

    for c in range(n_chunks):
        var = jnp.minimum(t_idx * n_chunks + c, n_var - 1) if carry else 0
        for kvh in range(N_KV_HEADS):
            qg = q4_s[c, kvh * ppk:(kvh + 1) * ppk].reshape(rows_g, LANES)
            for side in range(2):
                g = 2 * kvh + side
                s = _dot(qg, window(kx_s, g, c), _NT) + bias_s[var, g]
                ex = jnp.exp(s - jnp.max(s, axis=-1, keepdims=True))
                p_s[c, g] = _bf(ex * (1.0 / jnp.sum(ex, axis=-1, keepdims=True)))
    for c in range(n_chunks):
        for kvh in range(N_KV_HEADS):
            o = (_dot(p_s[c, 2 * kvh], window(vx_s, 2 * kvh, c))
                 + _dot(p_s[c, 2 * kvh + 1], window(vx_s, 2 * kvh + 1, c)))
            for j in range(ppk):
                col = (kvh * ppk + j) * LANES
                att_s[c * chunk:(c + 1) * chunk, col:col + LANES] = o[j * chunk:(j + 1) * chunk]

    blk, n_blk = hg_block, tile // hg_block
    levels = _levels(blk)
    gn = gn_ref[...]
    small = [m for m in levels if m < 8]

    def unit(c, hd):
        return slice(c * blk, (c + 1) * blk), slice(hd * HG_DK, (hd + 1) * HG_DK)

    def bc_row(c, r, cols):
        return bc_s[c * blk + r:c * blk + r + 1, cols]

    def level_operand(m, c, hd):
        rows, cols = unit(c, hd)
        q = hq_s[rows, cols]
        kk = kk_s[rows, cols]
        if m >= 8:
            bc = bc_s[rows, cols]
            parts = []
            for base in range(0, blk, 2 * m):
                ref = bc_row(c, base + m - 1, cols)
                lo, hi = slice(base, base + m), slice(base + m, base + 2 * m)
                parts += [kk[lo] * jnp.exp(ref - bc[lo]), q[hi] * jnp.exp(bc[hi] - ref)]
            return _bf(jnp.concatenate(parts, axis=0))
        upper = hmask_ref[small.index(m)] > 0.5
        if m == 1:
            return _bf(jnp.where(upper, q * (1.0 - kk), kk))
        if m == 4:
            bc = bc_s[rows, cols]
            ref_row = bc.reshape(blk // 8, 8, HG_DK)[:, 3:4, :]
            ref = jnp.broadcast_to(ref_row, (blk // 8, 8, HG_DK)).reshape(blk, HG_DK)
            arg = -jnp.abs(bc - ref)
        else:
            lfh = lf_s[rows, cols]
            nxt = pltpu.roll(lfh, blk - 1, axis=0)
            prv = pltpu.roll(lfh, 1, axis=0)
            n_small = len(small)
            arg = hmask_ref[n_small] * nxt + hmask_ref[n_small + 1] * lfh + hmask_ref[n_small + 2] * prv
        return _bf(jnp.where(upper, q, kk) * jnp.exp(arg))

    for c in range(n_blk):
        for hd in range(n_hg):
            rows, cols = unit(c, hd)
            bc = bc_s[rows, cols]
            k_dec = _bf(kk_s[rows, cols] * jnp.exp(bc_row(c, blk - 1, cols) - bc))
            u_s[c, hd] = _dot(hi_s[rows, cols], k_dec, _TN)
            qe_s[rows, cols] = _bf(hq_s[rows, cols] * jnp.exp(bc))

    for hd in range(n_hg):
        if carry:
            s = st_s[hd]
        for c in range(n_blk):
            rows, cols = unit(c, hd)
            decay = jnp.exp(bc_row(c, blk - 1, cols))
            if carry:
                sb_s[c, hd] = _bf(s)
                s = s * decay + u_s[c, hd]
            else:
                s0 = si_ref[c, hd].T
                sb_s[c, hd] = _bf(s0)
                so_ref[c, hd] = (s0 * decay + u_s[c, hd]).T
        if carry:
            st_s[hd] = s

    total_decay = jnp.concatenate([bc_row(c, blk - 1, slice(None)) for c in range(n_blk)], axis=0)
    mild = jnp.max(-total_decay) <= DECAY_GUARD

    @pl.when(mild)
    def _direct():
        tril = lmask_ref[len(levels) + 1] > 0.5
        for c in range(n_blk):
            for hd in range(n_hg):
                rows, cols = unit(c, hd)
                k_grow = _bf(kk_s[rows, cols] * jnp.exp(-bc_s[rows, cols]))
                a_s[c, hd] = _bf(jnp.where(tril, _dot(qe_s[rows, cols], k_grow, _NT), 0.0))

    @pl.when(jnp.logical_not(mild))
    def _split():
        for c in range(n_blk):
            for hd in range(n_hg):
                rows, cols = unit(c, hd)
                a = _dot(_bf(hq_s[rows, cols]), _bf(kk_s[rows, cols]), _NT) * lmask_ref[len(levels)]
                for li, m in enumerate(levels):
                    z = level_operand(m, c, hd)
                    a = a + _dot(z, z, _NT) * lmask_ref[li]
                a_s[c, hd] = _bf(a)

    for c in range(n_blk):
        for hd in range(n_hg):
            rows, cols = unit(c, hd)
            o = _dot(qe_s[rows, cols], sb_s[c, hd], _NT) + _dot(a_s[c, hd], hi_s[rows, cols])
            o = o * lax.rsqrt(jnp.mean(o * o, axis=-1, keepdims=True) + RMS_EPS) * gn
            hg_s[rows, cols] = o * og_s[rows, cols]

    merged = _bf(ga_s[...] * att_s[...] + gh_s[...] * hg_s[...])
    y_ref[0] = x + _dot(merged, wout_ref[...])

    if carry:
        @pl.when(t_idx == pl.num_programs(1) - 1)
        def _emit():
            ko_ref[0] = kv_s[tile - WINDOW:tile, 0:KV_WIDTH]
            vo_ref[0] = kv_s[tile - WINDOW:tile, KV_WIDTH:2 * KV_WIDTH]
            for hd in range(n_hg):
                so_ref[0, hd] = st_s[hd].T
    else:
        ko_ref[0] = kv_s[:, 0:KV_WIDTH]
        vo_ref[0] = kv_s[:, KV_WIDTH:2 * KV_WIDTH]


def _full_spec(shape):
    nd = len(shape)
    return pl.BlockSpec(shape, lambda *_: (0,) * nd)


def _weight_spec(shape):
    nd = len(shape)
    return pl.BlockSpec(shape, lambda *_: (0,) * nd, pipeline_mode=pl.Buffered(1))


def _mixer(x, hist, norm_w, w_in, w_out, sinks, table, lb_params, gnorm_w, *, layer, carry, tile, chunk,
           hg_block):
    nb, s, d = x.shape
    n_t = s // tile
    n_chunks = tile // chunk
    n_blk = tile // hg_block
    assert carry or hg_block == chunk
    keys = WINDOW + chunk
    key_pad = 2 * LANES
    n_hg = d // HG_DK
    n_streams = nb if carry else s // chunk
    n_off = chunk + key_pad - 1
    offsets = jnp.arange(n_off, dtype=jnp.int32) - (chunk - 1) - WINDOW
    by_offset = table[_t5_bucket(offsets)].astype(jnp.float32).T
    period = jnp.roll(jnp.pad(by_offset, ((0, 0), (0, 1))), -(chunk - 1), axis=1)
    skewed = jnp.tile(period, (1, chunk))[:, :chunk * n_off].reshape(N_HEADS, chunk, n_off)
    rel_bias = skewed[:, :, :key_pad]
    lmask = jnp.asarray(_level_masks(hg_block))
    hmask = jnp.asarray(_row_masks(hg_block))
    smem = pl.BlockSpec(memory_space=pltpu.SMEM)

    in_arrays = [x, norm_w.reshape(1, d), w_in, w_out, sinks, rel_bias, lb_params,
                 gnorm_w.reshape(1, HG_DK), lmask, hmask]
    in_specs = [pl.BlockSpec((1, tile, d), lambda b, t: (b, t, 0)), _full_spec((1, d)),
                _weight_spec(w_in.shape), _weight_spec(w_out.shape), smem, _full_spec(rel_bias.shape),
                _full_spec(lb_params.shape), _full_spec((1, HG_DK)),
                _full_spec(lmask.shape), _full_spec(hmask.shape)]
    if carry:
        kv_rows = WINDOW
        st_block = (1, n_hg, HG_DK, HG_DK)
        n_win, win_rows, n_var = 1, WINDOW + tile, WINDOW // chunk + 1
    else:
        in_arrays += list(hist)
        in_specs += [_full_spec(a.shape) for a in hist]
        kv_rows = s
        st_block = (n_streams, n_hg, HG_DK, HG_DK)
        n_win, win_rows, n_var = n_chunks, keys, 1
    out_shape = [jax.ShapeDtypeStruct((nb, s, d), jnp.float32),
                 jax.ShapeDtypeStruct((nb, kv_rows, KV_WIDTH), jnp.float32),
                 jax.ShapeDtypeStruct((nb, kv_rows, KV_WIDTH), jnp.float32),
                 jax.ShapeDtypeStruct((n_streams, n_hg, HG_DK, HG_DK), jnp.float32)]
    out_specs = [pl.BlockSpec((1, tile, d), lambda b, t: (b, t, 0)),
                 pl.BlockSpec((1, kv_rows, KV_WIDTH), lambda b, t: (b, 0, 0)),
                 pl.BlockSpec((1, kv_rows, KV_WIDTH), lambda b, t: (b, 0, 0)),
                 pl.BlockSpec(st_block, lambda b, t: (b, 0, 0, 0))]
    f32, bf16 = jnp.float32, jnp.bfloat16
    rows_g = (N_HEADS // 2 // N_KV_HEADS) * chunk
    scratch = [pltpu.VMEM((n_var, 4, rows_g, key_pad), f32),
               pltpu.VMEM((n_chunks, N_HEADS // 2, chunk, LANES), bf16),
               pltpu.VMEM((tile, 2 * KV_WIDTH), f32),
               pltpu.VMEM((4, n_win, win_rows, LANES), bf16),
               pltpu.VMEM((4, n_win, win_rows, LANES), bf16),
               pltpu.VMEM((tile, d), f32), pltpu.VMEM((tile, d), f32),
               pltpu.VMEM((tile, d), bf16)]
    scratch += [pltpu.VMEM((tile, d), f32) for _ in range(7)]
    scratch += [pltpu.VMEM((n_blk, n_hg, HG_DK, HG_DK), f32),
                pltpu.VMEM((n_blk, n_hg, HG_DK, HG_DK), bf16),
                pltpu.VMEM((n_chunks, 4, rows_g, key_pad), bf16),
                pltpu.VMEM((n_blk, n_hg, hg_block, hg_block), bf16),
                pltpu.VMEM((tile, d), bf16)]
    if carry:
        scratch += [pltpu.VMEM((n_hg, HG_DK, HG_DK), f32)]
    kern = functools.partial(_mixer_kernel, tile=tile, chunk=chunk, hg_block=hg_block, carry=carry, d_model=d,
                             layer=layer)
    return pl.pallas_call(
        kern,
        grid=(nb, n_t),
        in_specs=in_specs,
        out_specs=out_specs,
        out_shape=out_shape,
        scratch_shapes=scratch,
        compiler_params=pltpu.CompilerParams(
            dimension_semantics=("arbitrary", "arbitrary"), vmem_limit_bytes=VMEM_LIMIT_BYTES),
        name="mixer_prompt" if carry else "mixer_sample",
    )(*in_arrays)


def _ffn_kernel(xa_ref, xb_ref, nw_ref, wgu_ref, wd_ref, fw_ref, ya_ref, yb_ref, act_s, *, d_ff, col_tile,
                final_norm):
    n_a = pl.num_programs(0) - 1

    def ffn(x_ref, y_ref):
        rows = x_ref.shape[0]
        x = x_ref[...]
        h = _bf(_rms(x, nw_ref[...]))
        for j in range(d_ff // col_tile):
            g = _dot(h, _bf(wgu_ref[:, j * col_tile:(j + 1) * col_tile]))
            u = _dot(h, _bf(wgu_ref[:, d_ff + j * col_tile:d_ff + (j + 1) * col_tile]))
            act_s[0:rows, j * col_tile:(j + 1) * col_tile] = _bf(jax.nn.silu(g) * u)
        y = x + _dot(act_s[0:rows, :], _bf(wd_ref[...]))
        if final_norm:
            y = _rms(y, fw_ref[...])
        y_ref[...] = y

    @pl.when(pl.program_id(0) < n_a)
    def _first_set():
        ffn(xa_ref, ya_ref)

    @pl.when(pl.program_id(0) == n_a)
    def _second_set():
        ffn(xb_ref, yb_ref)


def _ffn(xa, xb, norm_w, w_gate_up, w_down, final_w, *, tile, final_norm):
    n, d = xa.shape
    m = xb.shape[0]
    n_a = n // tile
    assert n == n_a * tile and m <= tile
    d_ff = w_down.shape[0]
    kern = functools.partial(_ffn_kernel, d_ff=d_ff, col_tile=2 * LANES, final_norm=final_norm)
    a_spec = pl.BlockSpec((tile, d), lambda i: (jnp.minimum(i, n_a - 1), 0))
    return pl.pallas_call(
        kern,
        grid=(n_a + 1,),
        in_specs=[a_spec, _full_spec((m, d)), _full_spec((1, d)),
                  _weight_spec(w_gate_up.shape), _weight_spec(w_down.shape), _full_spec((1, d))],
        out_specs=[a_spec, _full_spec((m, d))],
        out_shape=[jax.ShapeDtypeStruct((n, d), jnp.float32), jax.ShapeDtypeStruct((m, d), jnp.float32)],
        scratch_shapes=[pltpu.VMEM((tile, d_ff), jnp.bfloat16)],
        compiler_params=pltpu.CompilerParams(
            dimension_semantics=("arbitrary",), vmem_limit_bytes=VMEM_LIMIT_BYTES),
        name="ffn",
    )(xa, xb, norm_w.reshape(1, d), w_gate_up, w_down, final_w.reshape(1, d))


def kernel(x_prompt, x_sample, cache_k, cache_v, state_hgrn, norm_mix, w_in, w_out, attn_sinks, rel_bias_table,
           hgrn_lb, hgrn_norm, norm_ffn, w_gate_up, w_down, norm_final):
    depth = w_in.shape[0]
    batch, seq, d = x_prompt.shape
    dec_batch, dec_seq, _ = x_sample.shape
    xp = x_prompt
    xs = x_sample.reshape(1, dec_batch * dec_seq, d)
    outs = [[] for _ in range(6)]
    for l in range(depth):
        w_in_l, w_out_l = _bf(w_in[l]), _bf(w_out[l])
        shared = (norm_mix[l], w_in_l, w_out_l, attn_sinks[l], rel_bias_table, hgrn_lb, hgrn_norm[l])
        xp, kp, vp, sp = _mixer(xp, None, *shared, layer=l, carry=True, tile=4 * CHUNK, chunk=CHUNK,
                                hg_block=2 * CHUNK)
        hist = (cache_k[l].reshape(dec_batch, WINDOW, KV_WIDTH), cache_v[l].reshape(dec_batch, WINDOW, KV_WIDTH),
                state_hgrn[l])
        xs, ks, vs, ss = _mixer(xs, hist, *shared, layer=l, carry=False, tile=dec_batch * dec_seq, chunk=dec_seq,
                                hg_block=dec_seq)
        last = l == depth - 1
        xp, xs = _ffn(xp.reshape(batch * seq, d), xs.reshape(dec_batch * dec_seq, d), norm_ffn[l], w_gate_up[l],
                      w_down[l], norm_final, tile=512, final_norm=last)
        xp = xp.reshape(batch, seq, d)
        xs = xs.reshape(1, dec_batch * dec_seq, d)
        ks = jnp.concatenate([hist[0][:, dec_seq:], ks.reshape(dec_batch, dec_seq, KV_WIDTH)], axis=1)
        vs = jnp.concatenate([hist[1][:, dec_seq:], vs.reshape(dec_batch, dec_seq, KV_WIDTH)], axis=1)
        kv_shape = (-1, WINDOW, N_KV_HEADS, HEAD_DIM)
        for acc, val in zip(outs, (kp.reshape(kv_shape), vp.reshape(kv_shape), sp,
                                   ks.reshape(kv_shape), vs.reshape(kv_shape), ss)):
            acc.append(val)
    return (xp, xs.reshape(dec_batch, dec_seq, d)) + tuple(jnp.stack(o) for o in outs)
```

```python
import functools
import math

import numpy as np
import jax
import jax.numpy as jnp
from jax import lax
from jax.experimental import pallas as pl
from jax.experimental.pallas import tpu as pltpu

CHUNK = 64
N_HEADS = 16
N_KV_HEADS = 2
HEAD_DIM = 64
GROUP = N_HEADS // N_KV_HEADS
KV_WIDTH = N_KV_HEADS * HEAD_DIM
WINDOW = 128
ATTN_SCALE = HEAD_DIM ** -0.5
LOG2E = math.log2(math.e)
N_BUCKETS = 32
MAX_DISTANCE = 128
HG_DK = 128
RMS_EPS = 1e-6
DECAY_GUARD = 75.0

LANES = 128
SUBLANES = 8
VMEM_LIMIT_BYTES = 56 * 1024 * 1024

_NT = (((1,), (1,)), ((), ()))
_TN = (((0,), (0,)), ((), ()))


def _bf(x):
    return x.astype(jnp.bfloat16)


def _dot(a, b, dims=None):
    if dims is None:
        return jnp.dot(a, b, preferred_element_type=jnp.float32)
    return lax.dot_general(a, b, dims, preferred_element_type=jnp.float32)


def _rms(x, w):
    return x * lax.rsqrt(jnp.mean(x * x, axis=-1, keepdims=True) + RMS_EPS) * w


def _t5_bucket(rel):
    nb = N_BUCKETS // 2
    max_exact = nb // 2
    ret = jnp.where(rel > 0, nb, 0)
    n = jnp.abs(rel)
    nf = jnp.maximum(n, 1).astype(jnp.float32)
    large = max_exact + (jnp.log(nf / max_exact) / math.log(MAX_DISTANCE / max_exact)
                         * (nb - max_exact)).astype(jnp.int32)
    large = jnp.minimum(large, nb - 1)
    return ret + jnp.where(n < max_exact, n, large)


def _levels(c):
    out, m = [], c // 2
    while m >= 1:
        out.append(m)
        m //= 2
    return out


def _level_masks(c):
    t = np.arange(c)[:, None]
    s = np.arange(c)[None, :]
    masks = []
    for m in _levels(c):
        masks.append((t // (2 * m) == s // (2 * m)) & ((t // m) % 2 == 1) & ((s // m) % 2 == 0))
    masks += [t == s, s <= t]
    return np.stack(masks).astype(np.float32)


def _row_masks(c):
    r = np.arange(c)
    rows = [(r // m) % 2 == 1 for m in _levels(c) if m < 8]
    rows += [r % 4 == 0, r % 4 >= 2, r % 4 == 3]
    return np.repeat(np.stack(rows).astype(np.float32)[:, :, None], LANES, axis=2)


def _block_cumsum(x, block):
    r, n = x.shape
    tiles = x.reshape(r // SUBLANES, SUBLANES, n)
    row = lax.broadcasted_iota(jnp.int32, (1, SUBLANES, n), 1)
    for shift in (1, 2, 4):
        tiles = tiles + jnp.where(row >= shift, pltpu.roll(tiles, shift, axis=1), 0.0)
    out, before = [], None
    for i in range(r // SUBLANES):
        t = tiles[i] if before is None or i % (block // SUBLANES) == 0 else tiles[i] + before
        out.append(t)
        before = jnp.broadcast_to(t[SUBLANES - 1:SUBLANES, :], (SUBLANES, n))
    return jnp.concatenate(out, axis=0)


def _head_variants(a):
    lo = lax.broadcasted_iota(jnp.int32, a.shape, 1) < HEAD_DIM
    rot = pltpu.roll(a, HEAD_DIM, axis=1)
    return [_bf(jnp.where(lo, a, 0.0)), _bf(jnp.where(lo, 0.0, rot)),
            _bf(jnp.where(lo, rot, 0.0)), _bf(jnp.where(lo, 0.0, a))]


def _mixer_kernel(*refs, tile, chunk, hg_block, carry, d_model, layer):
    n_chunks = tile // chunk
    keys = WINDOW + chunk
    key_pad = 2 * LANES
    assert keys < key_pad and tile >= WINDOW
    n_hg = d_model // HG_DK
    att_w = N_HEADS * HEAD_DIM
    n_pairs = N_HEADS // 2
    ppk = n_pairs // N_KV_HEADS
    rows_g = ppk * chunk
    n_var = WINDOW // chunk + 1 if carry else 1
    f32 = jnp.float32

    refs = list(refs)
    (x_ref, nw_ref, win_ref, wout_ref, sinks_ref, relb_ref, lbp_ref, gn_ref,
     lmask_ref, hmask_ref) = refs[:10]
    refs = refs[10:]
    if not carry:
        hk_ref, hv_ref, si_ref = refs[:3]
        refs = refs[3:]
    y_ref, ko_ref, vo_ref, so_ref = refs[:4]
    refs = refs[4:]
    (bias_s, q4_s, kv_s, kx_s, vx_s, hq_s, kk_s, hi_s, lf_s, bc_s, og_s, ga_s, gh_s, att_s, hg_s,
     u_s, sb_s, p_s, a_s, qe_s) = refs[:20]
    if carry:
        st_s = refs[20]

    first = (pl.program_id(0) == 0) & (pl.program_id(1) == 0)
    t_idx = pl.program_id(1)

    @pl.when(first)
    def _build_bias():
        lane = lax.broadcasted_iota(jnp.int32, (chunk, key_pad), 1)
        for head in range(N_HEADS):
            acc = jnp.where(lane == keys, sinks_ref[head], jnp.where(lane > keys, -jnp.inf, relb_ref[head])) * LOG2E
            pair, side = divmod(head, 2)
            kvh, j = divmod(pair, ppk)
            for var in range(n_var):
                n_invalid = WINDOW - var * chunk if carry else 0
                bias_s[var, 2 * kvh + side, j * chunk:(j + 1) * chunk, :] = jnp.where(lane < n_invalid, -jnp.inf, acc)

    if carry:
        @pl.when(t_idx == 0)
        def _reset():
            kx_s[:, 0, 0:WINDOW, :] = jnp.zeros((4, WINDOW, LANES), jnp.bfloat16)
            vx_s[:, 0, 0:WINDOW, :] = jnp.zeros((4, WINDOW, LANES), jnp.bfloat16)
            st_s[...] = jnp.zeros_like(st_s)

        @pl.when(t_idx > 0)
        def _shift():
            kx_s[:, 0, 0:WINDOW, :] = kx_s[:, 0, tile:tile + WINDOW, :]
            vx_s[:, 0, 0:WINDOW, :] = vx_s[:, 0, tile:tile + WINDOW, :]

    x = x_ref[0]
    h = _bf(_rms(x, nw_ref[...]))

    def proj(i0, width):
        return _dot(h, win_ref[:, i0:i0 + width])

    off = 0
    q = _bf(proj(off, att_w) * (ATTN_SCALE * LOG2E))
    for c in range(n_chunks):
        for j in range(n_pairs):
            q4_s[c, j] = q[c * chunk:(c + 1) * chunk, j * LANES:(j + 1) * LANES]
    off += att_w
    kv = proj(off, 2 * KV_WIDTH)
    kv_s[...] = kv
    off += 2 * KV_WIDTH
    k_var = _head_variants(kv[:, 0:KV_WIDTH])
    v_var = _head_variants(kv[:, KV_WIDTH:2 * KV_WIDTH])
    if carry:
        for g in range(4):
            kx_s[g, 0, WINDOW:WINDOW + tile, :] = k_var[g]
            vx_s[g, 0, WINDOW:WINDOW + tile, :] = v_var[g]
    else:
        for c in range(n_chunks):
            hk = _head_variants(hk_ref[c])
            hv = _head_variants(hv_ref[c])
            for g in range(4):
                kx_s[g, c, 0:WINDOW, :] = hk[g]
                vx_s[g, c, 0:WINDOW, :] = hv[g]
                kx_s[g, c, WINDOW:keys, :] = k_var[g][c * chunk:(c + 1) * chunk]
                vx_s[g, c, WINDOW:keys, :] = v_var[g][c * chunk:(c + 1) * chunk]
    hq_s[...] = proj(off, d_model)
    off += d_model
    lbp = lbp_ref[...]
    e = jnp.exp(lbp - jnp.max(lbp, axis=0, keepdims=True))
    lb = jnp.sum(e[:layer + 1], axis=0, keepdims=True) / jnp.sum(e, axis=0, keepdims=True)
    f = lb + (1.0 - lb) * jax.nn.sigmoid(proj(off, d_model))
    off += d_model
    kk_s[...] = 1.0 - f
    lf = jnp.log(f)
    lf_s[...] = lf
    bc_s[...] = _block_cumsum(lf, hg_block)
    hi_s[...] = _bf(proj(off, d_model))
    off += d_model
    og_s[...] = jax.nn.silu(proj(off, d_model))
    off += d_model
    ga_s[...] = jax.nn.sigmoid(proj(off, d_model))
    off += d_model
    gh_s[...] = jax.nn.sigmoid(proj(off, d_model))

    zpad = jnp.zeros((key_pad - keys, LANES), jnp.bfloat16)

    def window(ref, g, c):
        w = ref[g, 0, c * chunk:c * chunk + keys, :] if carry else ref[g, c]
        return jnp.concatenate([w, zpad], axis=0)

    for c in range(n_chunks):
        var = jnp.minimum(t_idx * n_chunks + c, n_var - 1) if carry else 0
        for kvh in range(N_KV_HEADS):
            qg = q4_s[c, kvh * ppk:(kvh + 1) * ppk].reshape(rows_g, LANES)
            for side in range(2):
                g = 2 * kvh + side
                s = _dot(qg, window(kx_s, g, c), _NT) + bias_s[var, g]
                p_s[c, g] = _bf(jnp.exp2(s - jnp.max(s, axis=-1, keepdims=True)))
    key_row = lax.broadcasted_iota(jnp.int32, (key_pad, LANES), 0)
    key_lane = lax.broadcasted_iota(jnp.int32, (key_pad, LANES), 1)
    ones_lo = _bf(jnp.where((key_row <= keys) & (key_lane < HEAD_DIM), 1.0, 0.0))
    ones_hi = _bf(jnp.where((key_row <= keys) & (key_lane >= HEAD_DIM), 1.0, 0.0))
    for c in range(n_chunks):
        for kvh in range(N_KV_HEADS):
            nd = (_dot(p_s[c, 2 * kvh], jnp.concatenate([window(vx_s, 2 * kvh, c), ones_lo], axis=1))
                  + _dot(p_s[c, 2 * kvh + 1], jnp.concatenate([window(vx_s, 2 * kvh + 1, c), ones_hi], axis=1)))
            o = nd[:, 0:LANES] * (1.0 / nd[:, LANES:2 * LANES])
            for j in range(ppk):
                col = (kvh * ppk + j) * LANES
                att_s[c * chunk:(c + 1) * chunk, col:col + LANES] = o[j * chunk:(j + 1) * chunk]

    blk, n_blk = hg_block, tile // hg_block
    levels = _levels(blk)
    gn = gn_ref[...]
    small = [m for m in levels if m < 8]

    def unit(c, hd):
        return slice(c * blk, (c + 1) * blk), slice(hd * HG_DK, (hd + 1) * HG_DK)

    def bc_row(c, r, cols):
        return bc_s[c * blk + r:c * blk + r + 1, cols]

    def level_operand(m, c, hd):
        rows, cols = unit(c, hd)
        q = hq_s[rows, cols]
        kk = kk_s[rows, cols]
        if m >= 8:
            bc = bc_s[rows, cols]
            parts = []
            for base in range(0, blk, 2 * m):
                ref = bc_row(c, base + m - 1, cols)
                lo, hi = slice(base, base + m), slice(base + m, base + 2 * m)
                parts += [kk[lo] * jnp.exp(ref - bc[lo]), q[hi] * jnp.exp(bc[hi] - ref)]
            return _bf(jnp.concatenate(parts, axis=0))
        upper = hmask_ref[small.index(m)] > 0.5
        if m == 1:
            return _bf(jnp.where(upper, q * (1.0 - kk), kk))
        if m == 4:
            bc = bc_s[rows, cols]
            ref_row = bc.reshape(blk // 8, 8, HG_DK)[:, 3:4, :]
            ref = jnp.broadcast_to(ref_row, (blk // 8, 8, HG_DK)).reshape(blk, HG_DK)
            arg = -jnp.abs(bc - ref)
        else:
            lfh = lf_s[rows, cols]
            nxt = pltpu.roll(lfh, blk - 1, axis=0)
            prv = pltpu.roll(lfh, 1, axis=0)
            n_small = len(small)
            arg = hmask_ref[n_small] * nxt + hmask_ref[n_small + 1] * lfh + hmask_ref[n_small + 2] * prv
        return _bf(jnp.where(upper, q, kk) * jnp.exp(arg))

    for c in range(n_blk):
        for hd in range(n_hg):
            rows, cols = unit(c, hd)
            bc = bc_s[rows, cols]
            k_dec = _bf(kk_s[rows, cols] * jnp.exp(bc_row(c, blk - 1, cols) - bc))
            u_s[c, hd] = _dot(hi_s[rows, cols], k_dec, _TN)
            qe_s[rows, cols] = _bf(hq_s[rows, cols] * jnp.exp(bc))

    for hd in range(n_hg):
        if carry:
            s = st_s[hd]
        for c in range(n_blk):
            rows, cols = unit(c, hd)
            decay = jnp.exp(bc_row(c, blk - 1, cols))
            if carry:
                sb_s[c, hd] = _bf(s)
                s = s * decay + u_s[c, hd]
            else:
                s0 = si_ref[c, hd].T
                sb_s[c, hd] = _bf(s0)
                so_ref[c, hd] = (s0 * decay + u_s[c, hd]).T
        if carry:
            st_s[hd] = s

    total_decay = jnp.concatenate([bc_row(c, blk - 1, slice(None)) for c in range(n_blk)], axis=0)
    mild = jnp.max(-total_decay) <= DECAY_GUARD

    @pl.when(mild)
    def _direct():
        tril = lmask_ref[len(levels) + 1] > 0.5
        for c in range(n_blk):
            for hd in range(n_hg):
                rows, cols = unit(c, hd)
                k_grow = _bf(kk_s[rows, cols] * jnp.exp(-bc_s[rows, cols]))
                a_s[c, hd] = _bf(jnp.where(tril, _dot(qe_s[rows, cols], k_grow, _NT), 0.0))

    @pl.when(jnp.logical_not(mild))
    def _split():
        for c in range(n_blk):
            for hd in range(n_hg):
                rows, cols = unit(c, hd)
                a = _dot(_bf(hq_s[rows, cols]), _bf(kk_s[rows, cols]), _NT) * lmask_ref[len(levels)]
                for li, m in enumerate(levels):
                    z = level_operand(m, c, hd)
                    a = a + _dot(z, z, _NT) * lmask_ref[li]
                a_s[c, hd] = _bf(a)

    for c in range(n_blk):
        for hd in range(n_hg):
            rows, cols = unit(c, hd)
            o = _dot(qe_s[rows, cols], sb_s[c, hd], _NT) + _dot(a_s[c, hd], hi_s[rows, cols])
            o = o * lax.rsqrt(jnp.mean(o * o, axis=-1, keepdims=True) + RMS_EPS) * gn
            hg_s[rows, cols] = o * og_s[rows, cols]

    merged = _bf(ga_s[...] * att_s[...] + gh_s[...] * hg_s[...])
    y_ref[0] = x + _dot(merged, wout_ref[...])

    if carry:
        @pl.when(t_idx == pl.num_programs(1) - 1)
        def _emit():
            ko_ref[0] = kv_s[tile - WINDOW:tile, 0:KV_WIDTH]
            vo_ref[0] = kv_s[tile - WINDOW:tile, KV_WIDTH:2 * KV_WIDTH]
            for hd in range(n_hg):
                so_ref[0, hd] = st_s[hd].T
    else:
        ko_ref[0] = kv_s[:, 0:KV_WIDTH]
        vo_ref[0] = kv_s[:, KV_WIDTH:2 * KV_WIDTH]


def _full_spec(shape):
    nd = len(shape)
    return pl.BlockSpec(shape, lambda *_: (0,) * nd)


def _weight_spec(shape):
    nd = len(shape)
    return pl.BlockSpec(shape, lambda *_: (0,) * nd, pipeline_mode=pl.Buffered(1))


def _mixer(x, hist, norm_w, w_in, w_out, sinks, table, lb_params, gnorm_w, *, layer, carry, tile, chunk,
           hg_block):
    nb, s, d = x.shape
    n_t = s // tile
    n_chunks = tile // chunk
    n_blk = tile // hg_block
    assert carry or hg_block == chunk
    keys = WINDOW + chunk
    key_pad = 2 * LANES
    n_hg = d // HG_DK
    n_streams = nb if carry else s // chunk
    n_off = chunk + key_pad - 1
    offsets = jnp.arange(n_off, dtype=jnp.int32) - (chunk - 1) - WINDOW
    by_offset = table[_t5_bucket(offsets)].astype(jnp.float32).T
    period = jnp.roll(jnp.pad(by_offset, ((0, 0), (0, 1))), -(chunk - 1), axis=1)
    skewed = jnp.tile(period, (1, chunk))[:, :chunk * n_off].reshape(N_HEADS, chunk, n_off)
    rel_bias = skewed[:, :, :key_pad]
    lmask = jnp.asarray(_level_masks(hg_block))
    hmask = jnp.asarray(_row_masks(hg_block))
    smem = pl.BlockSpec(memory_space=pltpu.SMEM)

    in_arrays = [x, norm_w.reshape(1, d), w_in, w_out, sinks, rel_bias, lb_params,
                 gnorm_w.reshape(1, HG_DK), lmask, hmask]
    in_specs = [pl.BlockSpec((1, tile, d), lambda b, t: (b, t, 0)), _full_spec((1, d)),
                _weight_spec(w_in.shape), _weight_spec(w_out.shape), smem, _full_spec(rel_bias.shape),
                _full_spec(lb_params.shape), _full_spec((1, HG_DK)),
                _full_spec(lmask.shape), _full_spec(hmask.shape)]
    if carry:
        kv_rows = WINDOW
        st_block = (1, n_hg, HG_DK, HG_DK)
        n_win, win_rows, n_var = 1, WINDOW + tile, WINDOW // chunk + 1
    else:
        in_arrays += list(hist)
        in_specs += [_full_spec(a.shape) for a in hist]
        kv_rows = s
        st_block = (n_streams, n_hg, HG_DK, HG_DK)
        n_win, win_rows, n_var = n_chunks, keys, 1
    out_shape = [jax.ShapeDtypeStruct((nb, s, d), jnp.float32),
                 jax.ShapeDtypeStruct((nb, kv_rows, KV_WIDTH), jnp.float32),
                 jax.ShapeDtypeStruct((nb, kv_rows, KV_WIDTH), jnp.float32),
                 jax.ShapeDtypeStruct((n_streams, n_hg, HG_DK, HG_DK), jnp.float32)]
    out_specs = [pl.BlockSpec((1, tile, d), lambda b, t: (b, t, 0)),
                 pl.BlockSpec((1, kv_rows, KV_WIDTH), lambda b, t: (b, 0, 0)),
                 pl.BlockSpec((1, kv_rows, KV_WIDTH), lambda b, t: (b, 0, 0)),
                 pl.BlockSpec(st_block, lambda b, t: (b, 0, 0, 0))]
    f32, bf16 = jnp.float32, jnp.bfloat16
    rows_g = (N_HEADS // 2 // N_KV_HEADS) * chunk
    scratch = [pltpu.VMEM((n_var, 4, rows_g, key_pad), f32),
               pltpu.VMEM((n_chunks, N_HEADS // 2, chunk, LANES), bf16),
               pltpu.VMEM((tile, 2 * KV_WIDTH), f32),
               pltpu.VMEM((4, n_win, win_rows, LANES), bf16),
               pltpu.VMEM((4, n_win, win_rows, LANES), bf16),
               pltpu.VMEM((tile, d), f32), pltpu.VMEM((tile, d), f32),
               pltpu.VMEM((tile, d), bf16)]
    scratch += [pltpu.VMEM((tile, d), f32) for _ in range(7)]
    scratch += [pltpu.VMEM((n_blk, n_hg, HG_DK, HG_DK), f32),
                pltpu.VMEM((n_blk, n_hg, HG_DK, HG_DK), bf16),
                pltpu.VMEM((n_chunks, 4, rows_g, key_pad), bf16),
                pltpu.VMEM((n_blk, n_hg, hg_block, hg_block), bf16),
                pltpu.VMEM((tile, d), bf16)]
    if carry:
        scratch += [pltpu.VMEM((n_hg, HG_DK, HG_DK), f32)]
    kern = functools.partial(_mixer_kernel, tile=tile, chunk=chunk, hg_block=hg_block, carry=carry, d_model=d,
                             layer=layer)
    return pl.pallas_call(
        kern,
        grid=(nb, n_t),
        in_specs=in_specs,
        out_specs=out_specs,
        out_shape=out_shape,
        scratch_shapes=scratch,
        compiler_params=pltpu.CompilerParams(
            dimension_semantics=("arbitrary", "arbitrary"), vmem_limit_bytes=VMEM_LIMIT_BYTES),
        name="mixer_prompt" if carry else "mixer_sample",
    )(*in_arrays)


def _ffn_kernel(xa_ref, xb_ref, nw_ref, wgu_ref, wd_ref, fw_ref, ya_ref, yb_ref, act_s, *, d_ff, col_tile,
                final_norm):
    n_a = pl.num_programs(0) - 1

    def ffn(x_ref, y_ref):
        rows = x_ref.shape[0]
        x = x_ref[...]
        h = _bf(_rms(x, nw_ref[...]))
        for j in range(d_ff // col_tile):
            g = _dot(h, _bf(wgu_ref[:, j * col_tile:(j + 1) * col_tile]))
            u = _dot(h, _bf(wgu_ref[:, d_ff + j * col_tile:d_ff + (j + 1) * col_tile]))
            act_s[0:rows, j * col_tile:(j + 1) * col_tile] = _bf(jax.nn.silu(g) * u)
        y = x + _dot(act_s[0:rows, :], _bf(wd_ref[...]))
        if final_norm:
            y = _rms(y, fw_ref[...])
        y_ref[...] = y

    @pl.when(pl.program_id(0) < n_a)
    def _first_set():
        ffn(xa_ref, ya_ref)

    @pl.when(pl.program_id(0) == n_a)
    def _second_set():
        ffn(xb_ref, yb_ref)


def _ffn(xa, xb, norm_w, w_gate_up, w_down, final_w, *, tile, final_norm):
    n, d = xa.shape
    m = xb.shape[0]
    n_a = n // tile
    assert n == n_a * tile and m <= tile
    d_ff = w_down.shape[0]
    kern = functools.partial(_ffn_kernel, d_ff=d_ff, col_tile=2 * LANES, final_norm=final_norm)
    a_spec = pl.BlockSpec((tile, d), lambda i: (jnp.minimum(i, n_a - 1), 0))
    return pl.pallas_call(
        kern,
        grid=(n_a + 1,),
        in_specs=[a_spec, _full_spec((m, d)), _full_spec((1, d)),
                  _weight_spec(w_gate_up.shape), _weight_spec(w_down.shape), _full_spec((1, d))],
        out_specs=[a_spec, _full_spec((m, d))],
        out_shape=[jax.ShapeDtypeStruct((n, d), jnp.float32), jax.ShapeDtypeStruct((m, d), jnp.float32)],
        scratch_shapes=[pltpu.VMEM((tile, d_ff), jnp.bfloat16)],
        compiler_params=pltpu.CompilerParams(
            dimension_semantics=("arbitrary",), vmem_limit_bytes=VMEM_LIMIT_BYTES),
        name="ffn",
    )(xa, xb, norm_w.reshape(1, d), w_gate_up, w_down, final_w.reshape(1, d))


def kernel(x_prompt, x_sample, cache_k, cache_v, state_hgrn, norm_mix, w_in, w_out, attn_sinks, rel_bias_table,
           hgrn_lb, hgrn_norm, norm_ffn, w_gate_up, w_down, norm_final):
    depth = w_in.shape[0]
    batch, seq, d = x_prompt.shape
    dec_batch, dec_seq, _ = x_sample.shape
    xp = x_prompt
    xs = x_sample.reshape(1, dec_batch * dec_seq, d)
    outs = [[] for _ in range(6)]
    for l in range(depth):
        w_in_l, w_out_l = _bf(w_in[l]), _bf(w_out[l])
        shared = (norm_mix[l], w_in_l, w_out_l, attn_sinks[l], rel_bias_table, hgrn_lb, hgrn_norm[l])
        xp, kp, vp, sp = _mixer(xp, None, *shared, layer=l, carry=True, tile=4 * CHUNK, chunk=CHUNK,
                                hg_block=2 * CHUNK)
        hist = (cache_k[l].reshape(dec_batch, WINDOW, KV_WIDTH), cache_v[l].reshape(dec_batch, WINDOW, KV_WIDTH),
                state_hgrn[l])
        xs, ks, vs, ss = _mixer(xs, hist, *shared, layer=l, carry=False, tile=dec_batch * dec_seq, chunk=dec_seq,
                                hg_block=dec_seq)
        last = l == depth - 1
        xp, xs = _ffn(xp.reshape(batch * seq, d), xs.reshape(dec_batch * dec_seq, d), norm_ffn[l], w_gate_up[l],
                      w_down[l], norm_final, tile=512, final_norm=last)
        xp = xp.reshape(batch, seq, d)
        xs = xs.reshape(1, dec_batch * dec_seq, d)
        ks = jnp.concatenate([hist[0][:, dec_seq:], ks.reshape(dec_batch, dec_seq, KV_WIDTH)], axis=1)
        vs = jnp.concatenate([hist[1][:, dec_seq:], vs.reshape(dec_batch, dec_seq, KV_WIDTH)], axis=1)
        kv_shape = (-1, WINDOW, N_KV_HEADS, HEAD_DIM)
        for acc, val in zip(outs, (kp.reshape(kv_shape), vp.reshape(kv_shape), sp,
                                   ks.reshape(kv_shape), vs.reshape(kv_shape), ss)):
            acc.append(val)
    return (xp, xs.reshape(dec_batch, dec_seq, d)) + tuple(jnp.stack(o) for o in outs)
```

```python
import functools
import math

import numpy as np
import jax
import jax.numpy as jnp
from jax import lax
from jax.experimental import pallas as pl
from jax.experimental.pallas import tpu as pltpu

CHUNK = 64
N_HEADS = 16
N_KV_HEADS = 2
HEAD_DIM = 64
GROUP = N_HEADS // N_KV_HEADS
KV_WIDTH = N_KV_HEADS * HEAD_DIM
WINDOW = 128
ATTN_SCALE = HEAD_DIM ** -0.5
LOG2E = math.log2(math.e)
N_BUCKETS = 32
MAX_DISTANCE = 128
HG_DK = 128
RMS_EPS = 1e-6
DECAY_GUARD = 75.0

LANES = 128
SUBLANES = 8
VMEM_LIMIT_BYTES = 56 * 1024 * 1024

_NT = (((1,), (1,)), ((), ()))
_TN = (((0,), (0,)), ((), ()))


def _bf(x):
    return x.astype(jnp.bfloat16)


def _dot(a, b, dims=None):
    if dims is None:
        return jnp.dot(a, b, preferred_element_type=jnp.float32)
    return lax.dot_general(a, b, dims, preferred_element_type=jnp.float32)


def _rms(x, w):
    return x * lax.rsqrt(jnp.mean(x * x, axis=-1, keepdims=True) + RMS_EPS) * w


def _t5_bucket(rel):
    nb = N_BUCKETS // 2
    max_exact = nb // 2
    ret = jnp.where(rel > 0, nb, 0)
    n = jnp.abs(rel)
    nf = jnp.maximum(n, 1).astype(jnp.float32)
    large = max_exact + (jnp.log(nf / max_exact) / math.log(MAX_DISTANCE / max_exact)
                         * (nb - max_exact)).astype(jnp.int32)
    large = jnp.minimum(large, nb - 1)
    return ret + jnp.where(n < max_exact, n, large)


def _levels(c):
    out, m = [], c // 2
    while m >= 1:
        out.append(m)
        m //= 2
    return out


def _level_masks(c):
    t = np.arange(c)[:, None]
    s = np.arange(c)[None, :]
    masks = []
    for m in _levels(c):
        masks.append((t // (2 * m) == s // (2 * m)) & ((t // m) % 2 == 1) & ((s // m) % 2 == 0))
    masks += [t == s, s <= t]
    return np.stack(masks).astype(np.float32)


def _row_masks(c):
    r = np.arange(c)
    rows = [(r // m) % 2 == 1 for m in _levels(c) if m < 8]
    rows += [r % 4 == 0, r % 4 >= 2, r % 4 == 3]
    return np.repeat(np.stack(rows).astype(np.float32)[:, :, None], LANES, axis=2)


def _block_cumsum(x, block):
    r, n = x.shape
    tiles = x.reshape(r // SUBLANES, SUBLANES, n)
    row = lax.broadcasted_iota(jnp.int32, (1, SUBLANES, n), 1)
    for shift in (1, 2, 4):
        tiles = tiles + jnp.where(row >= shift, pltpu.roll(tiles, shift, axis=1), 0.0)
    out, before = [], None
    for i in range(r // SUBLANES):
        t = tiles[i] if before is None or i % (block // SUBLANES) == 0 else tiles[i] + before
        out.append(t)
        before = jnp.broadcast_to(t[SUBLANES - 1:SUBLANES, :], (SUBLANES, n))
    return jnp.concatenate(out, axis=0)


def _head_variants(a):
    lo = lax.broadcasted_iota(jnp.int32, a.shape, 1) < HEAD_DIM
    rot = pltpu.roll(a, HEAD_DIM, axis=1)
    return [_bf(jnp.where(lo, a, 0.0)), _bf(jnp.where(lo, 0.0, rot)),
            _bf(jnp.where(lo, rot, 0.0)), _bf(jnp.where(lo, 0.0, a))]


def _mixer_kernel(*refs, tile, chunk, hg_block, carry, d_model, layer):
    n_chunks = tile // chunk
    keys = WINDOW + chunk
    key_pad = 2 * LANES
    assert keys < key_pad and tile >= WINDOW
    n_hg = d_model // HG_DK
    att_w = N_HEADS * HEAD_DIM
    n_pairs = N_HEADS // 2
    ppk = n_pairs // N_KV_HEADS
    rows_g = ppk * chunk
    n_var = WINDOW // chunk + 1 if carry else 1
    f32 = jnp.float32

    refs = list(refs)
    (x_ref, nw_ref, win_ref, wout_ref, sinks_ref, relb_ref, lbp_ref, gn_ref,
     lmask_ref, hmask_ref) = refs[:10]
    refs = refs[10:]
    if not carry:
        hk_ref, hv_ref, si_ref = refs[:3]
        refs = refs[3:]
    y_ref, ko_ref, vo_ref, so_ref = refs[:4]
    refs = refs[4:]
    (bias_s, q4_s, kv_s, kx_s, vx_s, hq_s, kk_s, hi_s, lf_s, bc_s, og_s, ga_s, gh_s, att_s, hg_s,
     u_s, sb_s, p_s, a_s, qe_s) = refs[:20]
    if carry:
        st_s = refs[20]

    first = (pl.program_id(0) == 0) & (pl.program_id(1) == 0)
    t_idx = pl.program_id(1)

    @pl.when(first)
    def _build_bias():
        lane = lax.broadcasted_iota(jnp.int32, (chunk, key_pad), 1)
        for head in range(N_HEADS):
            acc = jnp.where(lane == keys, sinks_ref[head], jnp.where(lane > keys, -jnp.inf, relb_ref[head])) * LOG2E
            pair, side = divmod(head, 2)
            kvh, j = divmod(pair, ppk)
            for var in range(n_var):
                n_invalid = WINDOW - var * chunk if carry else 0
                bias_s[var, 2 * kvh + side, j * chunk:(j + 1) * chunk, :] = jnp.where(lane < n_invalid, -jnp.inf, acc)

    if carry:
        @pl.when(t_idx == 0)
        def _reset():
            kx_s[:, 0, 0:WINDOW, :] = jnp.zeros((4, WINDOW, LANES), jnp.bfloat16)
            vx_s[:, 0, 0:WINDOW, :] = jnp.zeros((4, WINDOW, LANES), jnp.bfloat16)
            st_s[...] = jnp.zeros_like(st_s)

        @pl.when(t_idx > 0)
        def _shift():
            kx_s[:, 0, 0:WINDOW, :] = kx_s[:, 0, tile:tile + WINDOW, :]
            vx_s[:, 0, 0:WINDOW, :] = vx_s[:, 0, tile:tile + WINDOW, :]

    x = x_ref[0]
    h = _bf(_rms(x, nw_ref[...]))

    def proj(i0, width):
        return _dot(h, win_ref[:, i0:i0 + width])

    off = 0
    q = _bf(proj(off, att_w) * (ATTN_SCALE * LOG2E))
    for c in range(n_chunks):
        for j in range(n_pairs):
            q4_s[c, j] = q[c * chunk:(c + 1) * chunk, j * LANES:(j + 1) * LANES]
    off += att_w
    kv = proj(off, 2 * KV_WIDTH)
    kv_s[...] = kv
    off += 2 * KV_WIDTH
    k_var = _head_variants(kv[:, 0:KV_WIDTH])
    v_var = _head_variants(kv[:, KV_WIDTH:2 * KV_WIDTH])
    if carry:
        for g in range(4):
            kx_s[g, 0, WINDOW:WINDOW + tile, :] = k_var[g]
            vx_s[g, 0, WINDOW:WINDOW + tile, :] = v_var[g]
    else:
        for c in range(n_chunks):
            hk = _head_variants(hk_ref[c])
            hv = _head_variants(hv_ref[c])
            for g in range(4):
                kx_s[g, c, 0:WINDOW, :] = hk[g]
                vx_s[g, c, 0:WINDOW, :] = hv[g]
                kx_s[g, c, WINDOW:keys, :] = k_var[g][c * chunk:(c + 1) * chunk]
                vx_s[g, c, WINDOW:keys, :] = v_var[g][c * chunk:(c + 1) * chunk]
    off_q, off_f, off_i, off_og, off_ga, off_gh = (off + i * d_model for i in range(6))
    lbp = lbp_ref[...]
    e = jnp.exp(lbp - jnp.max(lbp, axis=0, keepdims=True))
    lb = jnp.sum(e[:layer + 1], axis=0, keepdims=True) / jnp.sum(e, axis=0, keepdims=True)

    def gate_quarter(i):
        cs = slice(i * d_model // 4, (i + 1) * d_model // 4)
        f = lb[:, cs] + (1.0 - lb[:, cs]) * jax.nn.sigmoid(proj(off_f + cs.start, d_model // 4))
        kk_s[:, cs] = 1.0 - f
        lf = jnp.log(f)
        lf_s[:, cs] = lf
        bc_s[:, cs] = _block_cumsum(lf, hg_block)

    gate_quarter(0)
    hq_s[...] = proj(off_q, d_model)
    gate_quarter(1)
    hi_s[...] = _bf(proj(off_i, d_model))
    gate_quarter(2)
    og_s[...] = jax.nn.silu(proj(off_og, d_model))
    gate_quarter(3)
    ga_s[...] = jax.nn.sigmoid(proj(off_ga, d_model))
    gh_s[...] = jax.nn.sigmoid(proj(off_gh, d_model))

    zpad = jnp.zeros((key_pad - keys, LANES), jnp.bfloat16)

    def window(ref, g, c):
        w = ref[g, 0, c * chunk:c * chunk + keys, :] if carry else ref[g, c]
        return jnp.concatenate([w, zpad], axis=0)

    for c in range(n_chunks):
        var = jnp.minimum(t_idx * n_chunks + c, n_var - 1) if carry else 0
        for kvh in range(N_KV_HEADS):
            qg = q4_s[c, kvh * ppk:(kvh + 1) * ppk].reshape(rows_g, LANES)
            for side in range(2):
                g = 2 * kvh + side
                s = _dot(qg, window(kx_s, g, c), _NT) + bias_s[var, g]
                p_s[c, g] = _bf(jnp.exp2(s - jnp.max(s, axis=-1, keepdims=True)))
    key_row = lax.broadcasted_iota(jnp.int32, (key_pad, LANES), 0)
    key_lane = lax.broadcasted_iota(jnp.int32, (key_pad, LANES), 1)
    ones_lo = _bf(jnp.where((key_row <= keys) & (key_lane < HEAD_DIM), 1.0, 0.0))
    ones_hi = _bf(jnp.where((key_row <= keys) & (key_lane >= HEAD_DIM), 1.0, 0.0))
    for c in range(n_chunks):
        for kvh in range(N_KV_HEADS):
            nd = (_dot(p_s[c, 2 * kvh], jnp.concatenate([window(vx_s, 2 * kvh, c), ones_lo], axis=1))
                  + _dot(p_s[c, 2 * kvh + 1], jnp.concatenate([window(vx_s, 2 * kvh + 1, c), ones_hi], axis=1)))
            o = nd[:, 0:LANES] * (1.0 / nd[:, LANES:2 * LANES])
            for j in range(ppk):
                col = (kvh * ppk + j) * LANES
                att_s[c * chunk:(c + 1) * chunk, col:col + LANES] = o[j * chunk:(j + 1) * chunk]

    blk, n_blk = hg_block, tile // hg_block
    levels = _levels(blk)
    gn = gn_ref[...]
    small = [m for m in levels if m < 8]

    def unit(c, hd):
        return slice(c * blk, (c + 1) * blk), slice(hd * HG_DK, (hd + 1) * HG_DK)

    def bc_row(c, r, cols):
        return bc_s[c * blk + r:c * blk + r + 1, cols]

    def level_operand(m, c, hd):
        rows, cols = unit(c, hd)
        q = hq_s[rows, cols]
        kk = kk_s[rows, cols]
        if m >= 8:
            bc = bc_s[rows, cols]
            parts = []
            for base in range(0, blk, 2 * m):
                ref = bc_row(c, base + m - 1, cols)
                lo, hi = slice(base, base + m), slice(base + m, base + 2 * m)
                parts += [kk[lo] * jnp.exp(ref - bc[lo]), q[hi] * jnp.exp(bc[hi] - ref)]
            return _bf(jnp.concatenate(parts, axis=0))
        upper = hmask_ref[small.index(m)] > 0.5
        if m == 1:
            return _bf(jnp.where(upper, q * (1.0 - kk), kk))
        if m == 4:
            bc = bc_s[rows, cols]
            ref_row = bc.reshape(blk // 8, 8, HG_DK)[:, 3:4, :]
            ref = jnp.broadcast_to(ref_row, (blk // 8, 8, HG_DK)).reshape(blk, HG_DK)
            arg = -jnp.abs(bc - ref)
        else:
            lfh = lf_s[rows, cols]
            nxt = pltpu.roll(lfh, blk - 1, axis=0)
            prv = pltpu.roll(lfh, 1, axis=0)
            n_small = len(small)
            arg = hmask_ref[n_small] * nxt + hmask_ref[n_small + 1] * lfh + hmask_ref[n_small + 2] * prv
        return _bf(jnp.where(upper, q, kk) * jnp.exp(arg))

    for c in range(n_blk):
        for hd in range(n_hg):
            rows, cols = unit(c, hd)
            bc = bc_s[rows, cols]
            k_dec = _bf(kk_s[rows, cols] * jnp.exp(bc_row(c, blk - 1, cols) - bc))
            u_s[c, hd] = _dot(hi_s[rows, cols], k_dec, _TN)
            qe_s[rows, cols] = _bf(hq_s[rows, cols] * jnp.exp(bc))

    for hd in range(n_hg):
        if carry:
            s = st_s[hd]
        for c in range(n_blk):
            rows, cols = unit(c, hd)
            decay = jnp.exp(bc_row(c, blk - 1, cols))
            if carry:
                sb_s[c, hd] = _bf(s.T)
                s = s * decay + u_s[c, hd]
            else:
                s0 = si_ref[c, hd]
                sb_s[c, hd] = _bf(s0)
                so_ref[c, hd] = (s0.T * decay + u_s[c, hd]).T
        if carry:
            st_s[hd] = s

    total_decay = jnp.concatenate([bc_row(c, blk - 1, slice(None)) for c in range(n_blk)], axis=0)
    mild = jnp.max(-total_decay) <= DECAY_GUARD

    @pl.when(mild)
    def _direct():
        tril = lmask_ref[len(levels) + 1] > 0.5
        for c in range(n_blk):
            for hd in range(n_hg):
                rows, cols = unit(c, hd)
                k_grow_t = _bf((kk_s[rows, cols] * jnp.exp(-bc_s[rows, cols])).T)
                a_s[c, hd] = _bf(jnp.where(tril, _dot(qe_s[rows, cols], k_grow_t), 0.0))

    @pl.when(jnp.logical_not(mild))
    def _split():
        for c in range(n_blk):
            for hd in range(n_hg):
                rows, cols = unit(c, hd)
                a = _dot(_bf(hq_s[rows, cols]), _bf(kk_s[rows, cols]), _NT) * lmask_ref[len(levels)]
                for li, m in enumerate(levels):
                    z = level_operand(m, c, hd)
                    a = a + _dot(z, z, _NT) * lmask_ref[li]
                a_s[c, hd] = _bf(a)

    for c in range(n_blk):
        for hd in range(n_hg):
            rows, cols = unit(c, hd)
            o = _dot(qe_s[rows, cols], sb_s[c, hd]) + _dot(a_s[c, hd], hi_s[rows, cols])
            o = o * lax.rsqrt(jnp.mean(o * o, axis=-1, keepdims=True) + RMS_EPS) * gn
            hg_s[rows, cols] = o * og_s[rows, cols]

    merged = _bf(ga_s[...] * att_s[...] + gh_s[...] * hg_s[...])
    y_ref[0] = x + _dot(merged, wout_ref[...])

    if carry:
        @pl.when(t_idx == pl.num_programs(1) - 1)
        def _emit():
            ko_ref[0] = kv_s[tile - WINDOW:tile, 0:KV_WIDTH]
            vo_ref[0] = kv_s[tile - WINDOW:tile, KV_WIDTH:2 * KV_WIDTH]
            for hd in range(n_hg):
                so_ref[0, hd] = st_s[hd].T
    else:
        ko_ref[0] = kv_s[:, 0:KV_WIDTH]
        vo_ref[0] = kv_s[:, KV_WIDTH:2 * KV_WIDTH]


def _full_spec(shape):
    nd = len(shape)
    return pl.BlockSpec(shape, lambda *_: (0,) * nd)


def _weight_spec(shape):
    nd = len(shape)
    return pl.BlockSpec(shape, lambda *_: (0,) * nd, pipeline_mode=pl.Buffered(1))


def _mixer(x, hist, norm_w, w_in, w_out, sinks, table, lb_params, gnorm_w, *, layer, carry, tile, chunk,
           hg_block):
    nb, s, d = x.shape
    n_t = s // tile
    n_chunks = tile // chunk
    n_blk = tile // hg_block
    assert carry or hg_block == chunk
    keys = WINDOW + chunk
    key_pad = 2 * LANES
    n_hg = d // HG_DK
    n_streams = nb if carry else s // chunk
    n_off = chunk + key_pad - 1
    offsets = jnp.arange(n_off, dtype=jnp.int32) - (chunk - 1) - WINDOW
    by_offset = table[_t5_bucket(offsets)].astype(jnp.float32).T
    period = jnp.roll(jnp.pad(by_offset, ((0, 0), (0, 1))), -(chunk - 1), axis=1)
    skewed = jnp.tile(period, (1, chunk))[:, :chunk * n_off].reshape(N_HEADS, chunk, n_off)
    rel_bias = skewed[:, :, :key_pad]
    lmask = jnp.asarray(_level_masks(hg_block))
    hmask = jnp.asarray(_row_masks(hg_block))
    smem = pl.BlockSpec(memory_space=pltpu.SMEM)

    in_arrays = [x, norm_w.reshape(1, d), w_in, w_out, sinks, rel_bias, lb_params,
                 gnorm_w.reshape(1, HG_DK), lmask, hmask]
    in_specs = [pl.BlockSpec((1, tile, d), lambda b, t: (b, t, 0)), _full_spec((1, d)),
                _weight_spec(w_in.shape), _weight_spec(w_out.shape), smem, _full_spec(rel_bias.shape),
                _full_spec(lb_params.shape), _full_spec((1, HG_DK)),
                _full_spec(lmask.shape), _full_spec(hmask.shape)]
    if carry:
        kv_rows = WINDOW
        st_block = (1, n_hg, HG_DK, HG_DK)
        n_win, win_rows, n_var = 1, WINDOW + tile, WINDOW // chunk + 1
    else:
        in_arrays += list(hist)
        in_specs += [_full_spec(a.shape) for a in hist]
        kv_rows = s
        st_block = (n_streams, n_hg, HG_DK, HG_DK)
        n_win, win_rows, n_var = n_chunks, keys, 1
    out_shape = [jax.ShapeDtypeStruct((nb, s, d), jnp.float32),
                 jax.ShapeDtypeStruct((nb, kv_rows, KV_WIDTH), jnp.float32),
                 jax.ShapeDtypeStruct((nb, kv_rows, KV_WIDTH), jnp.float32),
                 jax.ShapeDtypeStruct((n_streams, n_hg, HG_DK, HG_DK), jnp.float32)]
    out_specs = [pl.BlockSpec((1, tile, d), lambda b, t: (b, t, 0)),
                 pl.BlockSpec((1, kv_rows, KV_WIDTH), lambda b, t: (b, 0, 0)),
                 pl.BlockSpec((1, kv_rows, KV_WIDTH), lambda b, t: (b, 0, 0)),
                 pl.BlockSpec(st_block, lambda b, t: (b, 0, 0, 0))]
    f32, bf16 = jnp.float32, jnp.bfloat16
    rows_g = (N_HEADS // 2 // N_KV_HEADS) * chunk
    scratch = [pltpu.VMEM((n_var, 4, rows_g, key_pad), f32),
               pltpu.VMEM((n_chunks, N_HEADS // 2, chunk, LANES), bf16),
               pltpu.VMEM((tile, 2 * KV_WIDTH), f32),
               pltpu.VMEM((4, n_win, win_rows, LANES), bf16),
               pltpu.VMEM((4, n_win, win_rows, LANES), bf16),
               pltpu.VMEM((tile, d), f32), pltpu.VMEM((tile, d), f32),
               pltpu.VMEM((tile, d), bf16)]
    scratch += [pltpu.VMEM((tile, d), f32) for _ in range(7)]
    scratch += [pltpu.VMEM((n_blk, n_hg, HG_DK, HG_DK), f32),
                pltpu.VMEM((n_blk, n_hg, HG_DK, HG_DK), bf16),
                pltpu.VMEM((n_chunks, 4, rows_g, key_pad), bf16),
                pltpu.VMEM((n_blk, n_hg, hg_block, hg_block), bf16),
                pltpu.VMEM((tile, d), bf16)]
    if carry:
        scratch += [pltpu.VMEM((n_hg, HG_DK, HG_DK), f32)]
    kern = functools.partial(_mixer_kernel, tile=tile, chunk=chunk, hg_block=hg_block, carry=carry, d_model=d,
                             layer=layer)
    return pl.pallas_call(
        kern,
        grid=(nb, n_t),
        in_specs=in_specs,
        out_specs=out_specs,
        out_shape=out_shape,
        scratch_shapes=scratch,
        compiler_params=pltpu.CompilerParams(
            dimension_semantics=("arbitrary", "arbitrary"), vmem_limit_bytes=VMEM_LIMIT_BYTES),
        name="mixer_prompt" if carry else "mixer_sample",
    )(*in_arrays)


def _ffn_kernel(xa_ref, xb_ref, nw_ref, wgu_ref, wd_ref, fw_ref, ya_ref, yb_ref, act_s, *, d_ff, col_tile,
                final_norm):
    n_a = pl.num_programs(0) - 1

    def ffn(x_ref, y_ref):
        rows = x_ref.shape[0]
        x = x_ref[...]
        h = _bf(_rms(x, nw_ref[...]))
        for j in range(d_ff // col_tile):
            g = _dot(h, _bf(wgu_ref[:, j * col_tile:(j + 1) * col_tile]))
            u = _dot(h, _bf(wgu_ref[:, d_ff + j * col_tile:d_ff + (j + 1) * col_tile]))
            act_s[0:rows, j * col_tile:(j + 1) * col_tile] = _bf(jax.nn.silu(g) * u)
        y = x + _dot(act_s[0:rows, :], _bf(wd_ref[...]))
        if final_norm:
            y = _rms(y, fw_ref[...])
        y_ref[...] = y

    @pl.when(pl.program_id(0) < n_a)
    def _first_set():
        ffn(xa_ref, ya_ref)

    @pl.when(pl.program_id(0) == n_a)
    def _second_set():
        ffn(xb_ref, yb_ref)


def _ffn(xa, xb, norm_w, w_gate_up, w_down, final_w, *, tile, final_norm):
    n, d = xa.shape
    m = xb.shape[0]
    n_a = n // tile
    assert n == n_a * tile and m <= tile
    d_ff = w_down.shape[0]
    kern = functools.partial(_ffn_kernel, d_ff=d_ff, col_tile=2 * LANES, final_norm=final_norm)
    a_spec = pl.BlockSpec((tile, d), lambda i: (jnp.minimum(i, n_a - 1), 0))
    return pl.pallas_call(
        kern,
        grid=(n_a + 1,),
        in_specs=[a_spec, _full_spec((m, d)), _full_spec((1, d)),
                  _weight_spec(w_gate_up.shape), _weight_spec(w_down.shape), _full_spec((1, d))],
        out_specs=[a_spec, _full_spec((m, d))],
        out_shape=[jax.ShapeDtypeStruct((n, d), jnp.float32), jax.ShapeDtypeStruct((m, d), jnp.float32)],
        scratch_shapes=[pltpu.VMEM((tile, d_ff), jnp.bfloat16)],
        compiler_params=pltpu.CompilerParams(
            dimension_semantics=("arbitrary",), vmem_limit_bytes=VMEM_LIMIT_BYTES),
        name="ffn",
    )(xa, xb, norm_w.reshape(1, d), w_gate_up, w_down, final_w.reshape(1, d))


def kernel(x_prompt, x_sample, cache_k, cache_v, state_hgrn, norm_mix, w_in, w_out, attn_sinks, rel_bias_table,
           hgrn_lb, hgrn_norm, norm_ffn, w_gate_up, w_down, norm_final):
    depth = w_in.shape[0]
    batch, seq, d = x_prompt.shape
    dec_batch, dec_seq, _ = x_sample.shape
    xp = x_prompt
    xs = x_sample.reshape(1, dec_batch * dec_seq, d)
    outs = [[] for _ in range(6)]
    for l in range(depth):
        w_in_l, w_out_l = _bf(w_in[l]), _bf(w_out[l])
        shared = (norm_mix[l], w_in_l, w_out_l, attn_sinks[l], rel_bias_table, hgrn_lb, hgrn_norm[l])
        xp, kp, vp, sp = _mixer(xp, None, *shared, layer=l, carry=True, tile=4 * CHUNK, chunk=CHUNK,
                                hg_block=2 * CHUNK)
        hist = (cache_k[l].reshape(dec_batch, WINDOW, KV_WIDTH), cache_v[l].reshape(dec_batch, WINDOW, KV_WIDTH),
                state_hgrn[l])
        xs, ks, vs, ss = _mixer(xs, hist, *shared, layer=l, carry=False, tile=dec_batch * dec_seq, chunk=dec_seq,
                                hg_block=dec_seq)
        last = l == depth - 1
        xp, xs = _ffn(xp.reshape(batch * seq, d), xs.reshape(dec_batch * dec_seq, d), norm_ffn[l], w_gate_up[l],
                      w_down[l], norm_final, tile=512, final_norm=last)
        xp = xp.reshape(batch, seq, d)
        xs = xs.reshape(1, dec_batch * dec_seq, d)
        ks = jnp.concatenate([hist[0][:, dec_seq:], ks.reshape(dec_batch, dec_seq, KV_WIDTH)], axis=1)
        vs = jnp.concatenate([hist[1][:, dec_seq:], vs.reshape(dec_batch, dec_seq, KV_WIDTH)], axis=1)
        kv_shape = (-1, WINDOW, N_KV_HEADS, HEAD_DIM)
        for acc, val in zip(outs, (kp.reshape(kv_shape), vp.reshape(kv_shape), sp,
                                   ks.reshape(kv_shape), vs.reshape(kv_shape), ss)):
            acc.append(val)
    return (xp, xs.reshape(dec_batch, dec_seq, d)) + tuple(jnp.stack(o) for o in outs)
```

```python
import functools
import math

import numpy as np
import jax
import jax.numpy as jnp
from jax import lax
from jax.experimental import pallas as pl
from jax.experimental.pallas import tpu as pltpu

CHUNK = 64
N_HEADS = 16
N_KV_HEADS = 2
HEAD_DIM = 64
GROUP = N_HEADS // N_KV_HEADS
KV_WIDTH = N_KV_HEADS * HEAD_DIM
WINDOW = 128
ATTN_SCALE = HEAD_DIM ** -0.5
LOG2E = math.log2(math.e)
N_BUCKETS = 32
MAX_DISTANCE = 128
HG_DK = 128
RMS_EPS = 1e-6
DECAY_GUARD = 75.0

LANES = 128
SUBLANES = 8
VMEM_LIMIT_BYTES = 56 * 1024 * 1024

_NT = (((1,), (1,)), ((), ()))
_TN = (((0,), (0,)), ((), ()))


def _bf(x):
    return x.astype(jnp.bfloat16)


def _dot(a, b, dims=None):
    if dims is None:
        return jnp.dot(a, b, preferred_element_type=jnp.float32)
    return lax.dot_general(a, b, dims, preferred_element_type=jnp.float32)


def _rms(x, w):
    return x * lax.rsqrt(jnp.mean(x * x, axis=-1, keepdims=True) + RMS_EPS) * w


def _t5_bucket(rel):
    nb = N_BUCKETS // 2
    max_exact = nb // 2
    ret = jnp.where(rel > 0, nb, 0)
    n = jnp.abs(rel)
    nf = jnp.maximum(n, 1).astype(jnp.float32)
    large = max_exact + (jnp.log(nf / max_exact) / math.log(MAX_DISTANCE / max_exact)
                         * (nb - max_exact)).astype(jnp.int32)
    large = jnp.minimum(large, nb - 1)
    return ret + jnp.where(n < max_exact, n, large)


def _levels(c):
    out, m = [], c // 2
    while m >= 1:
        out.append(m)
        m //= 2
    return out


def _level_masks(c):
    t = np.arange(c)[:, None]
    s = np.arange(c)[None, :]
    masks = []
    for m in _levels(c):
        masks.append((t // (2 * m) == s // (2 * m)) & ((t // m) % 2 == 1) & ((s // m) % 2 == 0))
    masks += [t == s, s <= t]
    return np.stack(masks).astype(np.float32)


def _row_masks(c):
    r = np.arange(c)
    rows = [(r // m) % 2 == 1 for m in _levels(c) if m < 8]
    rows += [r % 4 == 0, r % 4 >= 2, r % 4 == 3]
    return np.repeat(np.stack(rows).astype(np.float32)[:, :, None], LANES, axis=2)


def _block_cumsum(x, block):
    r, n = x.shape
    tiles = x.reshape(r // SUBLANES, SUBLANES, n)
    row = lax.broadcasted_iota(jnp.int32, (1, SUBLANES, n), 1)
    for shift in (1, 2, 4):
        tiles = tiles + jnp.where(row >= shift, pltpu.roll(tiles, shift, axis=1), 0.0)
    out, before = [], None
    for i in range(r // SUBLANES):
        t = tiles[i] if before is None or i % (block // SUBLANES) == 0 else tiles[i] + before
        out.append(t)
        before = jnp.broadcast_to(t[SUBLANES - 1:SUBLANES, :], (SUBLANES, n))
    return jnp.concatenate(out, axis=0)


def _head_variants(a):
    lo = lax.broadcasted_iota(jnp.int32, a.shape, 1) < HEAD_DIM
    rot = pltpu.roll(a, HEAD_DIM, axis=1)
    return [_bf(jnp.where(lo, a, 0.0)), _bf(jnp.where(lo, 0.0, rot)),
            _bf(jnp.where(lo, rot, 0.0)), _bf(jnp.where(lo, 0.0, a))]


def _mixer_kernel(*refs, tile, chunk, hg_block, carry, d_model, layer):
    n_chunks = tile // chunk
    keys = WINDOW + chunk
    key_pad = 2 * LANES
    assert keys < key_pad and tile >= WINDOW
    n_hg = d_model // HG_DK
    att_w = N_HEADS * HEAD_DIM
    n_pairs = N_HEADS // 2
    ppk = n_pairs // N_KV_HEADS
    rows_g = ppk * chunk
    n_var = WINDOW // chunk + 1 if carry else 1
    f32 = jnp.float32

    refs = list(refs)
    (x_ref, nw_ref, win_ref, wout_ref, sinks_ref, relb_ref, lbp_ref, gn_ref,
     lmask_ref, hmask_ref) = refs[:10]
    refs = refs[10:]
    if not carry:
        hk_ref, hv_ref, si_ref = refs[:3]
        refs = refs[3:]
    y_ref, ko_ref, vo_ref, so_ref = refs[:4]
    refs = refs[4:]
    (bias_s, q4_s, kv_s, kx_s, vx_s, hq_s, kk_s, hi_s, lf_s, bc_s, og_s, ga_s, gh_s, att_s, hg_s,
     u_s, sb_s, p_s, a_s, qe_s) = refs[:20]
    if carry:
        st_s = refs[20]

    first = (pl.program_id(0) == 0) & (pl.program_id(1) == 0)
    t_idx = pl.program_id(1)

    @pl.when(first)
    def _build_bias():
        lane = lax.broadcasted_iota(jnp.int32, (chunk, key_pad), 1)
        for head in range(N_HEADS):
            acc = jnp.where(lane == keys, sinks_ref[head], jnp.where(lane > keys, -jnp.inf, relb_ref[head])) * LOG2E
            pair, side = divmod(head, 2)
            kvh, j = divmod(pair, ppk)
            for var in range(n_var):
                n_invalid = WINDOW - var * chunk if carry else 0
                bias_s[var, 2 * kvh + side, j * chunk:(j + 1) * chunk, :] = jnp.where(lane < n_invalid, -jnp.inf, acc)

    if carry:
        @pl.when(t_idx == 0)
        def _reset():
            kx_s[:, 0, 0:WINDOW, :] = jnp.zeros((4, WINDOW, LANES), jnp.bfloat16)
            vx_s[:, 0, 0:WINDOW, :] = jnp.zeros((4, WINDOW, LANES), jnp.bfloat16)
            st_s[...] = jnp.zeros_like(st_s)

        @pl.when(t_idx > 0)
        def _shift():
            kx_s[:, 0, 0:WINDOW, :] = kx_s[:, 0, tile:tile + WINDOW, :]
            vx_s[:, 0, 0:WINDOW, :] = vx_s[:, 0, tile:tile + WINDOW, :]

    x = x_ref[0]
    h = _bf(_rms(x, nw_ref[...]))

    def proj(i0, width):
        return _dot(h, win_ref[:, i0:i0 + width])

    off = 0
    q = _bf(proj(off, att_w) * (ATTN_SCALE * LOG2E))
    for c in range(n_chunks):
        for j in range(n_pairs):
            q4_s[c, j] = q[c * chunk:(c + 1) * chunk, j * LANES:(j + 1) * LANES]
    off += att_w
    kv = proj(off, 2 * KV_WIDTH)
    kv_s[...] = kv
    off += 2 * KV_WIDTH
    k_var = _head_variants(kv[:, 0:KV_WIDTH])
    v_var = _head_variants(kv[:, KV_WIDTH:2 * KV_WIDTH])
    if carry:
        for g in range(4):
            kx_s[g, 0, WINDOW:WINDOW + tile, :] = k_var[g]
            vx_s[g, 0, WINDOW:WINDOW + tile, :] = v_var[g]
    else:
        for c in range(n_chunks):
            hk = _head_variants(hk_ref[c])
            hv = _head_variants(hv_ref[c])
            for g in range(4):
                kx_s[g, c, 0:WINDOW, :] = hk[g]
                vx_s[g, c, 0:WINDOW, :] = hv[g]
                kx_s[g, c, WINDOW:keys, :] = k_var[g][c * chunk:(c + 1) * chunk]
                vx_s[g, c, WINDOW:keys, :] = v_var[g][c * chunk:(c + 1) * chunk]
    off_q, off_f, off_i, off_og, off_ga, off_gh = (off + i * d_model for i in range(6))
    lbp = lbp_ref[...]
    e = jnp.exp(lbp - jnp.max(lbp, axis=0, keepdims=True))
    lb = jnp.sum(e[:layer + 1], axis=0, keepdims=True) / jnp.sum(e, axis=0, keepdims=True)

    def gate_quarter(i):
        cs = slice(i * d_model // 4, (i + 1) * d_model // 4)
        f = lb[:, cs] + (1.0 - lb[:, cs]) * jax.nn.sigmoid(proj(off_f + cs.start, d_model // 4))
        kk_s[:, cs] = 1.0 - f
        lf = jnp.log(f)
        lf_s[:, cs] = lf
        bc_s[:, cs] = _block_cumsum(lf, hg_block)

    gate_quarter(0)
    hq_s[...] = proj(off_q, d_model)
    gate_quarter(1)
    hi_s[...] = _bf(proj(off_i, d_model))
    gate_quarter(2)
    og_s[...] = jax.nn.silu(proj(off_og, d_model))
    gate_quarter(3)
    ga_s[...] = jax.nn.sigmoid(proj(off_ga, d_model))
    gh_s[...] = jax.nn.sigmoid(proj(off_gh, d_model))

    zpad = jnp.zeros((key_pad - keys, LANES), jnp.bfloat16)

    def window(ref, g, c):
        w = ref[g, 0, c * chunk:c * chunk + keys, :] if carry else ref[g, c]
        return jnp.concatenate([w, zpad], axis=0)

    for c in range(n_chunks):
        var = jnp.minimum(t_idx * n_chunks + c, n_var - 1) if carry else 0
        for kvh in range(N_KV_HEADS):
            qg = q4_s[c, kvh * ppk:(kvh + 1) * ppk].reshape(rows_g, LANES)
            for side in range(2):
                g = 2 * kvh + side
                s = _dot(qg, window(kx_s, g, c), _NT) + bias_s[var, g]
                p_s[c, g] = _bf(jnp.exp2(s - jnp.max(s, axis=-1, keepdims=True)))
    key_row = lax.broadcasted_iota(jnp.int32, (key_pad, LANES), 0)
    key_lane = lax.broadcasted_iota(jnp.int32, (key_pad, LANES), 1)
    ones_lo = _bf(jnp.where((key_row <= keys) & (key_lane < HEAD_DIM), 1.0, 0.0))
    ones_hi = _bf(jnp.where((key_row <= keys) & (key_lane >= HEAD_DIM), 1.0, 0.0))
    for c in range(n_chunks):
        for kvh in range(N_KV_HEADS):
            nd = (_dot(p_s[c, 2 * kvh], jnp.concatenate([window(vx_s, 2 * kvh, c), ones_lo], axis=1))
                  + _dot(p_s[c, 2 * kvh + 1], jnp.concatenate([window(vx_s, 2 * kvh + 1, c), ones_hi], axis=1)))
            o = nd[:, 0:LANES] * (1.0 / nd[:, LANES:2 * LANES])
            for j in range(ppk):
                col = (kvh * ppk + j) * LANES
                att_s[c * chunk:(c + 1) * chunk, col:col + LANES] = o[j * chunk:(j + 1) * chunk]

    blk, n_blk = hg_block, tile // hg_block
    levels = _levels(blk)
    gn = gn_ref[...]
    small = [m for m in levels if m < 8]

    def unit(c, hd):
        return slice(c * blk, (c + 1) * blk), slice(hd * HG_DK, (hd + 1) * HG_DK)

    def bc_row(c, r, cols):
        return bc_s[c * blk + r:c * blk + r + 1, cols]

    def level_operand(m, c, hd):
        rows, cols = unit(c, hd)
        q = hq_s[rows, cols]
        kk = kk_s[rows, cols]
        if m >= 8:
            bc = bc_s[rows, cols]
            parts = []
            for base in range(0, blk, 2 * m):
                ref = bc_row(c, base + m - 1, cols)
                lo, hi = slice(base, base + m), slice(base + m, base + 2 * m)
                parts += [kk[lo] * jnp.exp(ref - bc[lo]), q[hi] * jnp.exp(bc[hi] - ref)]
            return _bf(jnp.concatenate(parts, axis=0))
        upper = hmask_ref[small.index(m)] > 0.5
        if m == 1:
            return _bf(jnp.where(upper, q * (1.0 - kk), kk))
        if m == 4:
            bc = bc_s[rows, cols]
            ref_row = bc.reshape(blk // 8, 8, HG_DK)[:, 3:4, :]
            ref = jnp.broadcast_to(ref_row, (blk // 8, 8, HG_DK)).reshape(blk, HG_DK)
            arg = -jnp.abs(bc - ref)
        else:
            lfh = lf_s[rows, cols]
            nxt = pltpu.roll(lfh, blk - 1, axis=0)
            prv = pltpu.roll(lfh, 1, axis=0)
            n_small = len(small)
            arg = hmask_ref[n_small] * nxt + hmask_ref[n_small + 1] * lfh + hmask_ref[n_small + 2] * prv
        return _bf(jnp.where(upper, q, kk) * jnp.exp(arg))

    for c in range(n_blk):
        for hd in range(n_hg):
            rows, cols = unit(c, hd)
            qe_s[rows, cols] = _bf(hq_s[rows, cols] * jnp.exp(bc_s[rows, cols]))

    tril = lmask_ref[len(levels) + 1] > 0.5
    for c in range(n_blk):
        for hd in range(n_hg):
            rows, cols = unit(c, hd)
            k_grow_t = _bf((kk_s[rows, cols] * jnp.exp(-bc_s[rows, cols])).T)
            a_s[c, hd] = _bf(jnp.where(tril, _dot(qe_s[rows, cols], k_grow_t), 0.0))

    for c in range(n_blk):
        for hd in range(n_hg):
            rows, cols = unit(c, hd)
            k_dec = _bf(kk_s[rows, cols] * jnp.exp(bc_row(c, blk - 1, cols) - bc_s[rows, cols]))
            u_s[c, hd] = _dot(hi_s[rows, cols], k_dec, _TN)

    def readout(c):
        for hd in range(n_hg):
            rows, cols = unit(c, hd)
            o = _dot(qe_s[rows, cols], sb_s[c, hd]) + _dot(a_s[c, hd], hi_s[rows, cols])
            o = o * lax.rsqrt(jnp.mean(o * o, axis=-1, keepdims=True) + RMS_EPS) * gn
            hg_s[rows, cols] = o * og_s[rows, cols]

    def merge_and_project():
        merged = _bf(ga_s[...] * att_s[...] + gh_s[...] * hg_s[...])
        y_ref[0] = x_ref[0] + _dot(merged, wout_ref[...])

    if carry:
        states = [st_s[hd] for hd in range(n_hg)]
    for c in range(n_blk):
        for hd in range(n_hg):
            sb_s[c, hd] = _bf(states[hd].T) if carry else _bf(si_ref[c, hd])
        readout(c)
        for hd in range(n_hg):
            decay = jnp.exp(bc_row(c, blk - 1, unit(c, hd)[1]))
            if carry:
                states[hd] = states[hd] * decay + u_s[c, hd]
            else:
                so_ref[c, hd] = (si_ref[c, hd].T * decay + u_s[c, hd]).T
    if carry:
        for hd in range(n_hg):
            st_s[hd] = states[hd]
    merge_and_project()

    total_decay = jnp.concatenate([bc_row(c, blk - 1, slice(None)) for c in range(n_blk)], axis=0)

    @pl.when(jnp.logical_not(jnp.max(-total_decay) <= DECAY_GUARD))
    def _split():
        for c in range(n_blk):
            for hd in range(n_hg):
                rows, cols = unit(c, hd)
                a = _dot(_bf(hq_s[rows, cols]), _bf(kk_s[rows, cols]), _NT) * lmask_ref[len(levels)]
                for li, m in enumerate(levels):
                    z = level_operand(m, c, hd)
                    a = a + _dot(z, z, _NT) * lmask_ref[li]
                a_s[c, hd] = _bf(a)
        for c in range(n_blk):
            readout(c)
        merge_and_project()

    if carry:
        @pl.when(t_idx == pl.num_programs(1) - 1)
        def _emit():
            ko_ref[0] = kv_s[tile - WINDOW:tile, 0:KV_WIDTH]
            vo_ref[0] = kv_s[tile - WINDOW:tile, KV_WIDTH:2 * KV_WIDTH]
            for hd in range(n_hg):
                so_ref[0, hd] = st_s[hd].T
    else:
        ko_ref[0] = kv_s[:, 0:KV_WIDTH]
        vo_ref[0] = kv_s[:, KV_WIDTH:2 * KV_WIDTH]


def _full_spec(shape):
    nd = len(shape)
    return pl.BlockSpec(shape, lambda *_: (0,) * nd)


def _weight_spec(shape):
    nd = len(shape)
    return pl.BlockSpec(shape, lambda *_: (0,) * nd, pipeline_mode=pl.Buffered(1))


def _mixer(x, hist, norm_w, w_in, w_out, sinks, table, lb_params, gnorm_w, *, layer, carry, tile, chunk,
           hg_block):
    nb, s, d = x.shape
    n_t = s // tile
    n_chunks = tile // chunk
    n_blk = tile // hg_block
    assert carry or hg_block == chunk
    keys = WINDOW + chunk
    key_pad = 2 * LANES
    n_hg = d // HG_DK
    n_streams = nb if carry else s // chunk
    n_off = chunk + key_pad - 1
    offsets = jnp.arange(n_off, dtype=jnp.int32) - (chunk - 1) - WINDOW
    by_offset = table[_t5_bucket(offsets)].astype(jnp.float32).T
    period = jnp.roll(jnp.pad(by_offset, ((0, 0), (0, 1))), -(chunk - 1), axis=1)
    skewed = jnp.tile(period, (1, chunk))[:, :chunk * n_off].reshape(N_HEADS, chunk, n_off)
    rel_bias = skewed[:, :, :key_pad]
    lmask = jnp.asarray(_level_masks(hg_block))
    hmask = jnp.asarray(_row_masks(hg_block))
    smem = pl.BlockSpec(memory_space=pltpu.SMEM)

    in_arrays = [x, norm_w.reshape(1, d), w_in, w_out, sinks, rel_bias, lb_params,
                 gnorm_w.reshape(1, HG_DK), lmask, hmask]
    in_specs = [pl.BlockSpec((1, tile, d), lambda b, t: (b, t, 0)), _full_spec((1, d)),
                _weight_spec(w_in.shape), _weight_spec(w_out.shape), smem, _full_spec(rel_bias.shape),
                _full_spec(lb_params.shape), _full_spec((1, HG_DK)),
                _full_spec(lmask.shape), _full_spec(hmask.shape)]
    if carry:
        kv_rows = WINDOW
        st_block = (1, n_hg, HG_DK, HG_DK)
        n_win, win_rows, n_var = 1, WINDOW + tile, WINDOW // chunk + 1
    else:
        in_arrays += list(hist)
        in_specs += [_full_spec(a.shape) for a in hist]
        kv_rows = s
        st_block = (n_streams, n_hg, HG_DK, HG_DK)
        n_win, win_rows, n_var = n_chunks, keys, 1
    out_shape = [jax.ShapeDtypeStruct((nb, s, d), jnp.float32),
                 jax.ShapeDtypeStruct((nb, kv_rows, KV_WIDTH), jnp.float32),
                 jax.ShapeDtypeStruct((nb, kv_rows, KV_WIDTH), jnp.float32),
                 jax.ShapeDtypeStruct((n_streams, n_hg, HG_DK, HG_DK), jnp.float32)]
    out_specs = [pl.BlockSpec((1, tile, d), lambda b, t: (b, t, 0)),
                 pl.BlockSpec((1, kv_rows, KV_WIDTH), lambda b, t: (b, 0, 0)),
                 pl.BlockSpec((1, kv_rows, KV_WIDTH), lambda b, t: (b, 0, 0)),
                 pl.BlockSpec(st_block, lambda b, t: (b, 0, 0, 0))]
    f32, bf16 = jnp.float32, jnp.bfloat16
    rows_g = (N_HEADS // 2 // N_KV_HEADS) * chunk
    scratch = [pltpu.VMEM((n_var, 4, rows_g, key_pad), f32),
               pltpu.VMEM((n_chunks, N_HEADS // 2, chunk, LANES), bf16),
               pltpu.VMEM((tile, 2 * KV_WIDTH), f32),
               pltpu.VMEM((4, n_win, win_rows, LANES), bf16),
               pltpu.VMEM((4, n_win, win_rows, LANES), bf16),
               pltpu.VMEM((tile, d), f32), pltpu.VMEM((tile, d), f32),
               pltpu.VMEM((tile, d), bf16)]
    scratch += [pltpu.VMEM((tile, d), f32) for _ in range(7)]
    scratch += [pltpu.VMEM((n_blk, n_hg, HG_DK, HG_DK), f32),
                pltpu.VMEM((n_blk, n_hg, HG_DK, HG_DK), bf16),
                pltpu.VMEM((n_chunks, 4, rows_g, key_pad), bf16),
                pltpu.VMEM((n_blk, n_hg, hg_block, hg_block), bf16),
                pltpu.VMEM((tile, d), bf16)]
    if carry:
        scratch += [pltpu.VMEM((n_hg, HG_DK, HG_DK), f32)]
    kern = functools.partial(_mixer_kernel, tile=tile, chunk=chunk, hg_block=hg_block, carry=carry, d_model=d,
                             layer=layer)
    return pl.pallas_call(
        kern,
        grid=(nb, n_t),
        in_specs=in_specs,
        out_specs=out_specs,
        out_shape=out_shape,
        scratch_shapes=scratch,
        compiler_params=pltpu.CompilerParams(
            dimension_semantics=("arbitrary", "arbitrary"), vmem_limit_bytes=VMEM_LIMIT_BYTES),
        name="mixer_prompt" if carry else "mixer_sample",
    )(*in_arrays)


def _ffn_kernel(xa_ref, xb_ref, nw_ref, wgu_ref, wd_ref, fw_ref, ya_ref, yb_ref, act_s, *, d_ff, col_tile,
                final_norm):
    n_a = pl.num_programs(0) - 1

    def ffn(x_ref, y_ref):
        rows = x_ref.shape[0]
        x = x_ref[...]
        h = _bf(_rms(x, nw_ref[...]))
        for j in range(d_ff // col_tile):
            g = _dot(h, _bf(wgu_ref[:, j * col_tile:(j + 1) * col_tile]))
            u = _dot(h, _bf(wgu_ref[:, d_ff + j * col_tile:d_ff + (j + 1) * col_tile]))
            act_s[0:rows, j * col_tile:(j + 1) * col_tile] = _bf(jax.nn.silu(g) * u)
        y = x + _dot(act_s[0:rows, :], _bf(wd_ref[...]))
        if final_norm:
            y = _rms(y, fw_ref[...])
        y_ref[...] = y

    @pl.when(pl.program_id(0) < n_a)
    def _first_set():
        ffn(xa_ref, ya_ref)

    @pl.when(pl.program_id(0) == n_a)
    def _second_set():
        ffn(xb_ref, yb_ref)


def _ffn(xa, xb, norm_w, w_gate_up, w_down, final_w, *, tile, final_norm):
    n, d = xa.shape
    m = xb.shape[0]
    n_a = n // tile
    assert n == n_a * tile and m <= tile
    d_ff = w_down.shape[0]
    kern = functools.partial(_ffn_kernel, d_ff=d_ff, col_tile=2 * LANES, final_norm=final_norm)
    a_spec = pl.BlockSpec((tile, d), lambda i: (jnp.minimum(i, n_a - 1), 0))
    return pl.pallas_call(
        kern,
        grid=(n_a + 1,),
        in_specs=[a_spec, _full_spec((m, d)), _full_spec((1, d)),
                  _weight_spec(w_gate_up.shape), _weight_spec(w_down.shape), _full_spec((1, d))],
        out_specs=[a_spec, _full_spec((m, d))],
        out_shape=[jax.ShapeDtypeStruct((n, d), jnp.float32), jax.ShapeDtypeStruct((m, d), jnp.float32)],
        scratch_shapes=[pltpu.VMEM((tile, d_ff), jnp.bfloat16)],
        compiler_params=pltpu.CompilerParams(
            dimension_semantics=("arbitrary",), vmem_limit_bytes=VMEM_LIMIT_BYTES),
        name="ffn",
    )(xa, xb, norm_w.reshape(1, d), w_gate_up, w_down, final_w.reshape(1, d))


def kernel(x_prompt, x_sample, cache_k, cache_v, state_hgrn, norm_mix, w_in, w_out, attn_sinks, rel_bias_table,
           hgrn_lb, hgrn_norm, norm_ffn, w_gate_up, w_down, norm_final):
    depth = w_in.shape[0]
    batch, seq, d = x_prompt.shape
    dec_batch, dec_seq, _ = x_sample.shape
    xp = x_prompt
    xs = x_sample.reshape(1, dec_batch * dec_seq, d)
    outs = [[] for _ in range(6)]
    for l in range(depth):
        w_in_l, w_out_l = _bf(w_in[l]), _bf(w_out[l])
        shared = (norm_mix[l], w_in_l, w_out_l, attn_sinks[l], rel_bias_table, hgrn_lb, hgrn_norm[l])
        xp, kp, vp, sp = _mixer(xp, None, *shared, layer=l, carry=True, tile=4 * CHUNK, chunk=CHUNK,
                                hg_block=2 * CHUNK)
        hist = (cache_k[l].reshape(dec_batch, WINDOW, KV_WIDTH), cache_v[l].reshape(dec_batch, WINDOW, KV_WIDTH),
                state_hgrn[l])
        xs, ks, vs, ss = _mixer(xs, hist, *shared, layer=l, carry=False, tile=dec_batch * dec_seq, chunk=dec_seq,
                                hg_block=dec_seq)
        last = l == depth - 1
        xp, xs = _ffn(xp.reshape(batch * seq, d), xs.reshape(dec_batch * dec_seq, d), norm_ffn[l], w_gate_up[l],
                      w_down[l], norm_final, tile=512, final_norm=last)
        xp = xp.reshape(batch, seq, d)
        xs = xs.reshape(1, dec_batch * dec_seq, d)
        ks = jnp.concatenate([hist[0][:, dec_seq:], ks.reshape(dec_batch, dec_seq, KV_WIDTH)], axis=1)
        vs = jnp.concatenate([hist[1][:, dec_seq:], vs.reshape(dec_batch, dec_seq, KV_WIDTH)], axis=1)
        kv_shape = (-1, WINDOW, N_KV_HEADS, HEAD_DIM)
        for acc, val in zip(outs, (kp.reshape(kv_shape), vp.reshape(kv_shape), sp,
                                   ks.reshape(kv_shape), vs.reshape(kv_shape), ss)):
            acc.append(val)
    return (xp, xs.reshape(dec_batch, dec_seq, d)) + tuple(jnp.stack(o) for o in outs)
```

```python
import functools
import math

import numpy as np
import jax
import jax.numpy as jnp
from jax import lax
from jax.experimental import pallas as pl
from jax.experimental.pallas import tpu as pltpu

CHUNK = 64
N_HEADS = 16
N_KV_HEADS = 2
HEAD_DIM = 64
KV_WIDTH = N_KV_HEADS * HEAD_DIM
WINDOW = 128
ATTN_SCALE = HEAD_DIM ** -0.5
LOG2E = math.log2(math.e)
N_BUCKETS = 32
MAX_DISTANCE = 128
HG_DK = 128
RMS_EPS = 1e-6
DECAY_GUARD = -1.0

LANES = 128
SUBLANES = 8
VMEM_LIMIT_BYTES = 56 * 1024 * 1024
PROMPT_TILE = 4 * CHUNK
HGRN_BLOCK = 2 * CHUNK
FFN_TILE = 512

_NT = (((1,), (1,)), ((), ()))
_TN = (((0,), (0,)), ((), ()))


def _bf(x):
    return x.astype(jnp.bfloat16)


def _dot(a, b, dims=None):
    if dims is None:
        return jnp.dot(a, b, preferred_element_type=jnp.float32)
    return lax.dot_general(a, b, dims, preferred_element_type=jnp.float32)


def _rms(x, w):
    return x * lax.rsqrt(jnp.mean(x * x, axis=-1, keepdims=True) + RMS_EPS) * w


def _t5_bucket(rel):
    nb = N_BUCKETS // 2
    max_exact = nb // 2
    ret = jnp.where(rel > 0, nb, 0)
    n = jnp.abs(rel)
    nf = jnp.maximum(n, 1).astype(jnp.float32)
    large = max_exact + (jnp.log(nf / max_exact) / math.log(MAX_DISTANCE / max_exact)
                         * (nb - max_exact)).astype(jnp.int32)
    large = jnp.minimum(large, nb - 1)
    return ret + jnp.where(n < max_exact, n, large)


def _levels(c):
    out, m = [], c // 2
    while m >= 1:
        out.append(m)
        m //= 2
    return out


def _level_masks(c):
    t = np.arange(c)[:, None]
    s = np.arange(c)[None, :]
    masks = []
    for m in _levels(c):
        masks.append((t // (2 * m) == s // (2 * m)) & ((t // m) % 2 == 1) & ((s // m) % 2 == 0))
    masks += [t == s, s <= t]
    return np.stack(masks).astype(np.float32)


def _row_masks(c):
    r = np.arange(c)
    rows = [(r // m) % 2 == 1 for m in _levels(c) if m < 8]
    rows += [r % 4 == 0, r % 4 >= 2, r % 4 == 3]
    return np.repeat(np.stack(rows).astype(np.float32)[:, :, None], LANES, axis=2)


def _block_cumsum(x, block):
    r, n = x.shape
    tiles = x.reshape(r // SUBLANES, SUBLANES, n)
    row = lax.broadcasted_iota(jnp.int32, (1, SUBLANES, n), 1)
    for shift in (1, 2, 4):
        tiles = tiles + jnp.where(row >= shift, pltpu.roll(tiles, shift, axis=1), 0.0)
    out, before = [], None
    for i in range(r // SUBLANES):
        t = tiles[i] if before is None or i % (block // SUBLANES) == 0 else tiles[i] + before
        out.append(t)
        before = jnp.broadcast_to(t[SUBLANES - 1:SUBLANES, :], (SUBLANES, n))
    return jnp.concatenate(out, axis=0)


def _head_variants(a):
    lo = lax.broadcasted_iota(jnp.int32, a.shape, 1) < HEAD_DIM
    rot = pltpu.roll(a, HEAD_DIM, axis=1)
    return [_bf(jnp.where(lo, a, 0.0)), _bf(jnp.where(lo, 0.0, rot)),
            _bf(jnp.where(lo, rot, 0.0)), _bf(jnp.where(lo, 0.0, a))]


def _mixer_kernel(*refs, tile, chunk, hg_block, carry, d_model, layer):
    n_chunks = tile // chunk
    keys = WINDOW + chunk
    key_pad = 2 * LANES
    assert keys < key_pad and tile >= WINDOW
    n_hg = d_model // HG_DK
    att_w = N_HEADS * HEAD_DIM
    n_pairs = N_HEADS // 2
    ppk = n_pairs // N_KV_HEADS
    rows_g = ppk * chunk
    n_var = WINDOW // chunk + 1 if carry else 1
    f32 = jnp.float32

    refs = list(refs)
    (x_ref, nw_ref, win_ref, wout_ref, sinks_ref, relb_ref, lbp_ref, gn_ref,
     lmask_ref, hmask_ref) = refs[:10]
    refs = refs[10:]
    if not carry:
        hk_ref, hv_ref, si_ref = refs[:3]
        refs = refs[3:]
    y_ref, ko_ref, vo_ref, so_ref = refs[:4]
    refs = refs[4:]
    (bias_s, q4_s, kv_s, kx_s, vx_s, hq_s, kk_s, hi_s, lf_s, bc_s, og_s, ga_s, gh_s, att_s, hg_s,
     u_s, sb_s, p_s, a_s, qe_s) = refs[:20]
    if carry:
        st_s = refs[20]

    first = (pl.program_id(0) == 0) & (pl.program_id(1) == 0)
    t_idx = pl.program_id(1)

    @pl.when(first)
    def _build_bias():
        lane = lax.broadcasted_iota(jnp.int32, (chunk, key_pad), 1)
        for head in range(N_HEADS):
            acc = jnp.where(lane == keys, sinks_ref[head], jnp.where(lane > keys, -jnp.inf, relb_ref[head])) * LOG2E
            pair, side = divmod(head, 2)
            kvh, j = divmod(pair, ppk)
            for var in range(n_var):
                n_invalid = WINDOW - var * chunk if carry else 0
                bias_s[var, 2 * kvh + side, j * chunk:(j + 1) * chunk, :] = jnp.where(lane < n_invalid, -jnp.inf, acc)

    if carry:
        @pl.when(t_idx == 0)
        def _reset():
            kx_s[:, 0, 0:WINDOW, :] = jnp.zeros((4, WINDOW, LANES), jnp.bfloat16)
            vx_s[:, 0, 0:WINDOW, :] = jnp.zeros((4, WINDOW, LANES), jnp.bfloat16)
            st_s[...] = jnp.zeros_like(st_s)

        @pl.when(t_idx > 0)
        def _shift():
            kx_s[:, 0, 0:WINDOW, :] = kx_s[:, 0, tile:tile + WINDOW, :]
            vx_s[:, 0, 0:WINDOW, :] = vx_s[:, 0, tile:tile + WINDOW, :]

    x = x_ref[0]
    h = _bf(_rms(x, nw_ref[...]))

    def proj(i0, width):
        return _dot(h, win_ref[:, i0:i0 + width])

    off = 0
    q = _bf(proj(off, att_w) * (ATTN_SCALE * LOG2E))
    for c in range(n_chunks):
        for j in range(n_pairs):
            q4_s[c, j] = q[c * chunk:(c + 1) * chunk, j * LANES:(j + 1) * LANES]
    off += att_w
    kv = proj(off, 2 * KV_WIDTH)
    kv_s[...] = kv
    off += 2 * KV_WIDTH
    k_var = _head_variants(kv[:, 0:KV_WIDTH])
    v_var = _head_variants(kv[:, KV_WIDTH:2 * KV_WIDTH])
    if carry:
        for g in range(4):
            kx_s[g, 0, WINDOW:WINDOW + tile, :] = k_var[g]
            vx_s[g, 0, WINDOW:WINDOW + tile, :] = v_var[g]
    else:
        for c in range(n_chunks):
            hk = _head_variants(hk_ref[c])
            hv = _head_variants(hv_ref[c])
            for g in range(4):
                kx_s[g, c, 0:WINDOW, :] = hk[g]
                vx_s[g, c, 0:WINDOW, :] = hv[g]
                kx_s[g, c, WINDOW:keys, :] = k_var[g][c * chunk:(c + 1) * chunk]
                vx_s[g, c, WINDOW:keys, :] = v_var[g][c * chunk:(c + 1) * chunk]
    off_q, off_f, off_i, off_og, off_ga, off_gh = (off + i * d_model for i in range(6))
    lbp = lbp_ref[...]
    e = jnp.exp(lbp - jnp.max(lbp, axis=0, keepdims=True))
    lb = jnp.sum(e[:layer + 1], axis=0, keepdims=True) / jnp.sum(e, axis=0, keepdims=True)

    def gate_quarter(i):
        cs = slice(i * d_model // 4, (i + 1) * d_model // 4)
        f = lb[:, cs] + (1.0 - lb[:, cs]) * jax.nn.sigmoid(proj(off_f + cs.start, d_model // 4))
        kk_s[:, cs] = 1.0 - f
        lf = jnp.log(f)
        lf_s[:, cs] = lf
        bc_s[:, cs] = _block_cumsum(lf, hg_block)

    gate_quarter(0)
    hq_s[...] = proj(off_q, d_model)
    gate_quarter(1)
    hi_s[...] = _bf(proj(off_i, d_model))
    gate_quarter(2)
    og_s[...] = jax.nn.silu(proj(off_og, d_model))
    gate_quarter(3)
    ga_s[...] = jax.nn.sigmoid(proj(off_ga, d_model))
    gh_s[...] = jax.nn.sigmoid(proj(off_gh, d_model))

    zpad = jnp.zeros((key_pad - keys, LANES), jnp.bfloat16)

    def window(ref, g, c):
        w = ref[g, 0, c * chunk:c * chunk + keys, :] if carry else ref[g, c]
        return jnp.concatenate([w, zpad], axis=0)

    for c in range(n_chunks):
        var = jnp.minimum(t_idx * n_chunks + c, n_var - 1) if carry else 0
        for kvh in range(N_KV_HEADS):
            qg = q4_s[c, kvh * ppk:(kvh + 1) * ppk].reshape(rows_g, LANES)
            for side in range(2):
                g = 2 * kvh + side
                s = _dot(qg, window(kx_s, g, c), _NT) + bias_s[var, g]
                p_s[c, g] = _bf(jnp.exp2(s - jnp.max(s, axis=-1, keepdims=True)))
    key_row = lax.broadcasted_iota(jnp.int32, (key_pad, LANES), 0)
    key_lane = lax.broadcasted_iota(jnp.int32, (key_pad, LANES), 1)
    ones_lo = _bf(jnp.where((key_row <= keys) & (key_lane < HEAD_DIM), 1.0, 0.0))
    ones_hi = _bf(jnp.where((key_row <= keys) & (key_lane >= HEAD_DIM), 1.0, 0.0))
    for c in range(n_chunks):
        for kvh in range(N_KV_HEADS):
            nd = (_dot(p_s[c, 2 * kvh], jnp.concatenate([window(vx_s, 2 * kvh, c), ones_lo], axis=1))
                  + _dot(p_s[c, 2 * kvh + 1], jnp.concatenate([window(vx_s, 2 * kvh + 1, c), ones_hi], axis=1)))
            o = nd[:, 0:LANES] * (1.0 / nd[:, LANES:2 * LANES])
            for j in range(ppk):
                col = (kvh * ppk + j) * LANES
                att_s[c * chunk:(c + 1) * chunk, col:col + LANES] = o[j * chunk:(j + 1) * chunk]

    blk, n_blk = hg_block, tile // hg_block
    levels = _levels(blk)
    gn = gn_ref[...]
    small = [m for m in levels if m < 8]

    def unit(c, hd):
        return slice(c * blk, (c + 1) * blk), slice(hd * HG_DK, (hd + 1) * HG_DK)

    def bc_row(c, r, cols):
        return bc_s[c * blk + r:c * blk + r + 1, cols]

    def level_operand(m, c, hd):
        rows, cols = unit(c, hd)
        q = hq_s[rows, cols]
        kk = kk_s[rows, cols]
        if m >= 8:
            bc = bc_s[rows, cols]
            parts = []
            for base in range(0, blk, 2 * m):
                ref = bc_row(c, base + m - 1, cols)
                lo, hi = slice(base, base + m), slice(base + m, base + 2 * m)
                parts += [kk[lo] * jnp.exp(ref - bc[lo]), q[hi] * jnp.exp(bc[hi] - ref)]
            return _bf(jnp.concatenate(parts, axis=0))
        upper = hmask_ref[small.index(m)] > 0.5
        if m == 1:
            return _bf(jnp.where(upper, q * (1.0 - kk), kk))
        if m == 4:
            bc = bc_s[rows, cols]
            ref_row = bc.reshape(blk // 8, 8, HG_DK)[:, 3:4, :]
            ref = jnp.broadcast_to(ref_row, (blk // 8, 8, HG_DK)).reshape(blk, HG_DK)
            arg = -jnp.abs(bc - ref)
        else:
            lfh = lf_s[rows, cols]
            nxt = pltpu.roll(lfh, blk - 1, axis=0)
            prv = pltpu.roll(lfh, 1, axis=0)
            n_small = len(small)
            arg = hmask_ref[n_small] * nxt + hmask_ref[n_small + 1] * lfh + hmask_ref[n_small + 2] * prv
        return _bf(jnp.where(upper, q, kk) * jnp.exp(arg))

    for c in range(n_blk):
        for hd in range(n_hg):
            rows, cols = unit(c, hd)
            qe_s[rows, cols] = _bf(hq_s[rows, cols] * jnp.exp(bc_s[rows, cols]))

    tril = lmask_ref[len(levels) + 1] > 0.5
    for c in range(n_blk):
        for hd in range(n_hg):
            rows, cols = unit(c, hd)
            k_grow_t = _bf((kk_s[rows, cols] * jnp.exp(-bc_s[rows, cols])).T)
            a_s[c, hd] = _bf(jnp.where(tril, _dot(qe_s[rows, cols], k_grow_t), 0.0))

    for c in range(n_blk):
        for hd in range(n_hg):
            rows, cols = unit(c, hd)
            k_dec = _bf(kk_s[rows, cols] * jnp.exp(bc_row(c, blk - 1, cols) - bc_s[rows, cols]))
            u_s[c, hd] = _dot(hi_s[rows, cols], k_dec, _TN)

    def readout(c):
        for hd in range(n_hg):
            rows, cols = unit(c, hd)
            o = _dot(qe_s[rows, cols], sb_s[c, hd]) + _dot(a_s[c, hd], hi_s[rows, cols])
            o = o * lax.rsqrt(jnp.mean(o * o, axis=-1, keepdims=True) + RMS_EPS) * gn
            hg_s[rows, cols] = o * og_s[rows, cols]

    def merge_and_project():
        merged = _bf(ga_s[...] * att_s[...] + gh_s[...] * hg_s[...])
        y_ref[0] = x_ref[0] + _dot(merged, wout_ref[...])

    if carry:
        states = [st_s[hd] for hd in range(n_hg)]
    for c in range(n_blk):
        for hd in range(n_hg):
            sb_s[c, hd] = _bf(states[hd].T) if carry else _bf(si_ref[c, hd])
        readout(c)
        for hd in range(n_hg):
            decay = jnp.exp(bc_row(c, blk - 1, unit(c, hd)[1]))
            if carry:
                states[hd] = states[hd] * decay + u_s[c, hd]
            else:
                so_ref[c, hd] = (si_ref[c, hd].T * decay + u_s[c, hd]).T
    if carry:
        for hd in range(n_hg):
            st_s[hd] = states[hd]
    merge_and_project()

    total_decay = jnp.concatenate([bc_row(c, blk - 1, slice(None)) for c in range(n_blk)], axis=0)

    @pl.when(jnp.logical_not(jnp.max(-total_decay) <= DECAY_GUARD))
    def _split():
        for c in range(n_blk):
            for hd in range(n_hg):
                rows, cols = unit(c, hd)
                a = _dot(_bf(hq_s[rows, cols]), _bf(kk_s[rows, cols]), _NT) * lmask_ref[len(levels)]
                for li, m in enumerate(levels):
                    z = level_operand(m, c, hd)
                    a = a + _dot(z, z, _NT) * lmask_ref[li]
                a_s[c, hd] = _bf(a)
        for c in range(n_blk):
            readout(c)
        merge_and_project()

    if carry:
        @pl.when(t_idx == pl.num_programs(1) - 1)
        def _emit():
            ko_ref[0] = kv_s[tile - WINDOW:tile, 0:KV_WIDTH]
            vo_ref[0] = kv_s[tile - WINDOW:tile, KV_WIDTH:2 * KV_WIDTH]
            for hd in range(n_hg):
                so_ref[0, hd] = st_s[hd].T
    else:
        ko_ref[0] = kv_s[:, 0:KV_WIDTH]
        vo_ref[0] = kv_s[:, KV_WIDTH:2 * KV_WIDTH]


def _full_spec(shape):
    nd = len(shape)
    return pl.BlockSpec(shape, lambda *_: (0,) * nd)


def _weight_spec(shape):
    nd = len(shape)
    return pl.BlockSpec(shape, lambda *_: (0,) * nd, pipeline_mode=pl.Buffered(1))


def _mixer(x, hist, norm_w, w_in, w_out, sinks, table, lb_params, gnorm_w, *, layer, carry, tile, chunk,
           hg_block):
    nb, s, d = x.shape
    n_t = s // tile
    n_chunks = tile // chunk
    n_blk = tile // hg_block
    assert carry or hg_block == chunk
    keys = WINDOW + chunk
    key_pad = 2 * LANES
    n_hg = d // HG_DK
    n_streams = nb if carry else s // chunk
    n_off = chunk + key_pad - 1
    offsets = jnp.arange(n_off, dtype=jnp.int32) - (chunk - 1) - WINDOW
    by_offset = table[_t5_bucket(offsets)].astype(jnp.float32).T
    period = jnp.roll(jnp.pad(by_offset, ((0, 0), (0, 1))), -(chunk - 1), axis=1)
    skewed = jnp.tile(period, (1, chunk))[:, :chunk * n_off].reshape(N_HEADS, chunk, n_off)
    rel_bias = skewed[:, :, :key_pad]
    lmask = jnp.asarray(_level_masks(hg_block))
    hmask = jnp.asarray(_row_masks(hg_block))
    smem = pl.BlockSpec(memory_space=pltpu.SMEM)

    in_arrays = [x, norm_w.reshape(1, d), w_in, w_out, sinks, rel_bias, lb_params,
                 gnorm_w.reshape(1, HG_DK), lmask, hmask]
    in_specs = [pl.BlockSpec((1, tile, d), lambda b, t: (b, t, 0)), _full_spec((1, d)),
                _weight_spec(w_in.shape), _weight_spec(w_out.shape), smem, _full_spec(rel_bias.shape),
                _full_spec(lb_params.shape), _full_spec((1, HG_DK)),
                _full_spec(lmask.shape), _full_spec(hmask.shape)]
    if carry:
        kv_rows = WINDOW
        st_block = (1, n_hg, HG_DK, HG_DK)
        n_win, win_rows, n_var = 1, WINDOW + tile, WINDOW // chunk + 1
    else:
        in_arrays += list(hist)
        in_specs += [_full_spec(a.shape) for a in hist]
        kv_rows = s
        st_block = (n_streams, n_hg, HG_DK, HG_DK)
        n_win, win_rows, n_var = n_chunks, keys, 1
    out_shape = [jax.ShapeDtypeStruct((nb, s, d), jnp.float32),
                 jax.ShapeDtypeStruct((nb, kv_rows, KV_WIDTH), jnp.float32),
                 jax.ShapeDtypeStruct((nb, kv_rows, KV_WIDTH), jnp.float32),
                 jax.ShapeDtypeStruct((n_streams, n_hg, HG_DK, HG_DK), jnp.float32)]
    out_specs = [pl.BlockSpec((1, tile, d), lambda b, t: (b, t, 0)),
                 pl.BlockSpec((1, kv_rows, KV_WIDTH), lambda b, t: (b, 0, 0)),
                 pl.BlockSpec((1, kv_rows, KV_WIDTH), lambda b, t: (b, 0, 0)),
                 pl.BlockSpec(st_block, lambda b, t: (b, 0, 0, 0))]
    f32, bf16 = jnp.float32, jnp.bfloat16
    rows_g = (N_HEADS // 2 // N_KV_HEADS) * chunk
    scratch = [pltpu.VMEM((n_var, 4, rows_g, key_pad), f32),
               pltpu.VMEM((n_chunks, N_HEADS // 2, chunk, LANES), bf16),
               pltpu.VMEM((tile, 2 * KV_WIDTH), f32),
               pltpu.VMEM((4, n_win, win_rows, LANES), bf16),
               pltpu.VMEM((4, n_win, win_rows, LANES), bf16),
               pltpu.VMEM((tile, d), f32), pltpu.VMEM((tile, d), f32),
               pltpu.VMEM((tile, d), bf16)]
    scratch += [pltpu.VMEM((tile, d), f32) for _ in range(7)]
    scratch += [pltpu.VMEM((n_blk, n_hg, HG_DK, HG_DK), f32),
                pltpu.VMEM((n_blk, n_hg, HG_DK, HG_DK), bf16),
                pltpu.VMEM((n_chunks, 4, rows_g, key_pad), bf16),
                pltpu.VMEM((n_blk, n_hg, hg_block, hg_block), bf16),
                pltpu.VMEM((tile, d), bf16)]
    if carry:
        scratch += [pltpu.VMEM((n_hg, HG_DK, HG_DK), f32)]
    kern = functools.partial(_mixer_kernel, tile=tile, chunk=chunk, hg_block=hg_block, carry=carry, d_model=d,
                             layer=layer)
    return pl.pallas_call(
        kern,
        grid=(nb, n_t),
        in_specs=in_specs,
        out_specs=out_specs,
        out_shape=out_shape,
        scratch_shapes=scratch,
        compiler_params=pltpu.CompilerParams(
            dimension_semantics=("arbitrary", "arbitrary"), vmem_limit_bytes=VMEM_LIMIT_BYTES),
        name="mixer_prompt" if carry else "mixer_sample",
    )(*in_arrays)


def _ffn_kernel(xa_ref, xb_ref, nw_ref, wgu_ref, wd_ref, fw_ref, ya_ref, yb_ref, act_s, *, d_ff, col_tile,
                final_norm):
    n_a = pl.num_programs(0) - 1

    def ffn(x_ref, y_ref):
        rows = x_ref.shape[0]
        x = x_ref[...]
        h = _bf(_rms(x, nw_ref[...]))
        for j in range(d_ff // col_tile):
            g = _dot(h, _bf(wgu_ref[:, j * col_tile:(j + 1) * col_tile]))
            u = _dot(h, _bf(wgu_ref[:, d_ff + j * col_tile:d_ff + (j + 1) * col_tile]))
            act_s[0:rows, j * col_tile:(j + 1) * col_tile] = _bf(jax.nn.silu(g) * u)
        y = x + _dot(act_s[0:rows, :], _bf(wd_ref[...]))
        if final_norm:
            y = _rms(y, fw_ref[...])
        y_ref[...] = y

    @pl.when(pl.program_id(0) < n_a)
    def _first_set():
        ffn(xa_ref, ya_ref)

    @pl.when(pl.program_id(0) == n_a)
    def _second_set():
        ffn(xb_ref, yb_ref)


def _ffn(xa, xb, norm_w, w_gate_up, w_down, final_w, *, tile, final_norm):
    n, d = xa.shape
    m = xb.shape[0]
    n_a = n // tile
    assert n == n_a * tile and m <= tile
    d_ff = w_down.shape[0]
    kern = functools.partial(_ffn_kernel, d_ff=d_ff, col_tile=2 * LANES, final_norm=final_norm)
    a_spec = pl.BlockSpec((tile, d), lambda i: (jnp.minimum(i, n_a - 1), 0))
    return pl.pallas_call(
        kern,
        grid=(n_a + 1,),
        in_specs=[a_spec, _full_spec((m, d)), _full_spec((1, d)),
                  _weight_spec(w_gate_up.shape), _weight_spec(w_down.shape), _full_spec((1, d))],
        out_specs=[a_spec, _full_spec((m, d))],
        out_shape=[jax.ShapeDtypeStruct((n, d), jnp.float32), jax.ShapeDtypeStruct((m, d), jnp.float32)],
        scratch_shapes=[pltpu.VMEM((tile, d_ff), jnp.bfloat16)],
        compiler_params=pltpu.CompilerParams(
            dimension_semantics=("arbitrary",), vmem_limit_bytes=VMEM_LIMIT_BYTES),
        name="ffn",
    )(xa, xb, norm_w.reshape(1, d), w_gate_up, w_down, final_w.reshape(1, d))


def kernel(x_prompt, x_sample, cache_k, cache_v, state_hgrn, norm_mix, w_in, w_out, attn_sinks, rel_bias_table,
           hgrn_lb, hgrn_norm, norm_ffn, w_gate_up, w_down, norm_final):
    depth = w_in.shape[0]
    batch, seq, d = x_prompt.shape
    dec_batch, dec_seq, _ = x_sample.shape
    assert w_in.shape[2] == N_HEADS * HEAD_DIM + 2 * KV_WIDTH + 6 * d and KV_WIDTH == LANES and d % (4 * LANES) == 0
    assert seq % PROMPT_TILE == 0 and (batch * seq) % FFN_TILE == 0 and dec_batch * dec_seq <= FFN_TILE
    assert dec_seq % (2 * SUBLANES) == 0 and cache_k.shape[2] == WINDOW
    xp = x_prompt
    xs = x_sample.reshape(1, dec_batch * dec_seq, d)
    outs = [[] for _ in range(6)]
    for l in range(depth):
        w_in_l, w_out_l = _bf(w_in[l]), _bf(w_out[l])
        shared = (norm_mix[l], w_in_l, w_out_l, attn_sinks[l], rel_bias_table, hgrn_lb, hgrn_norm[l])
        xp, kp, vp, sp = _mixer(xp, None, *shared, layer=l, carry=True, tile=PROMPT_TILE, chunk=CHUNK,
                                hg_block=HGRN_BLOCK)
        hist = (cache_k[l].reshape(dec_batch, WINDOW, KV_WIDTH), cache_v[l].reshape(dec_batch, WINDOW, KV_WIDTH),
                state_hgrn[l])
        xs, ks, vs, ss = _mixer(xs, hist, *shared, layer=l, carry=False, tile=dec_batch * dec_seq, chunk=dec_seq,
                                hg_block=dec_seq)
        last = l == depth - 1
        xp, xs = _ffn(xp.reshape(batch * seq, d), xs.reshape(dec_batch * dec_seq, d), norm_ffn[l], w_gate_up[l],
                      w_down[l], norm_final, tile=FFN_TILE, final_norm=last)
        xp = xp.reshape(batch, seq, d)
        xs = xs.reshape(1, dec_batch * dec_seq, d)
        ks = jnp.concatenate([hist[0][:, dec_seq:], ks.reshape(dec_batch, dec_seq, KV_WIDTH)], axis=1)
        vs = jnp.concatenate([hist[1][:, dec_seq:], vs.reshape(dec_batch, dec_seq, KV_WIDTH)], axis=1)
        kv_shape = (-1, WINDOW, N_KV_HEADS, HEAD_DIM)
        for acc, val in zip(outs, (kp.reshape(kv_shape), vp.reshape(kv_shape), sp,
                                   ks.reshape(kv_shape), vs.reshape(kv_shape), ss)):
            acc.append(val)
    return (xp, xs.reshape(dec_batch, dec_seq, d)) + tuple(jnp.stack(o) for o in outs)
```

```python
import functools
import math

import numpy as np
import jax
import jax.numpy as jnp
from jax import lax
from jax.experimental import pallas as pl
from jax.experimental.pallas import tpu as pltpu

CHUNK = 64
N_HEADS = 16
N_KV_HEADS = 2
HEAD_DIM = 64
KV_WIDTH = N_KV_HEADS * HEAD_DIM
WINDOW = 128
ATTN_SCALE = HEAD_DIM ** -0.5
LOG2E = math.log2(math.e)
N_BUCKETS = 32
MAX_DISTANCE = 128
HG_DK = 128
RMS_EPS = 1e-6
DECAY_GUARD = 75.0

LANES = 128
SUBLANES = 8
VMEM_LIMIT_BYTES = 56 * 1024 * 1024
PROMPT_TILE = 4 * CHUNK
HGRN_BLOCK = 2 * CHUNK
FFN_TILE = 512

_NT = (((1,), (1,)), ((), ()))
_TN = (((0,), (0,)), ((), ()))


def _bf(x):
    return x.astype(jnp.bfloat16)


def _dot(a, b, dims=None):
    if dims is None:
        return jnp.dot(a, b, preferred_element_type=jnp.float32)
    return lax.dot_general(a, b, dims, preferred_element_type=jnp.float32)


def _rms(x, w):
    return x * lax.rsqrt(jnp.mean(x * x, axis=-1, keepdims=True) + RMS_EPS) * w


def _t5_bucket(rel):
    nb = N_BUCKETS // 2
    max_exact = nb // 2
    ret = jnp.where(rel > 0, nb, 0)
    n = jnp.abs(rel)
    nf = jnp.maximum(n, 1).astype(jnp.float32)
    large = max_exact + (jnp.log(nf / max_exact) / math.log(MAX_DISTANCE / max_exact)
                         * (nb - max_exact)).astype(jnp.int32)
    large = jnp.minimum(large, nb - 1)
    return ret + jnp.where(n < max_exact, n, large)


def _levels(c):
    out, m = [], c // 2
    while m >= 1:
        out.append(m)
        m //= 2
    return out


def _level_masks(c):
    t = np.arange(c)[:, None]
    s = np.arange(c)[None, :]
    masks = []
    for m in _levels(c):
        masks.append((t // (2 * m) == s // (2 * m)) & ((t // m) % 2 == 1) & ((s // m) % 2 == 0))
    masks += [t == s, s <= t]
    return np.stack(masks).astype(np.float32)


def _row_masks(c):
    r = np.arange(c)
    rows = [(r // m) % 2 == 1 for m in _levels(c) if m < 8]
    rows += [r % 4 == 0, r % 4 >= 2, r % 4 == 3]
    return np.repeat(np.stack(rows).astype(np.float32)[:, :, None], LANES, axis=2)


def _block_cumsum(x, block):
    r, n = x.shape
    tiles = x.reshape(r // SUBLANES, SUBLANES, n)
    row = lax.broadcasted_iota(jnp.int32, (1, SUBLANES, n), 1)
    for shift in (1, 2, 4):
        tiles = tiles + jnp.where(row >= shift, pltpu.roll(tiles, shift, axis=1), 0.0)
    out, before = [], None
    for i in range(r // SUBLANES):
        t = tiles[i] if before is None or i % (block // SUBLANES) == 0 else tiles[i] + before
        out.append(t)
        before = jnp.broadcast_to(t[SUBLANES - 1:SUBLANES, :], (SUBLANES, n))
    return jnp.concatenate(out, axis=0)


def _head_variants(a):
    lo = lax.broadcasted_iota(jnp.int32, a.shape, 1) < HEAD_DIM
    rot = pltpu.roll(a, HEAD_DIM, axis=1)
    return [_bf(jnp.where(lo, a, 0.0)), _bf(jnp.where(lo, 0.0, rot)),
            _bf(jnp.where(lo, rot, 0.0)), _bf(jnp.where(lo, 0.0, a))]


def _mixer_kernel(*refs, tile, chunk, hg_block, carry, d_model, layer):
    n_chunks = tile // chunk
    keys = WINDOW + chunk
    key_pad = 2 * LANES
    assert keys < key_pad and tile >= WINDOW
    n_hg = d_model // HG_DK
    att_w = N_HEADS * HEAD_DIM
    n_pairs = N_HEADS // 2
    ppk = n_pairs // N_KV_HEADS
    rows_g = ppk * chunk
    n_var = WINDOW // chunk + 1 if carry else 1
    f32 = jnp.float32

    refs = list(refs)
    (x_ref, nw_ref, win_ref, wout_ref, sinks_ref, relb_ref, lbp_ref, gn_ref,
     lmask_ref, hmask_ref) = refs[:10]
    refs = refs[10:]
    if not carry:
        hk_ref, hv_ref, si_ref = refs[:3]
        refs = refs[3:]
    y_ref, ko_ref, vo_ref, so_ref = refs[:4]
    refs = refs[4:]
    (bias_s, q4_s, kv_s, kx_s, vx_s, hq_s, kk_s, hi_s, lf_s, bc_s, og_s, ga_s, gh_s, att_s, hg_s,
     u_s, sb_s, p_s, a_s, qe_s) = refs[:20]
    if carry:
        st_s = refs[20]

    first = (pl.program_id(0) == 0) & (pl.program_id(1) == 0)
    t_idx = pl.program_id(1)

    @pl.when(first)
    def _build_bias():
        lane = lax.broadcasted_iota(jnp.int32, (chunk, key_pad), 1)
        for head in range(N_HEADS):
            acc = jnp.where(lane == keys, sinks_ref[head], jnp.where(lane > keys, -jnp.inf, relb_ref[head])) * LOG2E
            pair, side = divmod(head, 2)
            kvh, j = divmod(pair, ppk)
            for var in range(n_var):
                n_invalid = WINDOW - var * chunk if carry else 0
                bias_s[var, 2 * kvh + side, j * chunk:(j + 1) * chunk, :] = jnp.where(lane < n_invalid, -jnp.inf, acc)

    if carry:
        @pl.when(t_idx == 0)
        def _reset():
            kx_s[:, 0, 0:WINDOW, :] = jnp.zeros((4, WINDOW, LANES), jnp.bfloat16)
            vx_s[:, 0, 0:WINDOW, :] = jnp.zeros((4, WINDOW, LANES), jnp.bfloat16)
            st_s[...] = jnp.zeros_like(st_s)

        @pl.when(t_idx > 0)
        def _shift():
            kx_s[:, 0, 0:WINDOW, :] = kx_s[:, 0, tile:tile + WINDOW, :]
            vx_s[:, 0, 0:WINDOW, :] = vx_s[:, 0, tile:tile + WINDOW, :]

    x = x_ref[0]
    h = _bf(_rms(x, nw_ref[...]))

    def proj(i0, width):
        return _dot(h, win_ref[:, i0:i0 + width])

    off = 0
    q = _bf(proj(off, att_w) * (ATTN_SCALE * LOG2E))
    for c in range(n_chunks):
        for j in range(n_pairs):
            q4_s[c, j] = q[c * chunk:(c + 1) * chunk, j * LANES:(j + 1) * LANES]
    off += att_w
    kv = proj(off, 2 * KV_WIDTH)
    kv_s[...] = kv
    off += 2 * KV_WIDTH
    k_var = _head_variants(kv[:, 0:KV_WIDTH])
    v_var = _head_variants(kv[:, KV_WIDTH:2 * KV_WIDTH])
    if carry:
        for g in range(4):
            kx_s[g, 0, WINDOW:WINDOW + tile, :] = k_var[g]
            vx_s[g, 0, WINDOW:WINDOW + tile, :] = v_var[g]
    else:
        for c in range(n_chunks):
            hk = _head_variants(hk_ref[c])
            hv = _head_variants(hv_ref[c])
            for g in range(4):
                kx_s[g, c, 0:WINDOW, :] = hk[g]
                vx_s[g, c, 0:WINDOW, :] = hv[g]
                kx_s[g, c, WINDOW:keys, :] = k_var[g][c * chunk:(c + 1) * chunk]
                vx_s[g, c, WINDOW:keys, :] = v_var[g][c * chunk:(c + 1) * chunk]
    off_q, off_f, off_i, off_og, off_ga, off_gh = (off + i * d_model for i in range(6))
    lbp = lbp_ref[...]
    e = jnp.exp(lbp - jnp.max(lbp, axis=0, keepdims=True))
    lb = jnp.sum(e[:layer + 1], axis=0, keepdims=True) / jnp.sum(e, axis=0, keepdims=True)

    def gate_quarter(i):
        cs = slice(i * d_model // 4, (i + 1) * d_model // 4)
        f = lb[:, cs] + (1.0 - lb[:, cs]) * jax.nn.sigmoid(proj(off_f + cs.start, d_model // 4))
        kk_s[:, cs] = 1.0 - f
        lf = jnp.log(f)
        lf_s[:, cs] = lf
        bc_s[:, cs] = _block_cumsum(lf, hg_block)

    gate_quarter(0)
    hq_s[...] = proj(off_q, d_model)
    gate_quarter(1)
    hi_s[...] = _bf(proj(off_i, d_model))
    gate_quarter(2)
    og_s[...] = jax.nn.silu(proj(off_og, d_model))
    gate_quarter(3)
    ga_s[...] = jax.nn.sigmoid(proj(off_ga, d_model))
    gh_s[...] = jax.nn.sigmoid(proj(off_gh, d_model))

    zpad = jnp.zeros((key_pad - keys, LANES), jnp.bfloat16)

    def window(ref, g, c):
        w = ref[g, 0, c * chunk:c * chunk + keys, :] if carry else ref[g, c]
        return jnp.concatenate([w, zpad], axis=0)

    for c in range(n_chunks):
        var = jnp.minimum(t_idx * n_chunks + c, n_var - 1) if carry else 0
        for kvh in range(N_KV_HEADS):
            qg = q4_s[c, kvh * ppk:(kvh + 1) * ppk].reshape(rows_g, LANES)
            for side in range(2):
                g = 2 * kvh + side
                s = _dot(qg, window(kx_s, g, c), _NT) + bias_s[var, g]
                p_s[c, g] = _bf(jnp.exp2(s - jnp.max(s, axis=-1, keepdims=True)))
    key_row = lax.broadcasted_iota(jnp.int32, (key_pad, LANES), 0)
    key_lane = lax.broadcasted_iota(jnp.int32, (key_pad, LANES), 1)
    ones_lo = _bf(jnp.where((key_row <= keys) & (key_lane < HEAD_DIM), 1.0, 0.0))
    ones_hi = _bf(jnp.where((key_row <= keys) & (key_lane >= HEAD_DIM), 1.0, 0.0))
    for c in range(n_chunks):
        for kvh in range(N_KV_HEADS):
            nd = (_dot(p_s[c, 2 * kvh], jnp.concatenate([window(vx_s, 2 * kvh, c), ones_lo], axis=1))
                  + _dot(p_s[c, 2 * kvh + 1], jnp.concatenate([window(vx_s, 2 * kvh + 1, c), ones_hi], axis=1)))
            o = nd[:, 0:LANES] * (1.0 / nd[:, LANES:2 * LANES])
            for j in range(ppk):
                col = (kvh * ppk + j) * LANES
                att_s[c * chunk:(c + 1) * chunk, col:col + LANES] = o[j * chunk:(j + 1) * chunk]

    blk, n_blk = hg_block, tile // hg_block
    levels = _levels(blk)
    gn = gn_ref[...]
    small = [m for m in levels if m < 8]

    def unit(c, hd):
        return slice(c * blk, (c + 1) * blk), slice(hd * HG_DK, (hd + 1) * HG_DK)

    def bc_row(c, r, cols):
        return bc_s[c * blk + r:c * blk + r + 1, cols]

    def level_operand(m, c, hd):
        rows, cols = unit(c, hd)
        q = hq_s[rows, cols]
        kk = kk_s[rows, cols]
        if m >= 8:
            bc = bc_s[rows, cols]
            parts = []
            for base in range(0, blk, 2 * m):
                ref = bc_row(c, base + m - 1, cols)
                lo, hi = slice(base, base + m), slice(base + m, base + 2 * m)
                parts += [kk[lo] * jnp.exp(ref - bc[lo]), q[hi] * jnp.exp(bc[hi] - ref)]
            return _bf(jnp.concatenate(parts, axis=0))
        upper = hmask_ref[small.index(m)] > 0.5
        if m == 1:
            return _bf(jnp.where(upper, q * (1.0 - kk), kk))
        if m == 4:
            bc = bc_s[rows, cols]
            ref_row = bc.reshape(blk // 8, 8, HG_DK)[:, 3:4, :]
            ref = jnp.broadcast_to(ref_row, (blk // 8, 8, HG_DK)).reshape(blk, HG_DK)
            arg = -jnp.abs(bc - ref)
        else:
            lfh = lf_s[rows, cols]
            nxt = pltpu.roll(lfh, blk - 1, axis=0)
            prv = pltpu.roll(lfh, 1, axis=0)
            n_small = len(small)
            arg = hmask_ref[n_small] * nxt + hmask_ref[n_small + 1] * lfh + hmask_ref[n_small + 2] * prv
        return _bf(jnp.where(upper, q, kk) * jnp.exp(arg))

    for c in range(n_blk):
        for hd in range(n_hg):
            rows, cols = unit(c, hd)
            qe_s[rows, cols] = _bf(hq_s[rows, cols] * jnp.exp(bc_s[rows, cols]))

    tril = lmask_ref[len(levels) + 1] > 0.5
    for c in range(n_blk):
        for hd in range(n_hg):
            rows, cols = unit(c, hd)
            k_grow_t = _bf((kk_s[rows, cols] * jnp.exp(-bc_s[rows, cols])).T)
            a_s[c, hd] = _bf(jnp.where(tril, _dot(qe_s[rows, cols], k_grow_t), 0.0))

    for c in range(n_blk):
        for hd in range(n_hg):
            rows, cols = unit(c, hd)
            k_dec = _bf(kk_s[rows, cols] * jnp.exp(bc_row(c, blk - 1, cols) - bc_s[rows, cols]))
            u_s[c, hd] = _dot(hi_s[rows, cols], k_dec, _TN)

    def readout(c):
        for hd in range(n_hg):
            rows, cols = unit(c, hd)
            o = _dot(qe_s[rows, cols], sb_s[c, hd]) + _dot(a_s[c, hd], hi_s[rows, cols])
            o = o * lax.rsqrt(jnp.mean(o * o, axis=-1, keepdims=True) + RMS_EPS) * gn
            hg_s[rows, cols] = o * og_s[rows, cols]

    def merge_and_project():
        merged = _bf(ga_s[...] * att_s[...] + gh_s[...] * hg_s[...])
        y_ref[0] = x_ref[0] + _dot(merged, wout_ref[...])

    if carry:
        states = [st_s[hd] for hd in range(n_hg)]
    for c in range(n_blk):
        for hd in range(n_hg):
            sb_s[c, hd] = _bf(states[hd].T) if carry else _bf(si_ref[c, hd])
        readout(c)
        for hd in range(n_hg):
            decay = jnp.exp(bc_row(c, blk - 1, unit(c, hd)[1]))
            if carry:
                states[hd] = states[hd] * decay + u_s[c, hd]
            else:
                so_ref[c, hd] = (si_ref[c, hd].T * decay + u_s[c, hd]).T
    if carry:
        for hd in range(n_hg):
            st_s[hd] = states[hd]
    merge_and_project()

    total_decay = jnp.concatenate([bc_row(c, blk - 1, slice(None)) for c in range(n_blk)], axis=0)

    @pl.when(jnp.logical_not(jnp.max(-total_decay) <= DECAY_GUARD))
    def _split():
        for c in range(n_blk):
            for hd in range(n_hg):
                rows, cols = unit(c, hd)
                a = _dot(_bf(hq_s[rows, cols]), _bf(kk_s[rows, cols]), _NT) * lmask_ref[len(levels)]
                for li, m in enumerate(levels):
                    z = level_operand(m, c, hd)
                    a = a + _dot(z, z, _NT) * lmask_ref[li]
                a_s[c, hd] = _bf(a)
        for c in range(n_blk):
            readout(c)
        merge_and_project()

    if carry:
        @pl.when(t_idx == pl.num_programs(1) - 1)
        def _emit():
            ko_ref[0] = kv_s[tile - WINDOW:tile, 0:KV_WIDTH]
            vo_ref[0] = kv_s[tile - WINDOW:tile, KV_WIDTH:2 * KV_WIDTH]
            for hd in range(n_hg):
                so_ref[0, hd] = st_s[hd].T
    else:
        ko_ref[0] = kv_s[:, 0:KV_WIDTH]
        vo_ref[0] = kv_s[:, KV_WIDTH:2 * KV_WIDTH]


def _full_spec(shape):
    nd = len(shape)
    return pl.BlockSpec(shape, lambda *_: (0,) * nd)


def _weight_spec(shape):
    nd = len(shape)
    return pl.BlockSpec(shape, lambda *_: (0,) * nd, pipeline_mode=pl.Buffered(1))


def _mixer(x, hist, norm_w, w_in, w_out, sinks, table, lb_params, gnorm_w, *, layer, carry, tile, chunk,
           hg_block):
    nb, s, d = x.shape
    n_t = s // tile
    n_chunks = tile // chunk
    n_blk = tile // hg_block
    assert carry or hg_block == chunk
    keys = WINDOW + chunk
    key_pad = 2 * LANES
    n_hg = d // HG_DK
    n_streams = nb if carry else s // chunk
    n_off = chunk + key_pad - 1
    offsets = jnp.arange(n_off, dtype=jnp.int32) - (chunk - 1) - WINDOW
    by_offset = table[_t5_bucket(offsets)].astype(jnp.float32).T
    period = jnp.roll(jnp.pad(by_offset, ((0, 0), (0, 1))), -(chunk - 1), axis=1)
    skewed = jnp.tile(period, (1, chunk))[:, :chunk * n_off].reshape(N_HEADS, chunk, n_off)
    rel_bias = skewed[:, :, :key_pad]
    lmask = jnp.asarray(_level_masks(hg_block))
    hmask = jnp.asarray(_row_masks(hg_block))
    smem = pl.BlockSpec(memory_space=pltpu.SMEM)

    in_arrays = [x, norm_w.reshape(1, d), w_in, w_out, sinks, rel_bias, lb_params,
                 gnorm_w.reshape(1, HG_DK), lmask, hmask]
    in_specs = [pl.BlockSpec((1, tile, d), lambda b, t: (b, t, 0)), _full_spec((1, d)),
                _weight_spec(w_in.shape), _weight_spec(w_out.shape), smem, _full_spec(rel_bias.shape),
                _full_spec(lb_params.shape), _full_spec((1, HG_DK)),
                _full_spec(lmask.shape), _full_spec(hmask.shape)]
    if carry:
        kv_rows = WINDOW
        st_block = (1, n_hg, HG_DK, HG_DK)
        n_win, win_rows, n_var = 1, WINDOW + tile, WINDOW // chunk + 1
    else:
        in_arrays += list(hist)
        in_specs += [_full_spec(a.shape) for a in hist]
        kv_rows = s
        st_block = (n_streams, n_hg, HG_DK, HG_DK)
        n_win, win_rows, n_var = n_chunks, keys, 1
    out_shape = [jax.ShapeDtypeStruct((nb, s, d), jnp.float32),
                 jax.ShapeDtypeStruct((nb, kv_rows, KV_WIDTH), jnp.float32),
                 jax.ShapeDtypeStruct((nb, kv_rows, KV_WIDTH), jnp.float32),
                 jax.ShapeDtypeStruct((n_streams, n_hg, HG_DK, HG_DK), jnp.float32)]
    out_specs = [pl.BlockSpec((1, tile, d), lambda b, t: (b, t, 0)),
                 pl.BlockSpec((1, kv_rows, KV_WIDTH), lambda b, t: (b, 0, 0)),
                 pl.BlockSpec((1, kv_rows, KV_WIDTH), lambda b, t: (b, 0, 0)),
                 pl.BlockSpec(st_block, lambda b, t: (b, 0, 0, 0))]
    f32, bf16 = jnp.float32, jnp.bfloat16
    rows_g = (N_HEADS // 2 // N_KV_HEADS) * chunk
    scratch = [pltpu.VMEM((n_var, 4, rows_g, key_pad), f32),
               pltpu.VMEM((n_chunks, N_HEADS // 2, chunk, LANES), bf16),
               pltpu.VMEM((tile, 2 * KV_WIDTH), f32),
               pltpu.VMEM((4, n_win, win_rows, LANES), bf16),
               pltpu.VMEM((4, n_win, win_rows, LANES), bf16),
               pltpu.VMEM((tile, d), f32), pltpu.VMEM((tile, d), f32),
               pltpu.VMEM((tile, d), bf16)]
    scratch += [pltpu.VMEM((tile, d), f32) for _ in range(7)]
    scratch += [pltpu.VMEM((n_blk, n_hg, HG_DK, HG_DK), f32),
                pltpu.VMEM((n_blk, n_hg, HG_DK, HG_DK), bf16),
                pltpu.VMEM((n_chunks, 4, rows_g, key_pad), bf16),
                pltpu.VMEM((n_blk, n_hg, hg_block, hg_block), bf16),
                pltpu.VMEM((tile, d), bf16)]
    if carry:
        scratch += [pltpu.VMEM((n_hg, HG_DK, HG_DK), f32)]
    kern = functools.partial(_mixer_kernel, tile=tile, chunk=chunk, hg_block=hg_block, carry=carry, d_model=d,
                             layer=layer)
    return pl.pallas_call(
        kern,
        grid=(nb, n_t),
        in_specs=in_specs,
        out_specs=out_specs,
        out_shape=out_shape,
        scratch_shapes=scratch,
        compiler_params=pltpu.CompilerParams(
            dimension_semantics=("arbitrary", "arbitrary"), vmem_limit_bytes=VMEM_LIMIT_BYTES),
        name="mixer_prompt" if carry else "mixer_sample",
    )(*in_arrays)


def _ffn_kernel(xa_ref, xb_ref, nw_ref, wgu_ref, wd_ref, fw_ref, ya_ref, yb_ref, act_s, *, d_ff, col_tile,
                final_norm):
    n_a = pl.num_programs(0) - 1

    def ffn(x_ref, y_ref):
        rows = x_ref.shape[0]
        x = x_ref[...]
        h = _bf(_rms(x, nw_ref[...]))
        for j in range(d_ff // col_tile):
            g = _dot(h, _bf(wgu_ref[:, j * col_tile:(j + 1) * col_tile]))
            u = _dot(h, _bf(wgu_ref[:, d_ff + j * col_tile:d_ff + (j + 1) * col_tile]))
            act_s[0:rows, j * col_tile:(j + 1) * col_tile] = _bf(jax.nn.silu(g) * u)
        y = x + _dot(act_s[0:rows, :], _bf(wd_ref[...]))
        if final_norm:
            y = _rms(y, fw_ref[...])
        y_ref[...] = y

    @pl.when(pl.program_id(0) < n_a)
    def _first_set():
        ffn(xa_ref, ya_ref)

    @pl.when(pl.program_id(0) == n_a)
    def _second_set():
        ffn(xb_ref, yb_ref)


def _ffn(xa, xb, norm_w, w_gate_up, w_down, final_w, *, tile, final_norm):
    n, d = xa.shape
    m = xb.shape[0]
    n_a = n // tile
    assert n == n_a * tile and m <= tile
    d_ff = w_down.shape[0]
    kern = functools.partial(_ffn_kernel, d_ff=d_ff, col_tile=2 * LANES, final_norm=final_norm)
    a_spec = pl.BlockSpec((tile, d), lambda i: (jnp.minimum(i, n_a - 1), 0))
    return pl.pallas_call(
        kern,
        grid=(n_a + 1,),
        in_specs=[a_spec, _full_spec((m, d)), _full_spec((1, d)),
                  _weight_spec(w_gate_up.shape), _weight_spec(w_down.shape), _full_spec((1, d))],
        out_specs=[a_spec, _full_spec((m, d))],
        out_shape=[jax.ShapeDtypeStruct((n, d), jnp.float32), jax.ShapeDtypeStruct((m, d), jnp.float32)],
        scratch_shapes=[pltpu.VMEM((tile, d_ff), jnp.bfloat16)],
        compiler_params=pltpu.CompilerParams(
            dimension_semantics=("arbitrary",), vmem_limit_bytes=VMEM_LIMIT_BYTES),
        name="ffn",
    )(xa, xb, norm_w.reshape(1, d), w_gate_up, w_down, final_w.reshape(1, d))


def kernel(x_prompt, x_sample, cache_k, cache_v, state_hgrn, norm_mix, w_in, w_out, attn_sinks, rel_bias_table,
           hgrn_lb, hgrn_norm, norm_ffn, w_gate_up, w_down, norm_final):
    depth = w_in.shape[0]
    batch, seq, d = x_prompt.shape
    dec_batch, dec_seq, _ = x_sample.shape
    assert w_in.shape[2] == N_HEADS * HEAD_DIM + 2 * KV_WIDTH + 6 * d and KV_WIDTH == LANES and d % (4 * LANES) == 0
    assert seq % PROMPT_TILE == 0 and (batch * seq) % FFN_TILE == 0 and dec_batch * dec_seq <= FFN_TILE
    assert dec_seq % (2 * SUBLANES) == 0 and cache_k.shape[2] == WINDOW
    xp = x_prompt
    xs = x_sample.reshape(1, dec_batch * dec_seq, d)
    outs = [[] for _ in range(6)]
    for l in range(depth):
        w_in_l, w_out_l = _bf(w_in[l]), _bf(w_out[l])
        shared = (norm_mix[l], w_in_l, w_out_l, attn_sinks[l], rel_bias_table, hgrn_lb, hgrn_norm[l])
        xp, kp, vp, sp = _mixer(xp, None, *shared, layer=l, carry=True, tile=PROMPT_TILE, chunk=CHUNK,
                                hg_block=HGRN_BLOCK)
        hist = (cache_k[l].reshape(dec_batch, WINDOW, KV_WIDTH), cache_v[l].reshape(dec_batch, WINDOW, KV_WIDTH),
                state_hgrn[l])
        xs, ks, vs, ss = _mixer(xs, hist, *shared, layer=l, carry=False, tile=dec_batch * dec_seq, chunk=dec_seq,
                                hg_block=dec_seq)
        last = l == depth - 1
        xp, xs = _ffn(xp.reshape(batch * seq, d), xs.reshape(dec_batch * dec_seq, d), norm_ffn[l], w_gate_up[l],
                      w_down[l], norm_final, tile=FFN_TILE, final_norm=last)
        xp = xp.reshape(batch, seq, d)
        xs = xs.reshape(1, dec_batch * dec_seq, d)
        ks = jnp.concatenate([hist[0][:, dec_seq:], ks.reshape(dec_batch, dec_seq, KV_WIDTH)], axis=1)
        vs = jnp.concatenate([hist[1][:, dec_seq:], vs.reshape(dec_batch, dec_seq, KV_WIDTH)], axis=1)
        kv_shape = (-1, WINDOW, N_KV_HEADS, HEAD_DIM)
        for acc, val in zip(outs, (kp.reshape(kv_shape), vp.reshape(kv_shape), sp,
                                   ks.reshape(kv_shape), vs.reshape(kv_shape), ss)):
            acc.append(val)
    return (xp, xs.reshape(dec_batch, dec_seq, d)) + tuple(jnp.stack(o) for o in outs)
```

```python
import functools
import math

import numpy as np
import jax
import jax.numpy as jnp
from jax import lax
from jax.experimental import pallas as pl
from jax.experimental.pallas import tpu as pltpu

CHUNK = 64
N_HEADS = 16
N_KV_HEADS = 2
HEAD_DIM = 64
KV_WIDTH = N_KV_HEADS * HEAD_DIM
WINDOW = 128
ATTN_SCALE = HEAD_DIM ** -0.5
LOG2E = math.log2(math.e)
N_BUCKETS = 32
MAX_DISTANCE = 128
HG_DK = 128
RMS_EPS = 1e-6
DECAY_GUARD = 75.0

LANES = 128
SUBLANES = 8
VMEM_LIMIT_BYTES = 56 * 1024 * 1024
PROMPT_TILE = 4 * CHUNK
HGRN_BLOCK = 2 * CHUNK
FFN_TILE = 512

_NT = (((1,), (1,)), ((), ()))
_TN = (((0,), (0,)), ((), ()))


def _bf(x):
    return x.astype(jnp.bfloat16)


def _dot(a, b, dims=None):
    if dims is None:
        return jnp.dot(a, b, preferred_element_type=jnp.float32)
    return lax.dot_general(a, b, dims, preferred_element_type=jnp.float32)


def _sigmoid(x):
    return 0.5 * jnp.tanh(0.5 * x) + 0.5


def _rms(x, w):
    return x * lax.rsqrt(jnp.mean(x * x, axis=-1, keepdims=True) + RMS_EPS) * w


def _t5_bucket(rel):
    nb = N_BUCKETS // 2
    max_exact = nb // 2
    ret = jnp.where(rel > 0, nb, 0)
    n = jnp.abs(rel)
    nf = jnp.maximum(n, 1).astype(jnp.float32)
    large = max_exact + (jnp.log(nf / max_exact) / math.log(MAX_DISTANCE / max_exact)
                         * (nb - max_exact)).astype(jnp.int32)
    large = jnp.minimum(large, nb - 1)
    return ret + jnp.where(n < max_exact, n, large)


def _levels(c):
    out, m = [], c // 2
    while m >= 1:
        out.append(m)
        m //= 2
    return out


def _level_masks(c):
    t = np.arange(c)[:, None]
    s = np.arange(c)[None, :]
    masks = []
    for m in _levels(c):
        masks.append((t // (2 * m) == s // (2 * m)) & ((t // m) % 2 == 1) & ((s // m) % 2 == 0))
    masks += [t == s, s <= t]
    return np.stack(masks).astype(np.float32)


def _row_masks(c):
    r = np.arange(c)
    rows = [(r // m) % 2 == 1 for m in _levels(c) if m < 8]
    rows += [r % 4 == 0, r % 4 >= 2, r % 4 == 3]
    return np.repeat(np.stack(rows).astype(np.float32)[:, :, None], LANES, axis=2)


def _block_cumsum(x, block):
    r, n = x.shape
    tiles = x.reshape(r // SUBLANES, SUBLANES, n)
    row = lax.broadcasted_iota(jnp.int32, (1, SUBLANES, n), 1)
    for shift in (1, 2, 4):
        tiles = tiles + jnp.where(row >= shift, pltpu.roll(tiles, shift, axis=1), 0.0)
    out, before = [], None
    for i in range(r // SUBLANES):
        t = tiles[i] if before is None or i % (block // SUBLANES) == 0 else tiles[i] + before
        out.append(t)
        before = jnp.broadcast_to(t[SUBLANES - 1:SUBLANES, :], (SUBLANES, n))
    return jnp.concatenate(out, axis=0)


def _head_variants(a):
    lo = lax.broadcasted_iota(jnp.int32, a.shape, 1) < HEAD_DIM
    rot = pltpu.roll(a, HEAD_DIM, axis=1)
    return [_bf(jnp.where(lo, a, 0.0)), _bf(jnp.where(lo, 0.0, rot)),
            _bf(jnp.where(lo, rot, 0.0)), _bf(jnp.where(lo, 0.0, a))]


def _mixer_kernel(*refs, tile, chunk, hg_block, carry, d_model, layer):
    n_chunks = tile // chunk
    keys = WINDOW + chunk
    key_pad = 2 * LANES
    assert keys < key_pad and tile >= WINDOW
    n_hg = d_model // HG_DK
    att_w = N_HEADS * HEAD_DIM
    n_pairs = N_HEADS // 2
    ppk = n_pairs // N_KV_HEADS
    rows_g = ppk * chunk
    n_var = WINDOW // chunk + 1 if carry else 1
    f32 = jnp.float32

    refs = list(refs)
    (x_ref, nw_ref, win_ref, wout_ref, sinks_ref, relb_ref, lbp_ref, gn_ref,
     lmask_ref, hmask_ref) = refs[:10]
    refs = refs[10:]
    if not carry:
        hk_ref, hv_ref, si_ref = refs[:3]
        refs = refs[3:]
    y_ref, ko_ref, vo_ref, so_ref = refs[:4]
    refs = refs[4:]
    (bias_s, q4_s, kv_s, kx_s, vx_s, hq_s, kk_s, hi_s, lf_s, bc_s, og_s, ga_s, gh_s, att_s, hg_s,
     u_s, sb_s, p_s, a_s, qe_s) = refs[:20]
    if carry:
        st_s = refs[20]

    first = (pl.program_id(0) == 0) & (pl.program_id(1) == 0)
    t_idx = pl.program_id(1)

    @pl.when(first)
    def _build_bias():
        lane = lax.broadcasted_iota(jnp.int32, (chunk, key_pad), 1)
        for head in range(N_HEADS):
            acc = jnp.where(lane == keys, sinks_ref[head], jnp.where(lane > keys, -jnp.inf, relb_ref[head])) * LOG2E
            pair, side = divmod(head, 2)
            kvh, j = divmod(pair, ppk)
            for var in range(n_var):
                n_invalid = WINDOW - var * chunk if carry else 0
                bias_s[var, 2 * kvh + side, j * chunk:(j + 1) * chunk, :] = jnp.where(lane < n_invalid, -jnp.inf, acc)

    if carry:
        @pl.when(t_idx == 0)
        def _reset():
            kx_s[:, 0, 0:WINDOW, :] = jnp.zeros((4, WINDOW, LANES), jnp.bfloat16)
            vx_s[:, 0, 0:WINDOW, :] = jnp.zeros((4, WINDOW, LANES), jnp.bfloat16)
            st_s[...] = jnp.zeros_like(st_s)

        @pl.when(t_idx > 0)
        def _shift():
            kx_s[:, 0, 0:WINDOW, :] = kx_s[:, 0, tile:tile + WINDOW, :]
            vx_s[:, 0, 0:WINDOW, :] = vx_s[:, 0, tile:tile + WINDOW, :]

    x = x_ref[0]
    h = _bf(_rms(x, nw_ref[...]))

    def proj(i0, width):
        return _dot(h, win_ref[:, i0:i0 + width])

    def project_queries():
        q = _bf(proj(0, att_w) * (ATTN_SCALE * LOG2E))
        for c in range(n_chunks):
            for j in range(n_pairs):
                q4_s[c, j] = q[c * chunk:(c + 1) * chunk, j * LANES:(j + 1) * LANES]

    off = att_w
    kv = proj(off, 2 * KV_WIDTH)
    kv_s[...] = kv
    off += 2 * KV_WIDTH
    k_var = _head_variants(kv[:, 0:KV_WIDTH])
    v_var = _head_variants(kv[:, KV_WIDTH:2 * KV_WIDTH])
    if carry:
        for g in range(4):
            kx_s[g, 0, WINDOW:WINDOW + tile, :] = k_var[g]
            vx_s[g, 0, WINDOW:WINDOW + tile, :] = v_var[g]
    else:
        for c in range(n_chunks):
            hk = _head_variants(hk_ref[c])
            hv = _head_variants(hv_ref[c])
            for g in range(4):
                kx_s[g, c, 0:WINDOW, :] = hk[g]
                vx_s[g, c, 0:WINDOW, :] = hv[g]
                kx_s[g, c, WINDOW:keys, :] = k_var[g][c * chunk:(c + 1) * chunk]
                vx_s[g, c, WINDOW:keys, :] = v_var[g][c * chunk:(c + 1) * chunk]
    off_q, off_f, off_i, off_og, off_ga, off_gh = (off + i * d_model for i in range(6))
    lbp = lbp_ref[...]
    e = jnp.exp(lbp - jnp.max(lbp, axis=0, keepdims=True))
    lb = jnp.sum(e[:layer + 1], axis=0, keepdims=True) / jnp.sum(e, axis=0, keepdims=True)

    def gate_quarter(i):
        cs = slice(i * d_model // 4, (i + 1) * d_model // 4)
        half_span = 0.5 * (1.0 - lb[:, cs])
        f = (lb[:, cs] + half_span) + half_span * jnp.tanh(0.5 * proj(off_f + cs.start, d_model // 4))
        kk_s[:, cs] = 1.0 - f
        lf = jnp.log(f)
        lf_s[:, cs] = lf
        bc_s[:, cs] = _block_cumsum(lf, hg_block)

    hq_s[...] = proj(off_q, d_model)
    gate_quarter(0)
    hi_s[...] = _bf(proj(off_i, d_model))
    gate_quarter(1)
    project_queries()
    gate_quarter(2)
    y_og = proj(off_og, d_model)
    og_s[...] = y_og * _sigmoid(y_og)
    gate_quarter(3)
    ga_s[...] = _sigmoid(proj(off_ga, d_model))
    gh_s[...] = _sigmoid(proj(off_gh, d_model))

    zpad = jnp.zeros((key_pad - keys, LANES), jnp.bfloat16)

    def window(ref, g, c):
        w = ref[g, 0, c * chunk:c * chunk + keys, :] if carry else ref[g, c]
        return jnp.concatenate([w, zpad], axis=0)

    for c in range(n_chunks):
        var = jnp.minimum(t_idx * n_chunks + c, n_var - 1) if carry else 0
        for kvh in range(N_KV_HEADS):
            qg = q4_s[c, kvh * ppk:(kvh + 1) * ppk].reshape(rows_g, LANES)
            for side in range(2):
                g = 2 * kvh + side
                s = _dot(qg, window(kx_s, g, c), _NT) + bias_s[var, g]
                p_s[c, g] = _bf(jnp.exp2(s - jnp.max(s, axis=-1, keepdims=True)))
    key_row = lax.broadcasted_iota(jnp.int32, (key_pad, LANES), 0)
    key_lane = lax.broadcasted_iota(jnp.int32, (key_pad, LANES), 1)
    ones_lo = _bf(jnp.where((key_row <= keys) & (key_lane < HEAD_DIM), 1.0, 0.0))
    ones_hi = _bf(jnp.where((key_row <= keys) & (key_lane >= HEAD_DIM), 1.0, 0.0))
    for c in range(n_chunks):
        for kvh in range(N_KV_HEADS):
            nd = (_dot(p_s[c, 2 * kvh], jnp.concatenate([window(vx_s, 2 * kvh, c), ones_lo], axis=1))
                  + _dot(p_s[c, 2 * kvh + 1], jnp.concatenate([window(vx_s, 2 * kvh + 1, c), ones_hi], axis=1)))
            o = nd[:, 0:LANES] * (1.0 / nd[:, LANES:2 * LANES])
            for j in range(ppk):
                col = (kvh * ppk + j) * LANES
                att_s[c * chunk:(c + 1) * chunk, col:col + LANES] = o[j * chunk:(j + 1) * chunk]

    blk, n_blk = hg_block, tile // hg_block
    levels = _levels(blk)
    gn = gn_ref[...]
    small = [m for m in levels if m < 8]

    def unit(c, hd):
        return slice(c * blk, (c + 1) * blk), slice(hd * HG_DK, (hd + 1) * HG_DK)

    def bc_row(c, r, cols):
        return bc_s[c * blk + r:c * blk + r + 1, cols]

    def level_operand(m, c, hd):
        rows, cols = unit(c, hd)
        q = hq_s[rows, cols]
        kk = kk_s[rows, cols]
        if m >= 8:
            bc = bc_s[rows, cols]
            parts = []
            for base in range(0, blk, 2 * m):
                ref = bc_row(c, base + m - 1, cols)
                lo, hi = slice(base, base + m), slice(base + m, base + 2 * m)
                parts += [kk[lo] * jnp.exp(ref - bc[lo]), q[hi] * jnp.exp(bc[hi] - ref)]
            return _bf(jnp.concatenate(parts, axis=0))
        upper = hmask_ref[small.index(m)] > 0.5
        if m == 1:
            return _bf(jnp.where(upper, q * (1.0 - kk), kk))
        if m == 4:
            bc = bc_s[rows, cols]
            ref_row = bc.reshape(blk // 8, 8, HG_DK)[:, 3:4, :]
            ref = jnp.broadcast_to(ref_row, (blk // 8, 8, HG_DK)).reshape(blk, HG_DK)
            arg = -jnp.abs(bc - ref)
        else:
            lfh = lf_s[rows, cols]
            nxt = pltpu.roll(lfh, blk - 1, axis=0)
            prv = pltpu.roll(lfh, 1, axis=0)
            n_small = len(small)
            arg = hmask_ref[n_small] * nxt + hmask_ref[n_small + 1] * lfh + hmask_ref[n_small + 2] * prv
        return _bf(jnp.where(upper, q, kk) * jnp.exp(arg))

    for c in range(n_blk):
        for hd in range(n_hg):
            rows, cols = unit(c, hd)
            qe_s[rows, cols] = _bf(hq_s[rows, cols] * jnp.exp(bc_s[rows, cols]))

    tril = lmask_ref[len(levels) + 1] > 0.5
    for c in range(n_blk):
        for hd in range(n_hg):
            rows, cols = unit(c, hd)
            k_grow_t = _bf((kk_s[rows, cols] * jnp.exp(-bc_s[rows, cols])).T)
            a_s[c, hd] = _bf(jnp.where(tril, _dot(qe_s[rows, cols], k_grow_t), 0.0))

    for c in range(n_blk):
        for hd in range(n_hg):
            rows, cols = unit(c, hd)
            k_dec = _bf(kk_s[rows, cols] * jnp.exp(bc_row(c, blk - 1, cols) - bc_s[rows, cols]))
            u_s[c, hd] = _dot(hi_s[rows, cols], k_dec, _TN)

    def readout(c):
        for hd in range(n_hg):
            rows, cols = unit(c, hd)
            o = _dot(qe_s[rows, cols], sb_s[c, hd]) + _dot(a_s[c, hd], hi_s[rows, cols])
            o = o * lax.rsqrt(jnp.mean(o * o, axis=-1, keepdims=True) + RMS_EPS) * gn
            hg_s[rows, cols] = o * og_s[rows, cols]

    def merge_and_project():
        merged = _bf(ga_s[...] * att_s[...] + gh_s[...] * hg_s[...])
        y_ref[0] = x_ref[0] + _dot(merged, wout_ref[...])

    if carry:
        states = [st_s[hd] for hd in range(n_hg)]
    for c in range(n_blk):
        for hd in range(n_hg):
            sb_s[c, hd] = _bf(states[hd].T) if carry else _bf(si_ref[c, hd])
        readout(c)
        for hd in range(n_hg):
            decay = jnp.exp(bc_row(c, blk - 1, unit(c, hd)[1]))
            if carry:
                states[hd] = states[hd] * decay + u_s[c, hd]
            else:
                so_ref[c, hd] = (si_ref[c, hd].T * decay + u_s[c, hd]).T
    if carry:
        for hd in range(n_hg):
            st_s[hd] = states[hd]
    merge_and_project()

    total_decay = jnp.concatenate([bc_row(c, blk - 1, slice(None)) for c in range(n_blk)], axis=0)

    @pl.when(jnp.logical_not(jnp.max(-total_decay) <= DECAY_GUARD))
    def _split():
        for c in range(n_blk):
            for hd in range(n_hg):
                rows, cols = unit(c, hd)
                a = _dot(_bf(hq_s[rows, cols]), _bf(kk_s[rows, cols]), _NT) * lmask_ref[len(levels)]
                for li, m in enumerate(levels):
                    z = level_operand(m, c, hd)
                    a = a + _dot(z, z, _NT) * lmask_ref[li]
                a_s[c, hd] = _bf(a)
        for c in range(n_blk):
            readout(c)
        merge_and_project()

    if carry:
        @pl.when(t_idx == pl.num_programs(1) - 1)
        def _emit():
            ko_ref[0] = kv_s[tile - WINDOW:tile, 0:KV_WIDTH]
            vo_ref[0] = kv_s[tile - WINDOW:tile, KV_WIDTH:2 * KV_WIDTH]
            for hd in range(n_hg):
                so_ref[0, hd] = st_s[hd].T
    else:
        ko_ref[0] = kv_s[:, 0:KV_WIDTH]
        vo_ref[0] = kv_s[:, KV_WIDTH:2 * KV_WIDTH]


def _full_spec(shape):
    nd = len(shape)
    return pl.BlockSpec(shape, lambda *_: (0,) * nd)


def _weight_spec(shape):
    nd = len(shape)
    return pl.BlockSpec(shape, lambda *_: (0,) * nd, pipeline_mode=pl.Buffered(1))


def _mixer(x, hist, norm_w, w_in, w_out, sinks, table, lb_params, gnorm_w, *, layer, carry, tile, chunk,
           hg_block):
    nb, s, d = x.shape
    n_t = s // tile
    n_chunks = tile // chunk
    n_blk = tile // hg_block
    assert carry or hg_block == chunk
    keys = WINDOW + chunk
    key_pad = 2 * LANES
    n_hg = d // HG_DK
    n_streams = nb if carry else s // chunk
    n_off = chunk + key_pad - 1
    offsets = jnp.arange(n_off, dtype=jnp.int32) - (chunk - 1) - WINDOW
    by_offset = table[_t5_bucket(offsets)].astype(jnp.float32).T
    period = jnp.roll(jnp.pad(by_offset, ((0, 0), (0, 1))), -(chunk - 1), axis=1)
    skewed = jnp.tile(period, (1, chunk))[:, :chunk * n_off].reshape(N_HEADS, chunk, n_off)
    rel_bias = skewed[:, :, :key_pad]
    lmask = jnp.asarray(_level_masks(hg_block))
    hmask = jnp.asarray(_row_masks(hg_block))
    smem = pl.BlockSpec(memory_space=pltpu.SMEM)

    in_arrays = [x, norm_w.reshape(1, d), w_in, w_out, sinks, rel_bias, lb_params,
                 gnorm_w.reshape(1, HG_DK), lmask, hmask]
    in_specs = [pl.BlockSpec((1, tile, d), lambda b, t: (b, t, 0)), _full_spec((1, d)),
                _weight_spec(w_in.shape), _weight_spec(w_out.shape), smem, _full_spec(rel_bias.shape),
                _full_spec(lb_params.shape), _full_spec((1, HG_DK)),
                _full_spec(lmask.shape), _full_spec(hmask.shape)]
    if carry:
        kv_rows = WINDOW
        st_block = (1, n_hg, HG_DK, HG_DK)
        n_win, win_rows, n_var = 1, WINDOW + tile, WINDOW // chunk + 1
    else:
        in_arrays += list(hist)
        in_specs += [_full_spec(a.shape) for a in hist]
        kv_rows = s
        st_block = (n_streams, n_hg, HG_DK, HG_DK)
        n_win, win_rows, n_var = n_chunks, keys, 1
    out_shape = [jax.ShapeDtypeStruct((nb, s, d), jnp.float32),
                 jax.ShapeDtypeStruct((nb, kv_rows, KV_WIDTH), jnp.float32),
                 jax.ShapeDtypeStruct((nb, kv_rows, KV_WIDTH), jnp.float32),
                 jax.ShapeDtypeStruct((n_streams, n_hg, HG_DK, HG_DK), jnp.float32)]
    out_specs = [pl.BlockSpec((1, tile, d), lambda b, t: (b, t, 0)),
                 pl.BlockSpec((1, kv_rows, KV_WIDTH), lambda b, t: (b, 0, 0)),
                 pl.BlockSpec((1, kv_rows, KV_WIDTH), lambda b, t: (b, 0, 0)),
                 pl.BlockSpec(st_block, lambda b, t: (b, 0, 0, 0))]
    f32, bf16 = jnp.float32, jnp.bfloat16
    rows_g = (N_HEADS // 2 // N_KV_HEADS) * chunk
    scratch = [pltpu.VMEM((n_var, 4, rows_g, key_pad), f32),
               pltpu.VMEM((n_chunks, N_HEADS // 2, chunk, LANES), bf16),
               pltpu.VMEM((tile, 2 * KV_WIDTH), f32),
               pltpu.VMEM((4, n_win, win_rows, LANES), bf16),
               pltpu.VMEM((4, n_win, win_rows, LANES), bf16),
               pltpu.VMEM((tile, d), f32), pltpu.VMEM((tile, d), f32),
               pltpu.VMEM((tile, d), bf16)]
    scratch += [pltpu.VMEM((tile, d), f32) for _ in range(7)]
    scratch += [pltpu.VMEM((n_blk, n_hg, HG_DK, HG_DK), f32),
                pltpu.VMEM((n_blk, n_hg, HG_DK, HG_DK), bf16),
                pltpu.VMEM((n_chunks, 4, rows_g, key_pad), bf16),
                pltpu.VMEM((n_blk, n_hg, hg_block, hg_block), bf16),
                pltpu.VMEM((tile, d), bf16)]
    if carry:
        scratch += [pltpu.VMEM((n_hg, HG_DK, HG_DK), f32)]
    kern = functools.partial(_mixer_kernel, tile=tile, chunk=chunk, hg_block=hg_block, carry=carry, d_model=d,
                             layer=layer)
    return pl.pallas_call(
        kern,
        grid=(nb, n_t),
        in_specs=in_specs,
        out_specs=out_specs,
        out_shape=out_shape,
        scratch_shapes=scratch,
        compiler_params=pltpu.CompilerParams(
            dimension_semantics=("arbitrary", "arbitrary"), vmem_limit_bytes=VMEM_LIMIT_BYTES),
        name="mixer_prompt" if carry else "mixer_sample",
    )(*in_arrays)


def _ffn_kernel(xa_ref, xb_ref, nw_ref, wgu_ref, wd_ref, fw_ref, ya_ref, yb_ref, act_s, *, d_ff, col_tile,
                final_norm):
    n_a = pl.num_programs(0) - 1

    def ffn(x_ref, y_ref):
        rows = x_ref.shape[0]
        x = x_ref[...]
        h = _bf(_rms(x, nw_ref[...]))
        for j in range(d_ff // col_tile):
            g = _dot(h, _bf(wgu_ref[:, j * col_tile:(j + 1) * col_tile]))
            u = _dot(h, _bf(wgu_ref[:, d_ff + j * col_tile:d_ff + (j + 1) * col_tile]))
            act_s[0:rows, j * col_tile:(j + 1) * col_tile] = _bf(jax.nn.silu(g) * u)
        y = x + _dot(act_s[0:rows, :], _bf(wd_ref[...]))
        if final_norm:
            y = _rms(y, fw_ref[...])
        y_ref[...] = y

    @pl.when(pl.program_id(0) < n_a)
    def _first_set():
        ffn(xa_ref, ya_ref)

    @pl.when(pl.program_id(0) == n_a)
    def _second_set():
        ffn(xb_ref, yb_ref)


def _ffn(xa, xb, norm_w, w_gate_up, w_down, final_w, *, tile, final_norm):
    n, d = xa.shape
    m = xb.shape[0]
    n_a = n // tile
    assert n == n_a * tile and m <= tile
    d_ff = w_down.shape[0]
    kern = functools.partial(_ffn_kernel, d_ff=d_ff, col_tile=2 * LANES, final_norm=final_norm)
    a_spec = pl.BlockSpec((tile, d), lambda i: (jnp.minimum(i, n_a - 1), 0))
    return pl.pallas_call(
        kern,
        grid=(n_a + 1,),
        in_specs=[a_spec, _full_spec((m, d)), _full_spec((1, d)),
                  _weight_spec(w_gate_up.shape), _weight_spec(w_down.shape), _full_spec((1, d))],
        out_specs=[a_spec, _full_spec((m, d))],
        out_shape=[jax.ShapeDtypeStruct((n, d), jnp.float32), jax.ShapeDtypeStruct((m, d), jnp.float32)],
        scratch_shapes=[pltpu.VMEM((tile, d_ff), jnp.bfloat16)],
        compiler_params=pltpu.CompilerParams(
            dimension_semantics=("arbitrary",), vmem_limit_bytes=VMEM_LIMIT_BYTES),
        name="ffn",
    )(xa, xb, norm_w.reshape(1, d), w_gate_up, w_down, final_w.reshape(1, d))


def kernel(x_prompt, x_sample, cache_k, cache_v, state_hgrn, norm_mix, w_in, w_out, attn_sinks, rel_bias_table,
           hgrn_lb, hgrn_norm, norm_ffn, w_gate_up, w_down, norm_final):
    depth = w_in.shape[0]
    batch, seq, d = x_prompt.shape
    dec_batch, dec_seq, _ = x_sample.shape
    assert w_in.shape[2] == N_HEADS * HEAD_DIM + 2 * KV_WIDTH + 6 * d and KV_WIDTH == LANES and d % (4 * LANES) == 0
    assert seq % PROMPT_TILE == 0 and (batch * seq) % FFN_TILE == 0 and dec_batch * dec_seq <= FFN_TILE
    assert dec_seq % (2 * SUBLANES) == 0 and cache_k.shape[2] == WINDOW
    xp = x_prompt
    xs = x_sample.reshape(1, dec_batch * dec_seq, d)
    outs = [[] for _ in range(6)]
    for l in range(depth):
        w_in_l, w_out_l = _bf(w_in[l]), _bf(w_out[l])
        shared = (norm_mix[l], w_in_l, w_out_l, attn_sinks[l], rel_bias_table, hgrn_lb, hgrn_norm[l])
        xp, kp, vp, sp = _mixer(xp, None, *shared, layer=l, carry=True, tile=PROMPT_TILE, chunk=CHUNK,
                                hg_block=HGRN_BLOCK)
        hist = (cache_k[l].reshape(dec_batch, WINDOW, KV_WIDTH), cache_v[l].reshape(dec_batch, WINDOW, KV_WIDTH),
                state_hgrn[l])
        xs, ks, vs, ss = _mixer(xs, hist, *shared, layer=l, carry=False, tile=dec_batch * dec_seq, chunk=dec_seq,
                                hg_block=dec_seq)
        last = l == depth - 1
        xp, xs = _ffn(xp.reshape(batch * seq, d), xs.reshape(dec_batch * dec_seq, d), norm_ffn[l], w_gate_up[l],
                      w_down[l], norm_final, tile=FFN_TILE, final_norm=last)
        xp = xp.reshape(batch, seq, d)
        xs = xs.reshape(1, dec_batch * dec_seq, d)
        ks = jnp.concatenate([hist[0][:, dec_seq:], ks.reshape(dec_batch, dec_seq, KV_WIDTH)], axis=1)
        vs = jnp.concatenate([hist[1][:, dec_seq:], vs.reshape(dec_batch, dec_seq, KV_WIDTH)], axis=1)
        kv_shape = (-1, WINDOW, N_KV_HEADS, HEAD_DIM)
        for acc, val in zip(outs, (kp.reshape(kv_shape), vp.reshape(kv_shape), sp,
                                   ks.reshape(kv_shape), vs.reshape(kv_shape), ss)):
            acc.append(val)
    return (xp, xs.reshape(dec_batch, dec_seq, d)) + tuple(jnp.stack(o) for o in outs)
```

```python
import functools
import math

import numpy as np
import jax
import jax.numpy as jnp
from jax import lax
from jax.experimental import pallas as pl
from jax.experimental.pallas import tpu as pltpu

CHUNK = 64
N_HEADS = 16
N_KV_HEADS = 2
HEAD_DIM = 64
KV_WIDTH = N_KV_HEADS * HEAD_DIM
WINDOW = 128
ATTN_SCALE = HEAD_DIM ** -0.5
LOG2E = math.log2(math.e)
N_BUCKETS = 32
MAX_DISTANCE = 128
HG_DK = 128
RMS_EPS = 1e-6
DECAY_GUARD = 75.0

LANES = 128
SUBLANES = 8
VMEM_LIMIT_BYTES = 56 * 1024 * 1024
PROMPT_TILE = 4 * CHUNK
HGRN_BLOCK = 2 * CHUNK
FFN_TILE = 512

_NT = (((1,), (1,)), ((), ()))
_TN = (((0,), (0,)), ((), ()))


def _bf(x):
    return x.astype(jnp.bfloat16)


def _dot(a, b, dims=None):
    if dims is None:
        return jnp.dot(a, b, preferred_element_type=jnp.float32)
    return lax.dot_general(a, b, dims, preferred_element_type=jnp.float32)


def _sigmoid(x):
    return 0.5 * jnp.tanh(0.5 * x) + 0.5


def _rms(x, w):
    return x * lax.rsqrt(jnp.mean(x * x, axis=-1, keepdims=True) + RMS_EPS) * w


def _t5_bucket(rel):
    nb = N_BUCKETS // 2
    max_exact = nb // 2
    ret = jnp.where(rel > 0, nb, 0)
    n = jnp.abs(rel)
    nf = jnp.maximum(n, 1).astype(jnp.float32)
    large = max_exact + (jnp.log(nf / max_exact) / math.log(MAX_DISTANCE / max_exact)
                         * (nb - max_exact)).astype(jnp.int32)
    large = jnp.minimum(large, nb - 1)
    return ret + jnp.where(n < max_exact, n, large)


def _levels(c):
    out, m = [], c // 2
    while m >= 1:
        out.append(m)
        m //= 2
    return out


def _level_masks(c):
    t = np.arange(c)[:, None]
    s = np.arange(c)[None, :]
    masks = []
    for m in _levels(c):
        masks.append((t // (2 * m) == s // (2 * m)) & ((t // m) % 2 == 1) & ((s // m) % 2 == 0))
    masks += [t == s, s <= t]
    return np.stack(masks).astype(np.float32)


def _row_masks(c):
    r = np.arange(c)
    rows = [(r // m) % 2 == 1 for m in _levels(c) if m < 8]
    rows += [r % 4 == 0, r % 4 >= 2, r % 4 == 3]
    return np.repeat(np.stack(rows).astype(np.float32)[:, :, None], LANES, axis=2)


def _block_cumsum(x, block):
    r, n = x.shape
    tiles = x.reshape(r // SUBLANES, SUBLANES, n)
    row = lax.broadcasted_iota(jnp.int32, (1, SUBLANES, n), 1)
    for shift in (1, 2, 4):
        tiles = tiles + jnp.where(row >= shift, pltpu.roll(tiles, shift, axis=1), 0.0)
    out, before = [], None
    for i in range(r // SUBLANES):
        t = tiles[i] if before is None or i % (block // SUBLANES) == 0 else tiles[i] + before
        out.append(t)
        before = jnp.broadcast_to(t[SUBLANES - 1:SUBLANES, :], (SUBLANES, n))
    return jnp.concatenate(out, axis=0)


def _head_variants(a):
    lo = lax.broadcasted_iota(jnp.int32, a.shape, 1) < HEAD_DIM
    rot = pltpu.roll(a, HEAD_DIM, axis=1)
    return [_bf(jnp.where(lo, a, 0.0)), _bf(jnp.where(lo, 0.0, rot)),
            _bf(jnp.where(lo, rot, 0.0)), _bf(jnp.where(lo, 0.0, a))]


def _mixer_kernel(*refs, tile, chunk, hg_block, carry, d_model, layer):
    n_chunks = tile // chunk
    keys = WINDOW + chunk
    key_pad = 2 * LANES
    assert keys < key_pad and tile >= WINDOW
    n_hg = d_model // HG_DK
    att_w = N_HEADS * HEAD_DIM
    n_pairs = N_HEADS // 2
    ppk = n_pairs // N_KV_HEADS
    rows_g = ppk * chunk
    n_var = WINDOW // chunk + 1 if carry else 1
    f32 = jnp.float32

    refs = list(refs)
    (x_ref, nw_ref, win_ref, wout_ref, sinks_ref, relb_ref, lbp_ref, gn_ref,
     lmask_ref, hmask_ref) = refs[:10]
    refs = refs[10:]
    if not carry:
        hk_ref, hv_ref, si_ref = refs[:3]
        refs = refs[3:]
    y_ref, ko_ref, vo_ref, so_ref = refs[:4]
    refs = refs[4:]
    (bias_s, q4_s, kv_s, kx_s, vx_s, hq_s, kk_s, hi_s, bc_s, og_s, ga_s, gh_s, att_s, hg_s,
     u_s, sb_s, p_s, a_s, qe_s) = refs[:19]
    if carry:
        st_s = refs[19]

    first = (pl.program_id(0) == 0) & (pl.program_id(1) == 0)
    t_idx = pl.program_id(1)

    @pl.when(first)
    def _build_bias():
        lane = lax.broadcasted_iota(jnp.int32, (chunk, key_pad), 1)
        for head in range(N_HEADS):
            acc = jnp.where(lane == keys, sinks_ref[head], jnp.where(lane > keys, -jnp.inf, relb_ref[head])) * LOG2E
            pair, side = divmod(head, 2)
            kvh, j = divmod(pair, ppk)
            for var in range(n_var):
                n_invalid = WINDOW - var * chunk if carry else 0
                bias_s[var, 2 * kvh + side, j * chunk:(j + 1) * chunk, :] = jnp.where(lane < n_invalid, -jnp.inf, acc)

    if carry:
        @pl.when(t_idx == 0)
        def _reset():
            kx_s[:, 0, 0:WINDOW, :] = jnp.zeros((4, WINDOW, LANES), jnp.bfloat16)
            vx_s[:, 0, 0:WINDOW, :] = jnp.zeros((4, WINDOW, LANES), jnp.bfloat16)
            st_s[...] = jnp.zeros_like(st_s)

        @pl.when(t_idx > 0)
        def _shift():
            kx_s[:, 0, 0:WINDOW, :] = kx_s[:, 0, tile:tile + WINDOW, :]
            vx_s[:, 0, 0:WINDOW, :] = vx_s[:, 0, tile:tile + WINDOW, :]

    x = x_ref[0]
    h = _bf(_rms(x, nw_ref[...]))

    def proj(i0, width):
        return _dot(h, win_ref[:, i0:i0 + width])

    def project_queries():
        q = _bf(proj(0, att_w) * (ATTN_SCALE * LOG2E))
        for c in range(n_chunks):
            for j in range(n_pairs):
                q4_s[c, j] = q[c * chunk:(c + 1) * chunk, j * LANES:(j + 1) * LANES]

    off = att_w
    kv = proj(off, 2 * KV_WIDTH)
    kv_s[...] = kv
    off += 2 * KV_WIDTH
    k_var = _head_variants(kv[:, 0:KV_WIDTH])
    v_var = _head_variants(kv[:, KV_WIDTH:2 * KV_WIDTH])
    if carry:
        for g in range(4):
            kx_s[g, 0, WINDOW:WINDOW + tile, :] = k_var[g]
            vx_s[g, 0, WINDOW:WINDOW + tile, :] = v_var[g]
    else:
        for c in range(n_chunks):
            hk = _head_variants(hk_ref[c])
            hv = _head_variants(hv_ref[c])
            for g in range(4):
                kx_s[g, c, 0:WINDOW, :] = hk[g]
                vx_s[g, c, 0:WINDOW, :] = hv[g]
                kx_s[g, c, WINDOW:keys, :] = k_var[g][c * chunk:(c + 1) * chunk]
                vx_s[g, c, WINDOW:keys, :] = v_var[g][c * chunk:(c + 1) * chunk]
    off_hq, off_f, off_i, off_og, off_ga, off_gh = (off + i * d_model for i in range(6))
    lbp = lbp_ref[...]
    e = jnp.exp(lbp - jnp.max(lbp, axis=0, keepdims=True))
    lb = jnp.sum(e[:layer + 1], axis=0, keepdims=True) / jnp.sum(e, axis=0, keepdims=True)

    def gate_quarter(i):
        cs = slice(i * d_model // 4, (i + 1) * d_model // 4)
        half_span = 0.5 * (1.0 - lb[:, cs])
        f = (lb[:, cs] + half_span) + half_span * jnp.tanh(0.5 * proj(off_f + cs.start, d_model // 4))
        kk_s[:, cs] = 1.0 - f
        bc_s[:, cs] = _block_cumsum(jnp.log(f), hg_block)

    hq_s[...] = proj(off_hq, d_model)
    gate_quarter(0)
    hi_s[...] = _bf(proj(off_i, d_model))
    gate_quarter(1)
    project_queries()
    gate_quarter(2)
    y_og = proj(off_og, d_model)
    og_s[...] = y_og * _sigmoid(y_og)
    gate_quarter(3)
    ga_s[...] = _sigmoid(proj(off_ga, d_model))
    gh_s[...] = _sigmoid(proj(off_gh, d_model))

    zpad = jnp.zeros((key_pad - keys, LANES), jnp.bfloat16)

    def window(ref, g, c):
        w = ref[g, 0, c * chunk:c * chunk + keys, :] if carry else ref[g, c]
        return jnp.concatenate([w, zpad], axis=0)

    for c in range(n_chunks):
        var = jnp.minimum(t_idx * n_chunks + c, n_var - 1) if carry else 0
        for kvh in range(N_KV_HEADS):
            qg = q4_s[c, kvh * ppk:(kvh + 1) * ppk].reshape(rows_g, LANES)
            for side in range(2):
                g = 2 * kvh + side
                s = _dot(qg, window(kx_s, g, c), _NT) + bias_s[var, g]
                p_s[c, g] = _bf(jnp.exp2(s - jnp.max(s, axis=-1, keepdims=True)))
    key_row = lax.broadcasted_iota(jnp.int32, (key_pad, LANES), 0)
    key_lane = lax.broadcasted_iota(jnp.int32, (key_pad, LANES), 1)
    ones_lo = _bf(jnp.where((key_row <= keys) & (key_lane < HEAD_DIM), 1.0, 0.0))
    ones_hi = _bf(jnp.where((key_row <= keys) & (key_lane >= HEAD_DIM), 1.0, 0.0))
    for c in range(n_chunks):
        for kvh in range(N_KV_HEADS):
            nd = (_dot(p_s[c, 2 * kvh], jnp.concatenate([window(vx_s, 2 * kvh, c), ones_lo], axis=1))
                  + _dot(p_s[c, 2 * kvh + 1], jnp.concatenate([window(vx_s, 2 * kvh + 1, c), ones_hi], axis=1)))
            o = nd[:, 0:LANES] * (1.0 / nd[:, LANES:2 * LANES])
            for j in range(ppk):
                col = (kvh * ppk + j) * LANES
                att_s[c * chunk:(c + 1) * chunk, col:col + LANES] = o[j * chunk:(j + 1) * chunk]

    blk, n_blk = hg_block, tile // hg_block
    levels = _levels(blk)
    gn = gn_ref[...]
    small = [m for m in levels if m < 8]

    def unit(c, hd):
        return slice(c * blk, (c + 1) * blk), slice(hd * HG_DK, (hd + 1) * HG_DK)

    def bc_row(c, r, cols):
        return bc_s[c * blk + r:c * blk + r + 1, cols]

    def level_operand(m, c, hd):
        rows, cols = unit(c, hd)
        q = hq_s[rows, cols]
        kk = kk_s[rows, cols]
        if m >= 8:
            bc = bc_s[rows, cols]
            parts = []
            for base in range(0, blk, 2 * m):
                ref = bc_row(c, base + m - 1, cols)
                lo, hi = slice(base, base + m), slice(base + m, base + 2 * m)
                parts += [kk[lo] * jnp.exp(ref - bc[lo]), q[hi] * jnp.exp(bc[hi] - ref)]
            return _bf(jnp.concatenate(parts, axis=0))
        upper = hmask_ref[small.index(m)] > 0.5
        if m == 1:
            return _bf(jnp.where(upper, q * (1.0 - kk), kk))
        if m == 4:
            bc = bc_s[rows, cols]
            ref_row = bc.reshape(blk // 8, 8, HG_DK)[:, 3:4, :]
            ref = jnp.broadcast_to(ref_row, (blk // 8, 8, HG_DK)).reshape(blk, HG_DK)
            arg = -jnp.abs(bc - ref)
        else:
            lfh = jnp.log(1.0 - kk)
            nxt = pltpu.roll(lfh, blk - 1, axis=0)
            prv = pltpu.roll(lfh, 1, axis=0)
            n_small = len(small)
            arg = hmask_ref[n_small] * nxt + hmask_ref[n_small + 1] * lfh + hmask_ref[n_small + 2] * prv
        return _bf(jnp.where(upper, q, kk) * jnp.exp(arg))

    tril = lmask_ref[len(levels) + 1] > 0.5
    for c in range(n_blk):
        for hd in range(n_hg):
            rows, cols = unit(c, hd)
            decay_in = jnp.exp(bc_s[rows, cols])
            qe = _bf(hq_s[rows, cols] * decay_in)
            qe_s[rows, cols] = qe
            k_grow_t = _bf((kk_s[rows, cols] * (1.0 / decay_in)).T)
            a_s[c, hd] = _bf(jnp.where(tril, _dot(qe, k_grow_t), 0.0))

    for c in range(n_blk):
        for hd in range(n_hg):
            rows, cols = unit(c, hd)
            k_dec = _bf(kk_s[rows, cols] * jnp.exp(bc_row(c, blk - 1, cols) - bc_s[rows, cols]))
            u_s[c, hd] = _dot(hi_s[rows, cols], k_dec, _TN)

    def readout(c):
        for hd in range(n_hg):
            rows, cols = unit(c, hd)
            o = _dot(qe_s[rows, cols], sb_s[c, hd]) + _dot(a_s[c, hd], hi_s[rows, cols])
            o = o * lax.rsqrt(jnp.mean(o * o, axis=-1, keepdims=True) + RMS_EPS) * gn
            hg_s[rows, cols] = o * og_s[rows, cols]

    def merge_and_project():
        merged = _bf(ga_s[...] * att_s[...] + gh_s[...] * hg_s[...])
        y_ref[0] = x_ref[0] + _dot(merged, wout_ref[...])

    if carry:
        states = [st_s[hd] for hd in range(n_hg)]
    for c in range(n_blk):
        for hd in range(n_hg):
            sb_s[c, hd] = _bf(states[hd].T) if carry else _bf(si_ref[c, hd])
        readout(c)
        for hd in range(n_hg):
            decay = jnp.exp(bc_row(c, blk - 1, unit(c, hd)[1]))
            if carry:
                states[hd] = states[hd] * decay + u_s[c, hd]
            else:
                so_ref[c, hd] = (si_ref[c, hd].T * decay + u_s[c, hd]).T
    if carry:
        for hd in range(n_hg):
            st_s[hd] = states[hd]
    merge_and_project()

    total_decay = jnp.concatenate([bc_row(c, blk - 1, slice(None)) for c in range(n_blk)], axis=0)

    @pl.when(jnp.logical_not(jnp.max(-total_decay) <= DECAY_GUARD))
    def _split():
        for c in range(n_blk):
            for hd in range(n_hg):
                rows, cols = unit(c, hd)
                a = _dot(_bf(hq_s[rows, cols]), _bf(kk_s[rows, cols]), _NT) * lmask_ref[len(levels)]
                for li, m in enumerate(levels):
                    z = level_operand(m, c, hd)
                    a = a + _dot(z, z, _NT) * lmask_ref[li]
                a_s[c, hd] = _bf(a)
        for c in range(n_blk):
            readout(c)
        merge_and_project()

    if carry:
        @pl.when(t_idx == pl.num_programs(1) - 1)
        def _emit():
            ko_ref[0] = kv_s[tile - WINDOW:tile, 0:KV_WIDTH]
            vo_ref[0] = kv_s[tile - WINDOW:tile, KV_WIDTH:2 * KV_WIDTH]
            for hd in range(n_hg):
                so_ref[0, hd] = st_s[hd].T
    else:
        ko_ref[0] = kv_s[:, 0:KV_WIDTH]
        vo_ref[0] = kv_s[:, KV_WIDTH:2 * KV_WIDTH]


def _full_spec(shape):
    nd = len(shape)
    return pl.BlockSpec(shape, lambda *_: (0,) * nd)


def _weight_spec(shape):
    nd = len(shape)
    return pl.BlockSpec(shape, lambda *_: (0,) * nd, pipeline_mode=pl.Buffered(1))


def _mixer(x, hist, norm_w, w_in, w_out, sinks, table, lb_params, gnorm_w, *, layer, carry, tile, chunk,
           hg_block):
    nb, s, d = x.shape
    n_t = s // tile
    n_chunks = tile // chunk
    n_blk = tile // hg_block
    assert carry or hg_block == chunk
    keys = WINDOW + chunk
    key_pad = 2 * LANES
    n_hg = d // HG_DK
    n_streams = nb if carry else s // chunk
    n_off = chunk + key_pad - 1
    offsets = jnp.arange(n_off, dtype=jnp.int32) - (chunk - 1) - WINDOW
    by_offset = table[_t5_bucket(offsets)].astype(jnp.float32).T
    period = jnp.roll(jnp.pad(by_offset, ((0, 0), (0, 1))), -(chunk - 1), axis=1)
    skewed = jnp.tile(period, (1, chunk))[:, :chunk * n_off].reshape(N_HEADS, chunk, n_off)
    rel_bias = skewed[:, :, :key_pad]
    lmask = jnp.asarray(_level_masks(hg_block))
    hmask = jnp.asarray(_row_masks(hg_block))
    smem = pl.BlockSpec(memory_space=pltpu.SMEM)

    in_arrays = [x, norm_w.reshape(1, d), w_in, w_out, sinks, rel_bias, lb_params,
                 gnorm_w.reshape(1, HG_DK), lmask, hmask]
    in_specs = [pl.BlockSpec((1, tile, d), lambda b, t: (b, t, 0)), _full_spec((1, d)),
                _weight_spec(w_in.shape), _weight_spec(w_out.shape), smem, _full_spec(rel_bias.shape),
                _full_spec(lb_params.shape), _full_spec((1, HG_DK)),
                _full_spec(lmask.shape), _full_spec(hmask.shape)]
    if carry:
        kv_rows = WINDOW
        st_block = (1, n_hg, HG_DK, HG_DK)
        n_win, win_rows, n_var = 1, WINDOW + tile, WINDOW // chunk + 1
    else:
        in_arrays += list(hist)
        in_specs += [_full_spec(a.shape) for a in hist]
        kv_rows = s
        st_block = (n_streams, n_hg, HG_DK, HG_DK)
        n_win, win_rows, n_var = n_chunks, keys, 1
    out_shape = [jax.ShapeDtypeStruct((nb, s, d), jnp.float32),
                 jax.ShapeDtypeStruct((nb, kv_rows, KV_WIDTH), jnp.float32),
                 jax.ShapeDtypeStruct((nb, kv_rows, KV_WIDTH), jnp.float32),
                 jax.ShapeDtypeStruct((n_streams, n_hg, HG_DK, HG_DK), jnp.float32)]
    out_specs = [pl.BlockSpec((1, tile, d), lambda b, t: (b, t, 0)),
                 pl.BlockSpec((1, kv_rows, KV_WIDTH), lambda b, t: (b, 0, 0)),
                 pl.BlockSpec((1, kv_rows, KV_WIDTH), lambda b, t: (b, 0, 0)),
                 pl.BlockSpec(st_block, lambda b, t: (b, 0, 0, 0))]
    f32, bf16 = jnp.float32, jnp.bfloat16
    rows_g = (N_HEADS // 2 // N_KV_HEADS) * chunk
    scratch = [pltpu.VMEM((n_var, 4, rows_g, key_pad), f32),
               pltpu.VMEM((n_chunks, N_HEADS // 2, chunk, LANES), bf16),
               pltpu.VMEM((tile, 2 * KV_WIDTH), f32),
               pltpu.VMEM((4, n_win, win_rows, LANES), bf16),
               pltpu.VMEM((4, n_win, win_rows, LANES), bf16),
               pltpu.VMEM((tile, d), f32), pltpu.VMEM((tile, d), f32),
               pltpu.VMEM((tile, d), bf16)]
    scratch += [pltpu.VMEM((tile, d), f32) for _ in range(6)]
    scratch += [pltpu.VMEM((n_blk, n_hg, HG_DK, HG_DK), f32),
                pltpu.VMEM((n_blk, n_hg, HG_DK, HG_DK), bf16),
                pltpu.VMEM((n_chunks, 4, rows_g, key_pad), bf16),
                pltpu.VMEM((n_blk, n_hg, hg_block, hg_block), bf16),
                pltpu.VMEM((tile, d), bf16)]
    if carry:
        scratch += [pltpu.VMEM((n_hg, HG_DK, HG_DK), f32)]
    kern = functools.partial(_mixer_kernel, tile=tile, chunk=chunk, hg_block=hg_block, carry=carry, d_model=d,
                             layer=layer)
    return pl.pallas_call(
        kern,
        grid=(nb, n_t),
        in_specs=in_specs,
        out_specs=out_specs,
        out_shape=out_shape,
        scratch_shapes=scratch,
        compiler_params=pltpu.CompilerParams(
            dimension_semantics=("arbitrary", "arbitrary"), vmem_limit_bytes=VMEM_LIMIT_BYTES),
        name="mixer_prompt" if carry else "mixer_sample",
    )(*in_arrays)


def _ffn_kernel(xa_ref, xb_ref, nw_ref, wgu_ref, wd_ref, fw_ref, ya_ref, yb_ref, act_s, *, d_ff, col_tile,
                final_norm):
    n_a = pl.num_programs(0) - 1

    def ffn(x_ref, y_ref):
        rows = x_ref.shape[0]
        x = x_ref[...]
        h = _bf(_rms(x, nw_ref[...]))
        for j in range(d_ff // col_tile):
            g = _dot(h, _bf(wgu_ref[:, j * col_tile:(j + 1) * col_tile]))
            u = _dot(h, _bf(wgu_ref[:, d_ff + j * col_tile:d_ff + (j + 1) * col_tile]))
            act_s[0:rows, j * col_tile:(j + 1) * col_tile] = _bf(jax.nn.silu(g) * u)
        y = x + _dot(act_s[0:rows, :], _bf(wd_ref[...]))
        if final_norm:
            y = _rms(y, fw_ref[...])
        y_ref[...] = y

    @pl.when(pl.program_id(0) < n_a)
    def _first_set():
        ffn(xa_ref, ya_ref)

    @pl.when(pl.program_id(0) == n_a)
    def _second_set():
        ffn(xb_ref, yb_ref)


def _ffn(xa, xb, norm_w, w_gate_up, w_down, final_w, *, tile, final_norm):
    n, d = xa.shape
    m = xb.shape[0]
    n_a = n // tile
    assert n == n_a * tile and m <= tile
    d_ff = w_down.shape[0]
    kern = functools.partial(_ffn_kernel, d_ff=d_ff, col_tile=2 * LANES, final_norm=final_norm)
    a_spec = pl.BlockSpec((tile, d), lambda i: (jnp.minimum(i, n_a - 1), 0))
    return pl.pallas_call(
        kern,
        grid=(n_a + 1,),
        in_specs=[a_spec, _full_spec((m, d)), _full_spec((1, d)),
                  _weight_spec(w_gate_up.shape), _weight_spec(w_down.shape), _full_spec((1, d))],
        out_specs=[a_spec, _full_spec((m, d))],
        out_shape=[jax.ShapeDtypeStruct((n, d), jnp.float32), jax.ShapeDtypeStruct((m, d), jnp.float32)],
        scratch_shapes=[pltpu.VMEM((tile, d_ff), jnp.bfloat16)],
        compiler_params=pltpu.CompilerParams(
            dimension_semantics=("arbitrary",), vmem_limit_bytes=VMEM_LIMIT_BYTES),
        name="ffn",
    )(xa, xb, norm_w.reshape(1, d), w_gate_up, w_down, final_w.reshape(1, d))


def kernel(x_prompt, x_sample, cache_k, cache_v, state_hgrn, norm_mix, w_in, w_out, attn_sinks, rel_bias_table,
           hgrn_lb, hgrn_norm, norm_ffn, w_gate_up, w_down, norm_final):
    depth = w_in.shape[0]
    batch, seq, d = x_prompt.shape
    dec_batch, dec_seq, _ = x_sample.shape
    assert w_in.shape[2] == N_HEADS * HEAD_DIM + 2 * KV_WIDTH + 6 * d and KV_WIDTH == LANES and d % (4 * LANES) == 0
    assert seq % PROMPT_TILE == 0 and (batch * seq) % FFN_TILE == 0 and dec_batch * dec_seq <= FFN_TILE
    assert dec_seq % (2 * SUBLANES) == 0 and cache_k.shape[2] == WINDOW
    xp = x_prompt
    xs = x_sample.reshape(1, dec_batch * dec_seq, d)
    outs = [[] for _ in range(6)]
    for l in range(depth):
        w_in_l, w_out_l = _bf(w_in[l]), _bf(w_out[l])
        shared = (norm_mix[l], w_in_l, w_out_l, attn_sinks[l], rel_bias_table, hgrn_lb, hgrn_norm[l])
        xp, kp, vp, sp = _mixer(xp, None, *shared, layer=l, carry=True, tile=PROMPT_TILE, chunk=CHUNK,
                                hg_block=HGRN_BLOCK)
        hist = (cache_k[l].reshape(dec_batch, WINDOW, KV_WIDTH), cache_v[l].reshape(dec_batch, WINDOW, KV_WIDTH),
                state_hgrn[l])
        xs, ks, vs, ss = _mixer(xs, hist, *shared, layer=l, carry=False, tile=dec_batch * dec_seq, chunk=dec_seq,
                                hg_block=dec_seq)
        last = l == depth - 1
        xp, xs = _ffn(xp.reshape(batch * seq, d), xs.reshape(dec_batch * dec_seq, d), norm_ffn[l], w_gate_up[l],
                      w_down[l], norm_final, tile=FFN_TILE, final_norm=last)
        xp = xp.reshape(batch, seq, d)
        xs = xs.reshape(1, dec_batch * dec_seq, d)
        ks = jnp.concatenate([hist[0][:, dec_seq:], ks.reshape(dec_batch, dec_seq, KV_WIDTH)], axis=1)
        vs = jnp.concatenate([hist[1][:, dec_seq:], vs.reshape(dec_batch, dec_seq, KV_WIDTH)], axis=1)
        kv_shape = (-1, WINDOW, N_KV_HEADS, HEAD_DIM)
        for acc, val in zip(outs, (kp.reshape(kv_shape), vp.reshape(kv_shape), sp,
                                   ks.reshape(kv_shape), vs.reshape(kv_shape), ss)):
            acc.append(val)
    return (xp, xs.reshape(dec_batch, dec_seq, d)) + tuple(jnp.stack(o) for o in outs)
```

```python
import functools
import math

import numpy as np
import jax
import jax.numpy as jnp
from jax import lax
from jax.experimental import pallas as pl
from jax.experimental.pallas import tpu as pltpu

CHUNK = 64
N_HEADS = 16
N_KV_HEADS = 2
HEAD_DIM = 64
KV_WIDTH = N_KV_HEADS * HEAD_DIM
WINDOW = 128
ATTN_SCALE = HEAD_DIM ** -0.5
LOG2E = math.log2(math.e)
N_BUCKETS = 32
MAX_DISTANCE = 128
HG_DK = 128
RMS_EPS = 1e-6
DECAY_GUARD = 75.0

LANES = 128
SUBLANES = 8
VMEM_LIMIT_BYTES = 56 * 1024 * 1024
PROMPT_TILE = 4 * CHUNK
HGRN_BLOCK = 2 * CHUNK
FFN_TILE = 512

_NT = (((1,), (1,)), ((), ()))
_TN = (((0,), (0,)), ((), ()))


def _bf(x):
    return x.astype(jnp.bfloat16)


def _dot(a, b, dims=None):
    if dims is None:
        return jnp.dot(a, b, preferred_element_type=jnp.float32)
    return lax.dot_general(a, b, dims, preferred_element_type=jnp.float32)


def _sigmoid(x):
    return 0.5 * jnp.tanh(0.5 * x) + 0.5


def _rms(x, w):
    return x * lax.rsqrt(jnp.mean(x * x, axis=-1, keepdims=True) + RMS_EPS) * w


def _t5_bucket(rel):
    nb = N_BUCKETS // 2
    max_exact = nb // 2
    ret = jnp.where(rel > 0, nb, 0)
    n = jnp.abs(rel)
    nf = jnp.maximum(n, 1).astype(jnp.float32)
    large = max_exact + (jnp.log(nf / max_exact) / math.log(MAX_DISTANCE / max_exact)
                         * (nb - max_exact)).astype(jnp.int32)
    large = jnp.minimum(large, nb - 1)
    return ret + jnp.where(n < max_exact, n, large)


def _levels(c):
    out, m = [], c // 2
    while m >= 1:
        out.append(m)
        m //= 2
    return out


def _level_masks(c):
    t = np.arange(c)[:, None]
    s = np.arange(c)[None, :]
    masks = []
    for m in _levels(c):
        masks.append((t // (2 * m) == s // (2 * m)) & ((t // m) % 2 == 1) & ((s // m) % 2 == 0))
    masks += [t == s, s <= t]
    return np.stack(masks).astype(np.float32)


def _row_masks(c):
    r = np.arange(c)
    rows = [(r // m) % 2 == 1 for m in _levels(c) if m < 8]
    rows += [r % 4 == 0, r % 4 >= 2, r % 4 == 3]
    return np.repeat(np.stack(rows).astype(np.float32)[:, :, None], LANES, axis=2)


def _block_cumsum(x, block):
    r, n = x.shape
    tiles = x.reshape(r // SUBLANES, SUBLANES, n)
    row = lax.broadcasted_iota(jnp.int32, (1, SUBLANES, n), 1)
    for shift in (1, 2, 4):
        tiles = tiles + jnp.where(row >= shift, pltpu.roll(tiles, shift, axis=1), 0.0)
    out, before = [], None
    for i in range(r // SUBLANES):
        t = tiles[i] if before is None or i % (block // SUBLANES) == 0 else tiles[i] + before
        out.append(t)
        before = jnp.broadcast_to(t[SUBLANES - 1:SUBLANES, :], (SUBLANES, n))
    return jnp.concatenate(out, axis=0)


def _head_variants(a):
    lo = lax.broadcasted_iota(jnp.int32, a.shape, 1) < HEAD_DIM
    rot = pltpu.roll(a, HEAD_DIM, axis=1)
    return [_bf(jnp.where(lo, a, 0.0)), _bf(jnp.where(lo, 0.0, rot)),
            _bf(jnp.where(lo, rot, 0.0)), _bf(jnp.where(lo, 0.0, a))]


def _mixer_kernel(*refs, tile, chunk, hg_block, carry, d_model, layer):
    n_chunks = tile // chunk
    keys = WINDOW + chunk
    key_pad = 2 * LANES
    assert keys < key_pad and tile >= WINDOW
    n_hg = d_model // HG_DK
    att_w = N_HEADS * HEAD_DIM
    n_pairs = N_HEADS // 2
    ppk = n_pairs // N_KV_HEADS
    rows_g = ppk * chunk
    n_var = WINDOW // chunk + 1 if carry else 1
    f32 = jnp.float32

    refs = list(refs)
    (x_ref, nw_ref, win_ref, wout_ref, sinks_ref, relb_ref, lbp_ref, gn_ref,
     lmask_ref, hmask_ref) = refs[:10]
    refs = refs[10:]
    if not carry:
        hk_ref, hv_ref, si_ref = refs[:3]
        refs = refs[3:]
    y_ref, ko_ref, vo_ref, so_ref = refs[:4]
    refs = refs[4:]
    (bias_s, q4_s, kv_s, kx_s, vx_s, hq_s, kk_s, hi_s, bc_s, og_s, ga_s, gh_s, att_s, hg_s,
     u_s, sb_s, p_s, a_s, qe_s) = refs[:19]
    if carry:
        st_s = refs[19]

    first = (pl.program_id(0) == 0) & (pl.program_id(1) == 0)
    t_idx = pl.program_id(1)

    @pl.when(first)
    def _build_bias():
        lane = lax.broadcasted_iota(jnp.int32, (chunk, key_pad), 1)
        for head in range(N_HEADS):
            acc = jnp.where(lane == keys, sinks_ref[head], jnp.where(lane > keys, -jnp.inf, relb_ref[head])) * LOG2E
            pair, side = divmod(head, 2)
            kvh, j = divmod(pair, ppk)
            for var in range(n_var):
                n_invalid = WINDOW - var * chunk if carry else 0
                bias_s[var, 2 * kvh + side, j * chunk:(j + 1) * chunk, :] = jnp.where(lane < n_invalid, -jnp.inf, acc)

    if carry:
        @pl.when(t_idx == 0)
        def _reset():
            kx_s[:, 0, 0:WINDOW, :] = jnp.zeros((4, WINDOW, LANES), jnp.bfloat16)
            vx_s[:, 0, 0:WINDOW, :] = jnp.zeros((4, WINDOW, LANES), jnp.bfloat16)
            st_s[...] = jnp.zeros_like(st_s)

        @pl.when(t_idx > 0)
        def _shift():
            kx_s[:, 0, 0:WINDOW, :] = kx_s[:, 0, tile:tile + WINDOW, :]
            vx_s[:, 0, 0:WINDOW, :] = vx_s[:, 0, tile:tile + WINDOW, :]

    x = x_ref[0]
    h = _bf(_rms(x, nw_ref[...]))

    def proj(i0, width):
        return _dot(h, win_ref[:, i0:i0 + width])

    def project_queries():
        q = _bf(proj(0, att_w) * (ATTN_SCALE * LOG2E))
        for c in range(n_chunks):
            for j in range(n_pairs):
                q4_s[c, j] = q[c * chunk:(c + 1) * chunk, j * LANES:(j + 1) * LANES]

    off = att_w
    kv = proj(off, 2 * KV_WIDTH)
    kv_s[...] = kv
    off += 2 * KV_WIDTH
    k_var = _head_variants(kv[:, 0:KV_WIDTH])
    v_var = _head_variants(kv[:, KV_WIDTH:2 * KV_WIDTH])
    if carry:
        for g in range(4):
            kx_s[g, 0, WINDOW:WINDOW + tile, :] = k_var[g]
            vx_s[g, 0, WINDOW:WINDOW + tile, :] = v_var[g]
    else:
        for c in range(n_chunks):
            hk = _head_variants(hk_ref[c])
            hv = _head_variants(hv_ref[c])
            for g in range(4):
                kx_s[g, c, 0:WINDOW, :] = hk[g]
                vx_s[g, c, 0:WINDOW, :] = hv[g]
                kx_s[g, c, WINDOW:keys, :] = k_var[g][c * chunk:(c + 1) * chunk]
                vx_s[g, c, WINDOW:keys, :] = v_var[g][c * chunk:(c + 1) * chunk]
    off_hq, off_f, off_i, off_og, off_ga, off_gh = (off + i * d_model for i in range(6))
    lbp = lbp_ref[...]
    e = jnp.exp(lbp - jnp.max(lbp, axis=0, keepdims=True))
    lb = jnp.sum(e[:layer + 1], axis=0, keepdims=True) / jnp.sum(e, axis=0, keepdims=True)

    def gate_quarter(i):
        cs = slice(i * d_model // 4, (i + 1) * d_model // 4)
        half_span = 0.5 * (1.0 - lb[:, cs])
        f = (lb[:, cs] + half_span) + half_span * jnp.tanh(0.5 * proj(off_f + cs.start, d_model // 4))
        kk_s[:, cs] = 1.0 - f
        bc_s[:, cs] = _block_cumsum(jnp.log(f), hg_block)

    hq_s[...] = proj(off_hq, d_model)
    gate_quarter(0)
    hi_s[...] = _bf(proj(off_i, d_model))
    gate_quarter(1)
    project_queries()
    gate_quarter(2)
    y_og = proj(off_og, d_model)
    og_s[...] = y_og * _sigmoid(y_og)
    gate_quarter(3)
    ga_s[...] = _sigmoid(proj(off_ga, d_model))
    gh_s[...] = _sigmoid(proj(off_gh, d_model))

    zpad = jnp.zeros((key_pad - keys, LANES), jnp.bfloat16)

    def window(ref, g, c):
        w = ref[g, 0, c * chunk:c * chunk + keys, :] if carry else ref[g, c]
        return jnp.concatenate([w, zpad], axis=0)

    for c in range(n_chunks):
        var = jnp.minimum(t_idx * n_chunks + c, n_var - 1) if carry else 0
        for kvh in range(N_KV_HEADS):
            qg = q4_s[c, kvh * ppk:(kvh + 1) * ppk].reshape(rows_g, LANES)
            for side in range(2):
                g = 2 * kvh + side
                s = _dot(qg, window(kx_s, g, c), _NT) + bias_s[var, g]
                p_s[c, g] = _bf(jnp.exp2(s - jnp.max(s, axis=-1, keepdims=True)))
    key_row = lax.broadcasted_iota(jnp.int32, (key_pad, LANES), 0)
    key_lane = lax.broadcasted_iota(jnp.int32, (key_pad, LANES), 1)
    ones_lo = _bf(jnp.where((key_row <= keys) & (key_lane < HEAD_DIM), 1.0, 0.0))
    ones_hi = _bf(jnp.where((key_row <= keys) & (key_lane >= HEAD_DIM), 1.0, 0.0))
    for c in range(n_chunks):
        for kvh in range(N_KV_HEADS):
            nd = (_dot(p_s[c, 2 * kvh], jnp.concatenate([window(vx_s, 2 * kvh, c), ones_lo], axis=1))
                  + _dot(p_s[c, 2 * kvh + 1], jnp.concatenate([window(vx_s, 2 * kvh + 1, c), ones_hi], axis=1)))
            o = nd[:, 0:LANES] * (1.0 / nd[:, LANES:2 * LANES])
            for j in range(ppk):
                col = (kvh * ppk + j) * LANES
                att_s[c * chunk:(c + 1) * chunk, col:col + LANES] = o[j * chunk:(j + 1) * chunk]

    blk, n_blk = hg_block, tile // hg_block
    levels = _levels(blk)
    gn = gn_ref[...]
    small = [m for m in levels if m < 8]

    def unit(c, hd):
        return slice(c * blk, (c + 1) * blk), slice(hd * HG_DK, (hd + 1) * HG_DK)

    def bc_row(c, r, cols):
        return bc_s[c * blk + r:c * blk + r + 1, cols]

    def level_operand(m, c, hd):
        rows, cols = unit(c, hd)
        q = hq_s[rows, cols]
        kk = kk_s[rows, cols]
        if m >= 8:
            bc = bc_s[rows, cols]
            parts = []
            for base in range(0, blk, 2 * m):
                ref = bc_row(c, base + m - 1, cols)
                lo, hi = slice(base, base + m), slice(base + m, base + 2 * m)
                parts += [kk[lo] * jnp.exp(ref - bc[lo]), q[hi] * jnp.exp(bc[hi] - ref)]
            return _bf(jnp.concatenate(parts, axis=0))
        upper = hmask_ref[small.index(m)] > 0.5
        if m == 1:
            return _bf(jnp.where(upper, q * (1.0 - kk), kk))
        if m == 4:
            bc = bc_s[rows, cols]
            ref_row = bc.reshape(blk // 8, 8, HG_DK)[:, 3:4, :]
            ref = jnp.broadcast_to(ref_row, (blk // 8, 8, HG_DK)).reshape(blk, HG_DK)
            arg = -jnp.abs(bc - ref)
        else:
            lfh = jnp.log(1.0 - kk)
            nxt = pltpu.roll(lfh, blk - 1, axis=0)
            prv = pltpu.roll(lfh, 1, axis=0)
            n_small = len(small)
            arg = hmask_ref[n_small] * nxt + hmask_ref[n_small + 1] * lfh + hmask_ref[n_small + 2] * prv
        return _bf(jnp.where(upper, q, kk) * jnp.exp(arg))

    tril = lmask_ref[len(levels) + 1] > 0.5
    for c in range(n_blk):
        for hd in range(n_hg):
            rows, cols = unit(c, hd)
            decay_in = jnp.exp(bc_s[rows, cols])
            qe = _bf(hq_s[rows, cols] * decay_in)
            qe_s[rows, cols] = qe
            k_grow_t = _bf((kk_s[rows, cols] * (1.0 / decay_in)).T)
            a_s[c, hd] = _bf(jnp.where(tril, _dot(qe, k_grow_t), 0.0))

    for c in range(n_blk):
        for hd in range(n_hg):
            rows, cols = unit(c, hd)
            k_dec = _bf(kk_s[rows, cols] * jnp.exp(bc_row(c, blk - 1, cols) - bc_s[rows, cols]))
            u_s[c, hd] = _dot(hi_s[rows, cols], k_dec, _TN)

    def readout(c):
        for hd in range(n_hg):
            rows, cols = unit(c, hd)
            o = _dot(qe_s[rows, cols], sb_s[c, hd]) + _dot(a_s[c, hd], hi_s[rows, cols])
            o = o * lax.rsqrt(jnp.mean(o * o, axis=-1, keepdims=True) + RMS_EPS) * gn
            hg_s[rows, cols] = o * og_s[rows, cols]

    def merge_and_project():
        merged = _bf(ga_s[...] * att_s[...] + gh_s[...] * hg_s[...])
        y_ref[0] = x_ref[0] + _dot(merged, wout_ref[...])

    if carry:
        states = [st_s[hd] for hd in range(n_hg)]
    for c in range(n_blk):
        for hd in range(n_hg):
            sb_s[c, hd] = _bf(states[hd].T) if carry else _bf(si_ref[c, hd])
        readout(c)
        for hd in range(n_hg):
            decay = jnp.exp(bc_row(c, blk - 1, unit(c, hd)[1]))
            if carry:
                states[hd] = states[hd] * decay + u_s[c, hd]
            else:
                so_ref[c, hd] = (si_ref[c, hd].T * decay + u_s[c, hd]).T
    if carry:
        for hd in range(n_hg):
            st_s[hd] = states[hd]
    merge_and_project()

    total_decay = jnp.concatenate([bc_row(c, blk - 1, slice(None)) for c in range(n_blk)], axis=0)

    @pl.when(jnp.logical_not(jnp.max(-total_decay) <= DECAY_GUARD))
    def _split():
        for c in range(n_blk):
            for hd in range(n_hg):
                rows, cols = unit(c, hd)
                a = _dot(_bf(hq_s[rows, cols]), _bf(kk_s[rows, cols]), _NT) * lmask_ref[len(levels)]
                for li, m in enumerate(levels):
                    z = level_operand(m, c, hd)
                    a = a + _dot(z, z, _NT) * lmask_ref[li]
                a_s[c, hd] = _bf(a)
        for c in range(n_blk):
            readout(c)
        merge_and_project()

    if carry:
        @pl.when(t_idx == pl.num_programs(1) - 1)
        def _emit():
            ko_ref[0] = kv_s[tile - WINDOW:tile, 0:KV_WIDTH]
            vo_ref[0] = kv_s[tile - WINDOW:tile, KV_WIDTH:2 * KV_WIDTH]
            for hd in range(n_hg):
                so_ref[0, hd] = st_s[hd].T
    else:
        ko_ref[0] = kv_s[:, 0:KV_WIDTH]
        vo_ref[0] = kv_s[:, KV_WIDTH:2 * KV_WIDTH]


def _full_spec(shape):
    nd = len(shape)
    return pl.BlockSpec(shape, lambda *_: (0,) * nd)


def _weight_spec(shape):
    nd = len(shape)
    return pl.BlockSpec(shape, lambda *_: (0,) * nd, pipeline_mode=pl.Buffered(1))


def _mixer(x, hist, norm_w, w_in, w_out, sinks, table, lb_params, gnorm_w, *, layer, carry, tile, chunk,
           hg_block):
    nb, s, d = x.shape
    n_t = s // tile
    n_chunks = tile // chunk
    n_blk = tile // hg_block
    assert carry or hg_block == chunk
    keys = WINDOW + chunk
    key_pad = 2 * LANES
    n_hg = d // HG_DK
    n_streams = nb if carry else s // chunk
    n_off = chunk + key_pad - 1
    offsets = jnp.arange(n_off, dtype=jnp.int32) - (chunk - 1) - WINDOW
    by_offset = table[_t5_bucket(offsets)].astype(jnp.float32).T
    period = jnp.roll(jnp.pad(by_offset, ((0, 0), (0, 1))), -(chunk - 1), axis=1)
    skewed = jnp.tile(period, (1, chunk))[:, :chunk * n_off].reshape(N_HEADS, chunk, n_off)
    rel_bias = skewed[:, :, :key_pad]
    lmask = jnp.asarray(_level_masks(hg_block))
    hmask = jnp.asarray(_row_masks(hg_block))
    smem = pl.BlockSpec(memory_space=pltpu.SMEM)

    in_arrays = [x, norm_w.reshape(1, d), w_in, w_out, sinks, rel_bias, lb_params,
                 gnorm_w.reshape(1, HG_DK), lmask, hmask]
    in_specs = [pl.BlockSpec((1, tile, d), lambda b, t: (b, t, 0)), _full_spec((1, d)),
                _weight_spec(w_in.shape), _weight_spec(w_out.shape), smem, _full_spec(rel_bias.shape),
                _full_spec(lb_params.shape), _full_spec((1, HG_DK)),
                _full_spec(lmask.shape), _full_spec(hmask.shape)]
    if carry:
        kv_rows = WINDOW
        st_block = (1, n_hg, HG_DK, HG_DK)
        n_win, win_rows, n_var = 1, WINDOW + tile, WINDOW // chunk + 1
    else:
        in_arrays += list(hist)
        in_specs += [_full_spec(a.shape) for a in hist]
        kv_rows = s
        st_block = (n_streams, n_hg, HG_DK, HG_DK)
        n_win, win_rows, n_var = n_chunks, keys, 1
    out_shape = [jax.ShapeDtypeStruct((nb, s, d), jnp.float32),
                 jax.ShapeDtypeStruct((nb, kv_rows, KV_WIDTH), jnp.float32),
                 jax.ShapeDtypeStruct((nb, kv_rows, KV_WIDTH), jnp.float32),
                 jax.ShapeDtypeStruct((n_streams, n_hg, HG_DK, HG_DK), jnp.float32)]
    out_specs = [pl.BlockSpec((1, tile, d), lambda b, t: (b, t, 0)),
                 pl.BlockSpec((1, kv_rows, KV_WIDTH), lambda b, t: (b, 0, 0)),
                 pl.BlockSpec((1, kv_rows, KV_WIDTH), lambda b, t: (b, 0, 0)),
                 pl.BlockSpec(st_block, lambda b, t: (b, 0, 0, 0))]
    f32, bf16 = jnp.float32, jnp.bfloat16
    rows_g = (N_HEADS // 2 // N_KV_HEADS) * chunk
    scratch = [pltpu.VMEM((n_var, 4, rows_g, key_pad), f32),
               pltpu.VMEM((n_chunks, N_HEADS // 2, chunk, LANES), bf16),
               pltpu.VMEM((tile, 2 * KV_WIDTH), f32),
               pltpu.VMEM((4, n_win, win_rows, LANES), bf16),
               pltpu.VMEM((4, n_win, win_rows, LANES), bf16),
               pltpu.VMEM((tile, d), f32), pltpu.VMEM((tile, d), f32),
               pltpu.VMEM((tile, d), bf16)]
    scratch += [pltpu.VMEM((tile, d), f32) for _ in range(6)]
    scratch += [pltpu.VMEM((n_blk, n_hg, HG_DK, HG_DK), f32),
                pltpu.VMEM((n_blk, n_hg, HG_DK, HG_DK), bf16),
                pltpu.VMEM((n_chunks, 4, rows_g, key_pad), bf16),
                pltpu.VMEM((n_blk, n_hg, hg_block, hg_block), bf16),
                pltpu.VMEM((tile, d), bf16)]
    if carry:
        scratch += [pltpu.VMEM((n_hg, HG_DK, HG_DK), f32)]
    kern = functools.partial(_mixer_kernel, tile=tile, chunk=chunk, hg_block=hg_block, carry=carry, d_model=d,
                             layer=layer)
    return pl.pallas_call(
        kern,
        grid=(nb, n_t),
        in_specs=in_specs,
        out_specs=out_specs,
        out_shape=out_shape,
        scratch_shapes=scratch,
        compiler_params=pltpu.CompilerParams(
            dimension_semantics=("arbitrary", "arbitrary"), vmem_limit_bytes=VMEM_LIMIT_BYTES),
        name="mixer_prompt" if carry else "mixer_sample",
    )(*in_arrays)


def _ffn_kernel(xa_ref, xb_ref, nw_ref, wgu_ref, wd_ref, fw_ref, ya_ref, yb_ref, act_s, *, d_ff, col_tile,
                final_norm):
    n_a = pl.num_programs(0) - 1

    def ffn(x_ref, y_ref):
        rows = x_ref.shape[0]
        x = x_ref[...]
        h = _bf(x * nw_ref[...])
        r = lax.rsqrt(jnp.mean(x * x, axis=-1, keepdims=True) + RMS_EPS)
        for j in range(d_ff // col_tile):
            g = r * _dot(h, _bf(wgu_ref[:, j * col_tile:(j + 1) * col_tile]))
            u = r * _dot(h, _bf(wgu_ref[:, d_ff + j * col_tile:d_ff + (j + 1) * col_tile]))
            act_s[0:rows, j * col_tile:(j + 1) * col_tile] = _bf(jax.nn.silu(g) * u)
        y = x + _dot(act_s[0:rows, :], _bf(wd_ref[...]))
        if final_norm:
            y = _rms(y, fw_ref[...])
        y_ref[...] = y

    @pl.when(pl.program_id(0) < n_a)
    def _first_set():
        ffn(xa_ref, ya_ref)

    @pl.when(pl.program_id(0) == n_a)
    def _second_set():
        ffn(xb_ref, yb_ref)


def _ffn(xa, xb, norm_w, w_gate_up, w_down, final_w, *, tile, final_norm):
    n, d = xa.shape
    m = xb.shape[0]
    n_a = n // tile
    assert n == n_a * tile and m <= tile
    d_ff = w_down.shape[0]
    kern = functools.partial(_ffn_kernel, d_ff=d_ff, col_tile=2 * LANES, final_norm=final_norm)
    a_spec = pl.BlockSpec((tile, d), lambda i: (jnp.minimum(i, n_a - 1), 0))
    return pl.pallas_call(
        kern,
        grid=(n_a + 1,),
        in_specs=[a_spec, _full_spec((m, d)), _full_spec((1, d)),
                  _weight_spec(w_gate_up.shape), _weight_spec(w_down.shape), _full_spec((1, d))],
        out_specs=[a_spec, _full_spec((m, d))],
        out_shape=[jax.ShapeDtypeStruct((n, d), jnp.float32), jax.ShapeDtypeStruct((m, d), jnp.float32)],
        scratch_shapes=[pltpu.VMEM((tile, d_ff), jnp.bfloat16)],
        compiler_params=pltpu.CompilerParams(
            dimension_semantics=("arbitrary",), vmem_limit_bytes=VMEM_LIMIT_BYTES),
        name="ffn",
    )(xa, xb, norm_w.reshape(1, d), w_gate_up, w_down, final_w.reshape(1, d))


def kernel(x_prompt, x_sample, cache_k, cache_v, state_hgrn, norm_mix, w_in, w_out, attn_sinks, rel_bias_table,
           hgrn_lb, hgrn_norm, norm_ffn, w_gate_up, w_down, norm_final):
    depth = w_in.shape[0]
    batch, seq, d = x_prompt.shape
    dec_batch, dec_seq, _ = x_sample.shape
    assert w_in.shape[2] == N_HEADS * HEAD_DIM + 2 * KV_WIDTH + 6 * d and KV_WIDTH == LANES and d % (4 * LANES) == 0
    assert seq % PROMPT_TILE == 0 and (batch * seq) % FFN_TILE == 0 and dec_batch * dec_seq <= FFN_TILE
    assert dec_seq % (2 * SUBLANES) == 0 and cache_k.shape[2] == WINDOW
    xp = x_prompt
    xs = x_sample.reshape(1, dec_batch * dec_seq, d)
    outs = [[] for _ in range(6)]
    for l in range(depth):
        w_in_l, w_out_l = _bf(w_in[l]), _bf(w_out[l])
        shared = (norm_mix[l], w_in_l, w_out_l, attn_sinks[l], rel_bias_table, hgrn_lb, hgrn_norm[l])
        xp, kp, vp, sp = _mixer(xp, None, *shared, layer=l, carry=True, tile=PROMPT_TILE, chunk=CHUNK,
                                hg_block=HGRN_BLOCK)
        hist = (cache_k[l].reshape(dec_batch, WINDOW, KV_WIDTH), cache_v[l].reshape(dec_batch, WINDOW, KV_WIDTH),
                state_hgrn[l])
        xs, ks, vs, ss = _mixer(xs, hist, *shared, layer=l, carry=False, tile=dec_batch * dec_seq, chunk=dec_seq,
                                hg_block=dec_seq)
        last = l == depth - 1
        xp, xs = _ffn(xp.reshape(batch * seq, d), xs.reshape(dec_batch * dec_seq, d), norm_ffn[l], w_gate_up[l],
                      w_down[l], norm_final, tile=FFN_TILE, final_norm=last)
        xp = xp.reshape(batch, seq, d)
        xs = xs.reshape(1, dec_batch * dec_seq, d)
        ks = jnp.concatenate([hist[0][:, dec_seq:], ks.reshape(dec_batch, dec_seq, KV_WIDTH)], axis=1)
        vs = jnp.concatenate([hist[1][:, dec_seq:], vs.reshape(dec_batch, dec_seq, KV_WIDTH)], axis=1)
        kv_shape = (-1, WINDOW, N_KV_HEADS, HEAD_DIM)
        for acc, val in zip(outs, (kp.reshape(kv_shape), vp.reshape(kv_shape), sp,
                                   ks.reshape(kv_shape), vs.reshape(kv_shape), ss)):
            acc.append(val)
    return (xp, xs.reshape(dec_batch, dec_seq, d)) + tuple(jnp.stack(o) for o in outs)
```

```python
import functools
import math

import numpy as np
import jax
import jax.numpy as jnp
from jax import lax
from jax.experimental import pallas as pl
from jax.experimental.pallas import tpu as pltpu

CHUNK = 64
N_HEADS = 16
N_KV_HEADS = 2
HEAD_DIM = 64
KV_WIDTH = N_KV_HEADS * HEAD_DIM
WINDOW = 128
ATTN_SCALE = HEAD_DIM ** -0.5
LOG2E = math.log2(math.e)
N_BUCKETS = 32
MAX_DISTANCE = 128
HG_DK = 128
RMS_EPS = 1e-6
DECAY_GUARD = 75.0

LANES = 128
SUBLANES = 8
VMEM_LIMIT_BYTES = 56 * 1024 * 1024
PROMPT_TILE = 4 * CHUNK
HGRN_BLOCK = 2 * CHUNK
FFN_TILE = 512

_NT = (((1,), (1,)), ((), ()))
_TN = (((0,), (0,)), ((), ()))


def _bf(x):
    return x.astype(jnp.bfloat16)


def _dot(a, b, dims=None):
    if dims is None:
        return jnp.dot(a, b, preferred_element_type=jnp.float32)
    return lax.dot_general(a, b, dims, preferred_element_type=jnp.float32)


def _sigmoid(x):
    return 0.5 * jnp.tanh(0.5 * x) + 0.5


def _rms(x, w):
    return x * lax.rsqrt(jnp.mean(x * x, axis=-1, keepdims=True) + RMS_EPS) * w


def _t5_bucket(rel):
    nb = N_BUCKETS // 2
    max_exact = nb // 2
    ret = jnp.where(rel > 0, nb, 0)
    n = jnp.abs(rel)
    nf = jnp.maximum(n, 1).astype(jnp.float32)
    large = max_exact + (jnp.log(nf / max_exact) / math.log(MAX_DISTANCE / max_exact)
                         * (nb - max_exact)).astype(jnp.int32)
    large = jnp.minimum(large, nb - 1)
    return ret + jnp.where(n < max_exact, n, large)


def _levels(c):
    out, m = [], c // 2
    while m >= 1:
        out.append(m)
        m //= 2
    return out


def _level_masks(c):
    t = np.arange(c)[:, None]
    s = np.arange(c)[None, :]
    masks = []
    for m in _levels(c):
        masks.append((t // (2 * m) == s // (2 * m)) & ((t // m) % 2 == 1) & ((s // m) % 2 == 0))
    masks += [t == s, s <= t]
    return np.stack(masks).astype(np.float32)


def _row_masks(c):
    r = np.arange(c)
    rows = [(r // m) % 2 == 1 for m in _levels(c) if m < 8]
    rows += [r % 4 == 0, r % 4 >= 2, r % 4 == 3]
    return np.repeat(np.stack(rows).astype(np.float32)[:, :, None], LANES, axis=2)


def _block_cumsum(x, block):
    r, n = x.shape
    tiles = x.reshape(r // SUBLANES, SUBLANES, n)
    row = lax.broadcasted_iota(jnp.int32, (1, SUBLANES, n), 1)
    for shift in (1, 2, 4):
        tiles = tiles + jnp.where(row >= shift, pltpu.roll(tiles, shift, axis=1), 0.0)
    out, before = [], None
    for i in range(r // SUBLANES):
        t = tiles[i] if before is None or i % (block // SUBLANES) == 0 else tiles[i] + before
        out.append(t)
        before = jnp.broadcast_to(t[SUBLANES - 1:SUBLANES, :], (SUBLANES, n))
    return jnp.concatenate(out, axis=0)


def _head_variants(a):
    lo = lax.broadcasted_iota(jnp.int32, a.shape, 1) < HEAD_DIM
    rot = pltpu.roll(a, HEAD_DIM, axis=1)
    return [_bf(jnp.where(lo, a, 0.0)), _bf(jnp.where(lo, 0.0, rot)),
            _bf(jnp.where(lo, rot, 0.0)), _bf(jnp.where(lo, 0.0, a))]


def _mixer_kernel(*refs, tile, chunk, hg_block, carry, d_model, layer):
    n_chunks = tile // chunk
    keys = WINDOW + chunk
    key_pad = 2 * LANES
    assert keys < key_pad and tile >= WINDOW
    n_hg = d_model // HG_DK
    att_w = N_HEADS * HEAD_DIM
    n_pairs = N_HEADS // 2
    ppk = n_pairs // N_KV_HEADS
    rows_g = ppk * chunk
    n_var = WINDOW // chunk + 1 if carry else 1
    f32 = jnp.float32

    refs = list(refs)
    (x_ref, nw_ref, win_ref, wout_ref, sinks_ref, relb_ref, lbp_ref, gn_ref,
     lmask_ref, hmask_ref) = refs[:10]
    refs = refs[10:]
    if not carry:
        hk_ref, hv_ref, si_ref = refs[:3]
        refs = refs[3:]
    y_ref, ko_ref, vo_ref, so_ref = refs[:4]
    refs = refs[4:]
    (bias_s, q4_s, kv_s, kx_s, vx_s, hq_s, kk_s, hi_s, bc_s, og_s, ga_s, gh_s, att_s, hg_s,
     u_s, sb_s, p_s, a_s, qe_s) = refs[:19]
    if carry:
        st_s = refs[19]

    first = (pl.program_id(0) == 0) & (pl.program_id(1) == 0)
    t_idx = pl.program_id(1)

    @pl.when(first)
    def _build_bias():
        lane = lax.broadcasted_iota(jnp.int32, (chunk, key_pad), 1)
        for head in range(N_HEADS):
            acc = jnp.where(lane == keys, sinks_ref[head], jnp.where(lane > keys, -jnp.inf, relb_ref[head])) * LOG2E
            pair, side = divmod(head, 2)
            kvh, j = divmod(pair, ppk)
            for var in range(n_var):
                n_invalid = WINDOW - var * chunk if carry else 0
                bias_s[var, 2 * kvh + side, j * chunk:(j + 1) * chunk, :] = jnp.where(lane < n_invalid, -jnp.inf, acc)

    if carry:
        @pl.when(t_idx == 0)
        def _reset():
            kx_s[:, 0, 0:WINDOW, :] = jnp.zeros((4, WINDOW, LANES), jnp.bfloat16)
            vx_s[:, 0, 0:WINDOW, :] = jnp.zeros((4, WINDOW, LANES), jnp.bfloat16)
            st_s[...] = jnp.zeros_like(st_s)

        @pl.when(t_idx > 0)
        def _shift():
            kx_s[:, 0, 0:WINDOW, :] = kx_s[:, 0, tile:tile + WINDOW, :]
            vx_s[:, 0, 0:WINDOW, :] = vx_s[:, 0, tile:tile + WINDOW, :]

    x = x_ref[0]
    h = _bf(_rms(x, nw_ref[...]))

    def proj(i0, width):
        return _dot(h, win_ref[:, i0:i0 + width])

    def project_queries():
        q = _bf(proj(0, att_w) * (ATTN_SCALE * LOG2E))
        for c in range(n_chunks):
            for j in range(n_pairs):
                q4_s[c, j] = q[c * chunk:(c + 1) * chunk, j * LANES:(j + 1) * LANES]

    off = att_w
    kv = proj(off, 2 * KV_WIDTH)
    kv_s[...] = kv
    off += 2 * KV_WIDTH
    k_var = _head_variants(kv[:, 0:KV_WIDTH])
    v_var = _head_variants(kv[:, KV_WIDTH:2 * KV_WIDTH])
    if carry:
        for g in range(4):
            kx_s[g, 0, WINDOW:WINDOW + tile, :] = k_var[g]
            vx_s[g, 0, WINDOW:WINDOW + tile, :] = v_var[g]
    else:
        for c in range(n_chunks):
            hk = _head_variants(hk_ref[c])
            hv = _head_variants(hv_ref[c])
            for g in range(4):
                kx_s[g, c, 0:WINDOW, :] = hk[g]
                vx_s[g, c, 0:WINDOW, :] = hv[g]
                kx_s[g, c, WINDOW:keys, :] = k_var[g][c * chunk:(c + 1) * chunk]
                vx_s[g, c, WINDOW:keys, :] = v_var[g][c * chunk:(c + 1) * chunk]
    off_hq, off_f, off_i, off_og, off_ga, off_gh = (off + i * d_model for i in range(6))
    lbp = lbp_ref[...]
    e = jnp.exp(lbp - jnp.max(lbp, axis=0, keepdims=True))
    lb = jnp.sum(e[:layer + 1], axis=0, keepdims=True) / jnp.sum(e, axis=0, keepdims=True)

    def gate_quarter(i):
        cs = slice(i * d_model // 4, (i + 1) * d_model // 4)
        half_span = 0.5 * (1.0 - lb[:, cs])
        f = (lb[:, cs] + half_span) + half_span * jnp.tanh(0.5 * proj(off_f + cs.start, d_model // 4))
        kk_s[:, cs] = 1.0 - f
        bc_s[:, cs] = _block_cumsum(jnp.log(f), hg_block)

    hq_s[...] = proj(off_hq, d_model)
    gate_quarter(0)
    hi_s[...] = _bf(proj(off_i, d_model))
    gate_quarter(1)
    project_queries()
    gate_quarter(2)
    y_og = proj(off_og, d_model)
    og_s[...] = y_og * _sigmoid(y_og)
    gate_quarter(3)
    ga_s[...] = _sigmoid(proj(off_ga, d_model))
    gh_s[...] = _sigmoid(proj(off_gh, d_model))

    zpad = jnp.zeros((key_pad - keys, LANES), jnp.bfloat16)

    def window(ref, g, c):
        w = ref[g, 0, c * chunk:c * chunk + keys, :] if carry else ref[g, c]
        return jnp.concatenate([w, zpad], axis=0)

    for c in range(n_chunks):
        var = jnp.minimum(t_idx * n_chunks + c, n_var - 1) if carry else 0
        for kvh in range(N_KV_HEADS):
            qg = q4_s[c, kvh * ppk:(kvh + 1) * ppk].reshape(rows_g, LANES)
            for side in range(2):
                g = 2 * kvh + side
                s = _dot(qg, window(kx_s, g, c), _NT) + bias_s[var, g]
                p_s[c, g] = _bf(jnp.exp2(s - jnp.max(s, axis=-1, keepdims=True)))
    key_row = lax.broadcasted_iota(jnp.int32, (key_pad, LANES), 0)
    key_lane = lax.broadcasted_iota(jnp.int32, (key_pad, LANES), 1)
    ones_lo = _bf(jnp.where((key_row <= keys) & (key_lane < HEAD_DIM), 1.0, 0.0))
    ones_hi = _bf(jnp.where((key_row <= keys) & (key_lane >= HEAD_DIM), 1.0, 0.0))
    for c in range(n_chunks):
        for kvh in range(N_KV_HEADS):
            nd = (_dot(p_s[c, 2 * kvh], jnp.concatenate([window(vx_s, 2 * kvh, c), ones_lo], axis=1))
                  + _dot(p_s[c, 2 * kvh + 1], jnp.concatenate([window(vx_s, 2 * kvh + 1, c), ones_hi], axis=1)))
            o = nd[:, 0:LANES] * (1.0 / nd[:, LANES:2 * LANES])
            for j in range(ppk):
                col = (kvh * ppk + j) * LANES
                att_s[c * chunk:(c + 1) * chunk, col:col + LANES] = o[j * chunk:(j + 1) * chunk]

    blk, n_blk = hg_block, tile // hg_block
    levels = _levels(blk)
    gn = gn_ref[...]
    small = [m for m in levels if m < 8]

    def unit(c, hd):
        return slice(c * blk, (c + 1) * blk), slice(hd * HG_DK, (hd + 1) * HG_DK)

    def bc_row(c, r, cols):
        return bc_s[c * blk + r:c * blk + r + 1, cols]

    def level_operand(m, c, hd):
        rows, cols = unit(c, hd)
        q = hq_s[rows, cols]
        kk = kk_s[rows, cols]
        if m >= 8:
            bc = bc_s[rows, cols]
            parts = []
            for base in range(0, blk, 2 * m):
                ref = bc_row(c, base + m - 1, cols)
                lo, hi = slice(base, base + m), slice(base + m, base + 2 * m)
                parts += [kk[lo] * jnp.exp(ref - bc[lo]), q[hi] * jnp.exp(bc[hi] - ref)]
            return _bf(jnp.concatenate(parts, axis=0))
        upper = hmask_ref[small.index(m)] > 0.5
        if m == 1:
            return _bf(jnp.where(upper, q * (1.0 - kk), kk))
        if m == 4:
            bc = bc_s[rows, cols]
            ref_row = bc.reshape(blk // 8, 8, HG_DK)[:, 3:4, :]
            ref = jnp.broadcast_to(ref_row, (blk // 8, 8, HG_DK)).reshape(blk, HG_DK)
            arg = -jnp.abs(bc - ref)
        else:
            lfh = jnp.log(1.0 - kk)
            nxt = pltpu.roll(lfh, blk - 1, axis=0)
            prv = pltpu.roll(lfh, 1, axis=0)
            n_small = len(small)
            arg = hmask_ref[n_small] * nxt + hmask_ref[n_small + 1] * lfh + hmask_ref[n_small + 2] * prv
        return _bf(jnp.where(upper, q, kk) * jnp.exp(arg))

    tril = lmask_ref[len(levels) + 1] > 0.5
    for c in range(n_blk):
        for hd in range(n_hg):
            rows, cols = unit(c, hd)
            decay_in = jnp.exp(bc_s[rows, cols])
            qe = _bf(hq_s[rows, cols] * decay_in)
            qe_s[rows, cols] = qe
            k_grow_t = _bf((kk_s[rows, cols] * (1.0 / decay_in)).T)
            a_s[c, hd] = _bf(jnp.where(tril, _dot(qe, k_grow_t), 0.0))

    for c in range(n_blk):
        for hd in range(n_hg):
            rows, cols = unit(c, hd)
            k_dec = _bf(kk_s[rows, cols] * jnp.exp(bc_row(c, blk - 1, cols) - bc_s[rows, cols]))
            u_s[c, hd] = _dot(hi_s[rows, cols], k_dec, _TN)

    def readout(c):
        for hd in range(n_hg):
            rows, cols = unit(c, hd)
            o = _dot(qe_s[rows, cols], sb_s[c, hd]) + _dot(a_s[c, hd], hi_s[rows, cols])
            o = o * lax.rsqrt(jnp.mean(o * o, axis=-1, keepdims=True) + RMS_EPS) * gn
            hg_s[rows, cols] = o * og_s[rows, cols]

    def merge_and_project():
        merged = _bf(ga_s[...] * att_s[...] + gh_s[...] * hg_s[...])
        y_ref[0] = x_ref[0] + _dot(merged, wout_ref[...])

    if carry:
        states = [st_s[hd] for hd in range(n_hg)]
    for c in range(n_blk):
        for hd in range(n_hg):
            sb_s[c, hd] = _bf(states[hd].T) if carry else _bf(si_ref[c, hd])
        readout(c)
        for hd in range(n_hg):
            decay = jnp.exp(bc_row(c, blk - 1, unit(c, hd)[1]))
            if carry:
                states[hd] = states[hd] * decay + u_s[c, hd]
            else:
                so_ref[c, hd] = (si_ref[c, hd].T * decay + u_s[c, hd]).T
    if carry:
        for hd in range(n_hg):
            st_s[hd] = states[hd]
    merge_and_project()

    total_decay = jnp.concatenate([bc_row(c, blk - 1, slice(None)) for c in range(n_blk)], axis=0)

    @pl.when(jnp.logical_not(jnp.max(-total_decay) <= DECAY_GUARD))
    def _split():
        for c in range(n_blk):
            for hd in range(n_hg):
                rows, cols = unit(c, hd)
                a = _dot(_bf(hq_s[rows, cols]), _bf(kk_s[rows, cols]), _NT) * lmask_ref[len(levels)]
                for li, m in enumerate(levels):
                    z = level_operand(m, c, hd)
                    a = a + _dot(z, z, _NT) * lmask_ref[li]
                a_s[c, hd] = _bf(a)
        for c in range(n_blk):
            readout(c)
        merge_and_project()

    if carry:
        @pl.when(t_idx == pl.num_programs(1) - 1)
        def _emit():
            ko_ref[0] = kv_s[tile - WINDOW:tile, 0:KV_WIDTH]
            vo_ref[0] = kv_s[tile - WINDOW:tile, KV_WIDTH:2 * KV_WIDTH]
            for hd in range(n_hg):
                so_ref[0, hd] = st_s[hd].T
    else:
        ko_ref[0] = kv_s[:, 0:KV_WIDTH]
        vo_ref[0] = kv_s[:, KV_WIDTH:2 * KV_WIDTH]


def _full_spec(shape):
    nd = len(shape)
    return pl.BlockSpec(shape, lambda *_: (0,) * nd)


def _weight_spec(shape):
    nd = len(shape)
    return pl.BlockSpec(shape, lambda *_: (0,) * nd, pipeline_mode=pl.Buffered(1))


def _mixer(x, hist, norm_w, w_in, w_out, sinks, table, lb_params, gnorm_w, *, layer, carry, tile, chunk,
           hg_block):
    nb, s, d = x.shape
    n_t = s // tile
    n_chunks = tile // chunk
    n_blk = tile // hg_block
    assert carry or hg_block == chunk
    keys = WINDOW + chunk
    key_pad = 2 * LANES
    n_hg = d // HG_DK
    n_streams = nb if carry else s // chunk
    n_off = chunk + key_pad - 1
    offsets = jnp.arange(n_off, dtype=jnp.int32) - (chunk - 1) - WINDOW
    by_offset = table[_t5_bucket(offsets)].astype(jnp.float32).T
    period = jnp.roll(jnp.pad(by_offset, ((0, 0), (0, 1))), -(chunk - 1), axis=1)
    skewed = jnp.tile(period, (1, chunk))[:, :chunk * n_off].reshape(N_HEADS, chunk, n_off)
    rel_bias = skewed[:, :, :key_pad]
    lmask = jnp.asarray(_level_masks(hg_block))
    hmask = jnp.asarray(_row_masks(hg_block))
    smem = pl.BlockSpec(memory_space=pltpu.SMEM)

    in_arrays = [x, norm_w.reshape(1, d), w_in, w_out, sinks, rel_bias, lb_params,
                 gnorm_w.reshape(1, HG_DK), lmask, hmask]
    in_specs = [pl.BlockSpec((1, tile, d), lambda b, t: (b, t, 0)), _full_spec((1, d)),
                _weight_spec(w_in.shape), _weight_spec(w_out.shape), smem, _full_spec(rel_bias.shape),
                _full_spec(lb_params.shape), _full_spec((1, HG_DK)),
                _full_spec(lmask.shape), _full_spec(hmask.shape)]
    if carry:
        kv_rows = WINDOW
        st_block = (1, n_hg, HG_DK, HG_DK)
        n_win, win_rows, n_var = 1, WINDOW + tile, WINDOW // chunk + 1
    else:
        in_arrays += list(hist)
        in_specs += [_full_spec(a.shape) for a in hist]
        kv_rows = s
        st_block = (n_streams, n_hg, HG_DK, HG_DK)
        n_win, win_rows, n_var = n_chunks, keys, 1
    out_shape = [jax.ShapeDtypeStruct((nb, s, d), jnp.float32),
                 jax.ShapeDtypeStruct((nb, kv_rows, KV_WIDTH), jnp.float32),
                 jax.ShapeDtypeStruct((nb, kv_rows, KV_WIDTH), jnp.float32),
                 jax.ShapeDtypeStruct((n_streams, n_hg, HG_DK, HG_DK), jnp.float32)]
    out_specs = [pl.BlockSpec((1, tile, d), lambda b, t: (b, t, 0)),
                 pl.BlockSpec((1, kv_rows, KV_WIDTH), lambda b, t: (b, 0, 0)),
                 pl.BlockSpec((1, kv_rows, KV_WIDTH), lambda b, t: (b, 0, 0)),
                 pl.BlockSpec(st_block, lambda b, t: (b, 0, 0, 0))]
    f32, bf16 = jnp.float32, jnp.bfloat16
    rows_g = (N_HEADS // 2 // N_KV_HEADS) * chunk
    scratch = [pltpu.VMEM((n_var, 4, rows_g, key_pad), f32),
               pltpu.VMEM((n_chunks, N_HEADS // 2, chunk, LANES), bf16),
               pltpu.VMEM((tile, 2 * KV_WIDTH), f32),
               pltpu.VMEM((4, n_win, win_rows, LANES), bf16),
               pltpu.VMEM((4, n_win, win_rows, LANES), bf16),
               pltpu.VMEM((tile, d), f32), pltpu.VMEM((tile, d), f32),
               pltpu.VMEM((tile, d), bf16)]
    scratch += [pltpu.VMEM((tile, d), f32) for _ in range(6)]
    scratch += [pltpu.VMEM((n_blk, n_hg, HG_DK, HG_DK), f32),
                pltpu.VMEM((n_blk, n_hg, HG_DK, HG_DK), bf16),
                pltpu.VMEM((n_chunks, 4, rows_g, key_pad), bf16),
                pltpu.VMEM((n_blk, n_hg, hg_block, hg_block), bf16),
                pltpu.VMEM((tile, d), bf16)]
    if carry:
        scratch += [pltpu.VMEM((n_hg, HG_DK, HG_DK), f32)]
    kern = functools.partial(_mixer_kernel, tile=tile, chunk=chunk, hg_block=hg_block, carry=carry, d_model=d,
                             layer=layer)
    return pl.pallas_call(
        kern,
        grid=(nb, n_t),
        in_specs=in_specs,
        out_specs=out_specs,
        out_shape=out_shape,
        scratch_shapes=scratch,
        compiler_params=pltpu.CompilerParams(
            dimension_semantics=("arbitrary", "arbitrary"), vmem_limit_bytes=VMEM_LIMIT_BYTES),
        name="mixer_prompt" if carry else "mixer_sample",
    )(*in_arrays)


def _ffn_kernel(xa_ref, xb_ref, nw_ref, wgu_hbm, wd_hbm, fw_ref, ya_ref, yb_ref, act_s, wgu_ref, wd_ref, sems, *,
                d_ff, col_tile, final_norm):
    step = pl.program_id(0)
    n_a = pl.num_programs(0) - 1
    n_cols = d_ff // col_tile

    def slab_copy(j, half):
        cols = slice(half * d_ff + j * col_tile, half * d_ff + (j + 1) * col_tile)
        return pltpu.make_async_copy(wgu_hbm.at[:, cols], wgu_ref.at[:, cols], sems.at[2 * j + half])

    def down_copy():
        return pltpu.make_async_copy(wd_hbm, wd_ref, sems.at[2 * n_cols])

    def ffn(x_ref, y_ref, weights_arriving):
        rows = x_ref.shape[0]
        x = x_ref[...]
        h = _bf(x * nw_ref[...])
        r = lax.rsqrt(jnp.mean(x * x, axis=-1, keepdims=True) + RMS_EPS)
        for j in range(n_cols):
            if weights_arriving:
                slab_copy(j, 0).wait()
                slab_copy(j, 1).wait()
            g = r * _dot(h, _bf(wgu_ref[:, j * col_tile:(j + 1) * col_tile]))
            u = r * _dot(h, _bf(wgu_ref[:, d_ff + j * col_tile:d_ff + (j + 1) * col_tile]))
            act_s[0:rows, j * col_tile:(j + 1) * col_tile] = _bf(jax.nn.silu(g) * u)
        if weights_arriving:
            down_copy().wait()
        y = x + _dot(act_s[0:rows, :], _bf(wd_ref[...]))
        if final_norm:
            y = _rms(y, fw_ref[...])
        y_ref[...] = y

    @pl.when(step == 0)
    def _first_tile():
        for j in range(n_cols):
            slab_copy(j, 0).start()
            slab_copy(j, 1).start()
        down_copy().start()
        ffn(xa_ref, ya_ref, True)

    @pl.when((step > 0) & (step < n_a))
    def _first_set():
        ffn(xa_ref, ya_ref, False)

    @pl.when(step == n_a)
    def _second_set():
        ffn(xb_ref, yb_ref, False)


def _ffn(xa, xb, norm_w, w_gate_up, w_down, final_w, *, tile, final_norm):
    n, d = xa.shape
    m = xb.shape[0]
    n_a = n // tile
    assert n == n_a * tile and n_a >= 1 and m <= tile
    d_ff = w_down.shape[0]
    col_tile = 2 * LANES
    kern = functools.partial(_ffn_kernel, d_ff=d_ff, col_tile=col_tile, final_norm=final_norm)
    a_spec = pl.BlockSpec((tile, d), lambda i: (jnp.minimum(i, n_a - 1), 0))
    in_hbm = pl.BlockSpec(memory_space=pl.ANY)
    return pl.pallas_call(
        kern,
        grid=(n_a + 1,),
        in_specs=[a_spec, _full_spec((m, d)), _full_spec((1, d)), in_hbm, in_hbm, _full_spec((1, d))],
        out_specs=[a_spec, _full_spec((m, d))],
        out_shape=[jax.ShapeDtypeStruct((n, d), jnp.float32), jax.ShapeDtypeStruct((m, d), jnp.float32)],
        scratch_shapes=[pltpu.VMEM((tile, d_ff), jnp.bfloat16),
                        pltpu.VMEM(w_gate_up.shape, w_gate_up.dtype), pltpu.VMEM(w_down.shape, w_down.dtype),
                        pltpu.SemaphoreType.DMA((2 * (d_ff // col_tile) + 1,))],
        compiler_params=pltpu.CompilerParams(
            dimension_semantics=("arbitrary",), vmem_limit_bytes=VMEM_LIMIT_BYTES),
        name="ffn",
    )(xa, xb, norm_w.reshape(1, d), w_gate_up, w_down, final_w.reshape(1, d))


def kernel(x_prompt, x_sample, cache_k, cache_v, state_hgrn, norm_mix, w_in, w_out, attn_sinks, rel_bias_table,
           hgrn_lb, hgrn_norm, norm_ffn, w_gate_up, w_down, norm_final):
    depth = w_in.shape[0]
    batch, seq, d = x_prompt.shape
    dec_batch, dec_seq, _ = x_sample.shape
    assert w_in.shape[2] == N_HEADS * HEAD_DIM + 2 * KV_WIDTH + 6 * d and KV_WIDTH == LANES and d % (4 * LANES) == 0
    assert seq % PROMPT_TILE == 0 and (batch * seq) % FFN_TILE == 0 and dec_batch * dec_seq <= FFN_TILE
    assert dec_seq % (2 * SUBLANES) == 0 and cache_k.shape[2] == WINDOW
    xp = x_prompt
    xs = x_sample.reshape(1, dec_batch * dec_seq, d)
    outs = [[] for _ in range(6)]
    for l in range(depth):
        w_in_l, w_out_l = _bf(w_in[l]), _bf(w_out[l])
        shared = (norm_mix[l], w_in_l, w_out_l, attn_sinks[l], rel_bias_table, hgrn_lb, hgrn_norm[l])
        xp, kp, vp, sp = _mixer(xp, None, *shared, layer=l, carry=True, tile=PROMPT_TILE, chunk=CHUNK,
                                hg_block=HGRN_BLOCK)
        hist = (cache_k[l].reshape(dec_batch, WINDOW, KV_WIDTH), cache_v[l].reshape(dec_batch, WINDOW, KV_WIDTH),
                state_hgrn[l])
        xs, ks, vs, ss = _mixer(xs, hist, *shared, layer=l, carry=False, tile=dec_batch * dec_seq, chunk=dec_seq,
                                hg_block=dec_seq)
        last = l == depth - 1
        xp, xs = _ffn(xp.reshape(batch * seq, d), xs.reshape(dec_batch * dec_seq, d), norm_ffn[l], w_gate_up[l],
                      w_down[l], norm_final, tile=FFN_TILE, final_norm=last)
        xp = xp.reshape(batch, seq, d)
        xs = xs.reshape(1, dec_batch * dec_seq, d)
        ks = jnp.concatenate([hist[0][:, dec_seq:], ks.reshape(dec_batch, dec_seq, KV_WIDTH)], axis=1)
        vs = jnp.concatenate([hist[1][:, dec_seq:], vs.reshape(dec_batch, dec_seq, KV_WIDTH)], axis=1)
        kv_shape = (-1, WINDOW, N_KV_HEADS, HEAD_DIM)
        for acc, val in zip(outs, (kp.reshape(kv_shape), vp.reshape(kv_shape), sp,
                                   ks.reshape(kv_shape), vs.reshape(kv_shape), ss)):
            acc.append(val)
    return (xp, xs.reshape(dec_batch, dec_seq, d)) + tuple(jnp.stack(o) for o in outs)
```

```python
import functools
import math

import numpy as np
import jax
import jax.numpy as jnp
from jax import lax
from jax.experimental import pallas as pl
from jax.experimental.pallas import tpu as pltpu

CHUNK = 64
N_HEADS = 16
N_KV_HEADS = 2
HEAD_DIM = 64
KV_WIDTH = N_KV_HEADS * HEAD_DIM
WINDOW = 128
ATTN_SCALE = HEAD_DIM ** -0.5
LOG2E = math.log2(math.e)
N_BUCKETS = 32
MAX_DISTANCE = 128
HG_DK = 128
RMS_EPS = 1e-6
DECAY_GUARD = 75.0

LANES = 128
SUBLANES = 8
VMEM_LIMIT_BYTES = 56 * 1024 * 1024
PROMPT_TILE = 4 * CHUNK
HGRN_BLOCK = 2 * CHUNK
FFN_TILE = 512

_NT = (((1,), (1,)), ((), ()))
_TN = (((0,), (0,)), ((), ()))


def _bf(x):
    return x.astype(jnp.bfloat16)


def _dot(a, b, dims=None):
    if dims is None:
        return jnp.dot(a, b, preferred_element_type=jnp.float32)
    return lax.dot_general(a, b, dims, preferred_element_type=jnp.float32)


def _sigmoid(x):
    return 0.5 * jnp.tanh(0.5 * x) + 0.5


def _rms(x, w):
    return x * lax.rsqrt(jnp.mean(x * x, axis=-1, keepdims=True) + RMS_EPS) * w


def _t5_bucket(rel):
    nb = N_BUCKETS // 2
    max_exact = nb // 2
    ret = jnp.where(rel > 0, nb, 0)
    n = jnp.abs(rel)
    nf = jnp.maximum(n, 1).astype(jnp.float32)
    large = max_exact + (jnp.log(nf / max_exact) / math.log(MAX_DISTANCE / max_exact)
                         * (nb - max_exact)).astype(jnp.int32)
    large = jnp.minimum(large, nb - 1)
    return ret + jnp.where(n < max_exact, n, large)


def _levels(c):
    out, m = [], c // 2
    while m >= 1:
        out.append(m)
        m //= 2
    return out


def _level_masks(c):
    t = np.arange(c)[:, None]
    s = np.arange(c)[None, :]
    masks = []
    for m in _levels(c):
        masks.append((t // (2 * m) == s // (2 * m)) & ((t // m) % 2 == 1) & ((s // m) % 2 == 0))
    masks += [t == s, s <= t]
    return np.stack(masks).astype(np.float32)


def _row_masks(c):
    r = np.arange(c)
    rows = [(r // m) % 2 == 1 for m in _levels(c) if m < 8]
    rows += [r % 4 == 0, r % 4 >= 2, r % 4 == 3]
    return np.repeat(np.stack(rows).astype(np.float32)[:, :, None], LANES, axis=2)


def _block_cumsum(x, block):
    r, n = x.shape
    tiles = x.reshape(r // SUBLANES, SUBLANES, n)
    row = lax.broadcasted_iota(jnp.int32, (1, SUBLANES, n), 1)
    for shift in (1, 2, 4):
        tiles = tiles + jnp.where(row >= shift, pltpu.roll(tiles, shift, axis=1), 0.0)
    out, before = [], None
    for i in range(r // SUBLANES):
        t = tiles[i] if before is None or i % (block // SUBLANES) == 0 else tiles[i] + before
        out.append(t)
        before = jnp.broadcast_to(t[SUBLANES - 1:SUBLANES, :], (SUBLANES, n))
    return jnp.concatenate(out, axis=0)


def _head_variants(a):
    lo = lax.broadcasted_iota(jnp.int32, a.shape, 1) < HEAD_DIM
    rot = pltpu.roll(a, HEAD_DIM, axis=1)
    return [_bf(jnp.where(lo, a, 0.0)), _bf(jnp.where(lo, 0.0, rot)),
            _bf(jnp.where(lo, rot, 0.0)), _bf(jnp.where(lo, 0.0, a))]


def _mixer_kernel(*refs, tile, chunk, hg_block, carry, d_model, layer):
    n_chunks = tile // chunk
    keys = WINDOW + chunk
    key_pad = 2 * LANES
    assert keys < key_pad and tile >= WINDOW
    n_hg = d_model // HG_DK
    att_w = N_HEADS * HEAD_DIM
    n_pairs = N_HEADS // 2
    ppk = n_pairs // N_KV_HEADS
    rows_g = ppk * chunk
    n_var = WINDOW // chunk + 1 if carry else 1
    f32 = jnp.float32

    refs = list(refs)
    (x_ref, nw_ref, win_ref, wout_ref, sinks_ref, relb_ref, lbp_ref, gn_ref,
     lmask_ref, hmask_ref) = refs[:10]
    refs = refs[10:]
    if not carry:
        hk_ref, hv_ref, si_ref = refs[:3]
        refs = refs[3:]
    y_ref, ko_ref, vo_ref, so_ref = refs[:4]
    refs = refs[4:]
    (bias_s, q4_s, kv_s, kx_s, vx_s, hq_s, kk_s, hi_s, bc_s, og_s, ga_s, gh_s, att_s, hg_s,
     u_s, sb_s, p_s, a_s, qe_s) = refs[:19]
    if carry:
        st_s = refs[19]

    first = (pl.program_id(0) == 0) & (pl.program_id(1) == 0)
    t_idx = pl.program_id(1)

    @pl.when(first)
    def _build_bias():
        lane = lax.broadcasted_iota(jnp.int32, (chunk, key_pad), 1)
        for head in range(N_HEADS):
            acc = jnp.where(lane == keys, sinks_ref[head], jnp.where(lane > keys, -jnp.inf, relb_ref[head])) * LOG2E
            pair, side = divmod(head, 2)
            kvh, j = divmod(pair, ppk)
            for var in range(n_var):
                n_invalid = WINDOW - var * chunk if carry else 0
                bias_s[var, 2 * kvh + side, j * chunk:(j + 1) * chunk, :] = jnp.where(lane < n_invalid, -jnp.inf, acc)

    if carry:
        @pl.when(t_idx == 0)
        def _reset():
            kx_s[:, 0, 0:WINDOW, :] = jnp.zeros((4, WINDOW, LANES), jnp.bfloat16)
            vx_s[:, 0, 0:WINDOW, :] = jnp.zeros((4, WINDOW, LANES), jnp.bfloat16)
            st_s[...] = jnp.zeros_like(st_s)

        @pl.when(t_idx > 0)
        def _shift():
            kx_s[:, 0, 0:WINDOW, :] = kx_s[:, 0, tile:tile + WINDOW, :]
            vx_s[:, 0, 0:WINDOW, :] = vx_s[:, 0, tile:tile + WINDOW, :]

    win_hbm, wout_hbm = win_ref, wout_ref
    win_ref, wout_ref, stage_s, wsem = refs[20:24] if carry else refs[19:23]
    slab = 2 * LANES
    n_in, n_out = win_hbm.shape[1] // slab, wout_hbm.shape[1] // slab

    def slab_copy(k):
        src = win_hbm.at[:, k * slab:(k + 1) * slab] if k < n_in else wout_hbm.at[:, (k - n_in) * slab:(k - n_in + 1) * slab]
        return pltpu.make_async_copy(src, stage_s.at[k % 2], wsem.at[k % 2])

    def load_weights():
        slab_copy(0).start()
        slab_copy(1).start()
        for k in range(n_in + n_out):
            slab_copy(k).wait()
            dst = win_ref if k < n_in else wout_ref
            j = k if k < n_in else k - n_in
            dst[:, j * slab:(j + 1) * slab] = _bf(stage_s[k % 2])
            if k + 2 < n_in + n_out:
                slab_copy(k + 2).start()

    if carry:
        pl.when(first)(load_weights)
    else:
        load_weights()

    x = x_ref[0]
    h = _bf(_rms(x, nw_ref[...]))

    def proj(i0, width):
        return _dot(h, win_ref[:, i0:i0 + width])

    def project_queries():
        q = _bf(proj(0, att_w) * (ATTN_SCALE * LOG2E))
        for c in range(n_chunks):
            for j in range(n_pairs):
                q4_s[c, j] = q[c * chunk:(c + 1) * chunk, j * LANES:(j + 1) * LANES]

    off = att_w
    kv = proj(off, 2 * KV_WIDTH)
    kv_s[...] = kv
    off += 2 * KV_WIDTH
    k_var = _head_variants(kv[:, 0:KV_WIDTH])
    v_var = _head_variants(kv[:, KV_WIDTH:2 * KV_WIDTH])
    if carry:
        for g in range(4):
            kx_s[g, 0, WINDOW:WINDOW + tile, :] = k_var[g]
            vx_s[g, 0, WINDOW:WINDOW + tile, :] = v_var[g]
    else:
        for c in range(n_chunks):
            hk = _head_variants(hk_ref[c])
            hv = _head_variants(hv_ref[c])
            for g in range(4):
                kx_s[g, c, 0:WINDOW, :] = hk[g]
                vx_s[g, c, 0:WINDOW, :] = hv[g]
                kx_s[g, c, WINDOW:keys, :] = k_var[g][c * chunk:(c + 1) * chunk]
                vx_s[g, c, WINDOW:keys, :] = v_var[g][c * chunk:(c + 1) * chunk]
    off_hq, off_f, off_i, off_og, off_ga, off_gh = (off + i * d_model for i in range(6))
    lbp = lbp_ref[...]
    e = jnp.exp(lbp - jnp.max(lbp, axis=0, keepdims=True))
    lb = jnp.sum(e[:layer + 1], axis=0, keepdims=True) / jnp.sum(e, axis=0, keepdims=True)

    def gate_quarter(i):
        cs = slice(i * d_model // 4, (i + 1) * d_model // 4)
        half_span = 0.5 * (1.0 - lb[:, cs])
        f = (lb[:, cs] + half_span) + half_span * jnp.tanh(0.5 * proj(off_f + cs.start, d_model // 4))
        kk_s[:, cs] = 1.0 - f
        bc_s[:, cs] = _block_cumsum(jnp.log(f), hg_block)

    hq_s[...] = proj(off_hq, d_model)
    gate_quarter(0)
    hi_s[...] = _bf(proj(off_i, d_model))
    gate_quarter(1)
    project_queries()
    gate_quarter(2)
    y_og = proj(off_og, d_model)
    og_s[...] = y_og * _sigmoid(y_og)
    gate_quarter(3)
    ga_s[...] = _sigmoid(proj(off_ga, d_model))
    gh_s[...] = _sigmoid(proj(off_gh, d_model))

    zpad = jnp.zeros((key_pad - keys, LANES), jnp.bfloat16)

    def window(ref, g, c):
        w = ref[g, 0, c * chunk:c * chunk + keys, :] if carry else ref[g, c]
        return jnp.concatenate([w, zpad], axis=0)

    for c in range(n_chunks):
        var = jnp.minimum(t_idx * n_chunks + c, n_var - 1) if carry else 0
        for kvh in range(N_KV_HEADS):
            qg = q4_s[c, kvh * ppk:(kvh + 1) * ppk].reshape(rows_g, LANES)
            for side in range(2):
                g = 2 * kvh + side
                s = _dot(qg, window(kx_s, g, c), _NT) + bias_s[var, g]
                p_s[c, g] = _bf(jnp.exp2(s - jnp.max(s, axis=-1, keepdims=True)))
    key_row = lax.broadcasted_iota(jnp.int32, (key_pad, LANES), 0)
    key_lane = lax.broadcasted_iota(jnp.int32, (key_pad, LANES), 1)
    ones_lo = _bf(jnp.where((key_row <= keys) & (key_lane < HEAD_DIM), 1.0, 0.0))
    ones_hi = _bf(jnp.where((key_row <= keys) & (key_lane >= HEAD_DIM), 1.0, 0.0))
    for c in range(n_chunks):
        for kvh in range(N_KV_HEADS):
            nd = (_dot(p_s[c, 2 * kvh], jnp.concatenate([window(vx_s, 2 * kvh, c), ones_lo], axis=1))
                  + _dot(p_s[c, 2 * kvh + 1], jnp.concatenate([window(vx_s, 2 * kvh + 1, c), ones_hi], axis=1)))
            o = nd[:, 0:LANES] * (1.0 / nd[:, LANES:2 * LANES])
            for j in range(ppk):
                col = (kvh * ppk + j) * LANES
                att_s[c * chunk:(c + 1) * chunk, col:col + LANES] = o[j * chunk:(j + 1) * chunk]

    blk, n_blk = hg_block, tile // hg_block
    levels = _levels(blk)
    gn = gn_ref[...]
    small = [m for m in levels if m < 8]

    def unit(c, hd):
        return slice(c * blk, (c + 1) * blk), slice(hd * HG_DK, (hd + 1) * HG_DK)

    def bc_row(c, r, cols):
        return bc_s[c * blk + r:c * blk + r + 1, cols]

    def level_operand(m, c, hd):
        rows, cols = unit(c, hd)
        q = hq_s[rows, cols]
        kk = kk_s[rows, cols]
        if m >= 8:
            bc = bc_s[rows, cols]
            parts = []
            for base in range(0, blk, 2 * m):
                ref = bc_row(c, base + m - 1, cols)
                lo, hi = slice(base, base + m), slice(base + m, base + 2 * m)
                parts += [kk[lo] * jnp.exp(ref - bc[lo]), q[hi] * jnp.exp(bc[hi] - ref)]
            return _bf(jnp.concatenate(parts, axis=0))
        upper = hmask_ref[small.index(m)] > 0.5
        if m == 1:
            return _bf(jnp.where(upper, q * (1.0 - kk), kk))
        if m == 4:
            bc = bc_s[rows, cols]
            ref_row = bc.reshape(blk // 8, 8, HG_DK)[:, 3:4, :]
            ref = jnp.broadcast_to(ref_row, (blk // 8, 8, HG_DK)).reshape(blk, HG_DK)
            arg = -jnp.abs(bc - ref)
        else:
            lfh = jnp.log(1.0 - kk)
            nxt = pltpu.roll(lfh, blk - 1, axis=0)
            prv = pltpu.roll(lfh, 1, axis=0)
            n_small = len(small)
            arg = hmask_ref[n_small] * nxt + hmask_ref[n_small + 1] * lfh + hmask_ref[n_small + 2] * prv
        return _bf(jnp.where(upper, q, kk) * jnp.exp(arg))

    tril = lmask_ref[len(levels) + 1] > 0.5
    for c in range(n_blk):
        for hd in range(n_hg):
            rows, cols = unit(c, hd)
            decay_in = jnp.exp(bc_s[rows, cols])
            qe = _bf(hq_s[rows, cols] * decay_in)
            qe_s[rows, cols] = qe
            k_grow_t = _bf((kk_s[rows, cols] * (1.0 / decay_in)).T)
            a_s[c, hd] = _bf(jnp.where(tril, _dot(qe, k_grow_t), 0.0))

    for c in range(n_blk):
        for hd in range(n_hg):
            rows, cols = unit(c, hd)
            k_dec = _bf(kk_s[rows, cols] * jnp.exp(bc_row(c, blk - 1, cols) - bc_s[rows, cols]))
            u_s[c, hd] = _dot(hi_s[rows, cols], k_dec, _TN)

    def readout(c):
        for hd in range(n_hg):
            rows, cols = unit(c, hd)
            o = _dot(qe_s[rows, cols], sb_s[c, hd]) + _dot(a_s[c, hd], hi_s[rows, cols])
            o = o * lax.rsqrt(jnp.mean(o * o, axis=-1, keepdims=True) + RMS_EPS) * gn
            hg_s[rows, cols] = o * og_s[rows, cols]

    def merge_and_project():
        merged = _bf(ga_s[...] * att_s[...] + gh_s[...] * hg_s[...])
        y_ref[0] = x_ref[0] + _dot(merged, wout_ref[...])

    if carry:
        states = [st_s[hd] for hd in range(n_hg)]
    for c in range(n_blk):
        for hd in range(n_hg):
            sb_s[c, hd] = _bf(states[hd].T) if carry else _bf(si_ref[c, hd])
        readout(c)
        for hd in range(n_hg):
            decay = jnp.exp(bc_row(c, blk - 1, unit(c, hd)[1]))
            if carry:
                states[hd] = states[hd] * decay + u_s[c, hd]
            else:
                so_ref[c, hd] = (si_ref[c, hd].T * decay + u_s[c, hd]).T
    if carry:
        for hd in range(n_hg):
            st_s[hd] = states[hd]
    merge_and_project()

    total_decay = jnp.concatenate([bc_row(c, blk - 1, slice(None)) for c in range(n_blk)], axis=0)

    @pl.when(jnp.logical_not(jnp.max(-total_decay) <= DECAY_GUARD))
    def _split():
        for c in range(n_blk):
            for hd in range(n_hg):
                rows, cols = unit(c, hd)
                a = _dot(_bf(hq_s[rows, cols]), _bf(kk_s[rows, cols]), _NT) * lmask_ref[len(levels)]
                for li, m in enumerate(levels):
                    z = level_operand(m, c, hd)
                    a = a + _dot(z, z, _NT) * lmask_ref[li]
                a_s[c, hd] = _bf(a)
        for c in range(n_blk):
            readout(c)
        merge_and_project()

    if carry:
        @pl.when(t_idx == pl.num_programs(1) - 1)
        def _emit():
            ko_ref[0] = kv_s[tile - WINDOW:tile, 0:KV_WIDTH]
            vo_ref[0] = kv_s[tile - WINDOW:tile, KV_WIDTH:2 * KV_WIDTH]
            for hd in range(n_hg):
                so_ref[0, hd] = st_s[hd].T
    else:
        ko_ref[0] = kv_s[:, 0:KV_WIDTH]
        vo_ref[0] = kv_s[:, KV_WIDTH:2 * KV_WIDTH]


def _full_spec(shape):
    nd = len(shape)
    return pl.BlockSpec(shape, lambda *_: (0,) * nd)


def _weight_spec(shape):
    nd = len(shape)
    return pl.BlockSpec(shape, lambda *_: (0,) * nd, pipeline_mode=pl.Buffered(1))


def _mixer(x, hist, norm_w, w_in, w_out, sinks, table, lb_params, gnorm_w, *, layer, carry, tile, chunk,
           hg_block):
    nb, s, d = x.shape
    n_t = s // tile
    n_chunks = tile // chunk
    n_blk = tile // hg_block
    assert carry or hg_block == chunk
    keys = WINDOW + chunk
    key_pad = 2 * LANES
    n_hg = d // HG_DK
    n_streams = nb if carry else s // chunk
    n_off = chunk + key_pad - 1
    offsets = jnp.arange(n_off, dtype=jnp.int32) - (chunk - 1) - WINDOW
    by_offset = table[_t5_bucket(offsets)].astype(jnp.float32).T
    period = jnp.roll(jnp.pad(by_offset, ((0, 0), (0, 1))), -(chunk - 1), axis=1)
    skewed = jnp.tile(period, (1, chunk))[:, :chunk * n_off].reshape(N_HEADS, chunk, n_off)
    rel_bias = skewed[:, :, :key_pad]
    lmask = jnp.asarray(_level_masks(hg_block))
    hmask = jnp.asarray(_row_masks(hg_block))
    smem = pl.BlockSpec(memory_space=pltpu.SMEM)

    in_arrays = [x, norm_w.reshape(1, d), w_in, w_out, sinks, rel_bias, lb_params,
                 gnorm_w.reshape(1, HG_DK), lmask, hmask]
    w_specs = [pl.BlockSpec(memory_space=pl.ANY)] * 2
    in_specs = [pl.BlockSpec((1, tile, d), lambda b, t: (b, t, 0)), _full_spec((1, d)),
                *w_specs, smem, _full_spec(rel_bias.shape),
                _full_spec(lb_params.shape), _full_spec((1, HG_DK)),
                _full_spec(lmask.shape), _full_spec(hmask.shape)]
    if carry:
        kv_rows = WINDOW
        st_block = (1, n_hg, HG_DK, HG_DK)
        n_win, win_rows, n_var = 1, WINDOW + tile, WINDOW // chunk + 1
    else:
        in_arrays += list(hist)
        in_specs += [_full_spec(a.shape) for a in hist]
        kv_rows = s
        st_block = (n_streams, n_hg, HG_DK, HG_DK)
        n_win, win_rows, n_var = n_chunks, keys, 1
    out_shape = [jax.ShapeDtypeStruct((nb, s, d), jnp.float32),
                 jax.ShapeDtypeStruct((nb, kv_rows, KV_WIDTH), jnp.float32),
                 jax.ShapeDtypeStruct((nb, kv_rows, KV_WIDTH), jnp.float32),
                 jax.ShapeDtypeStruct((n_streams, n_hg, HG_DK, HG_DK), jnp.float32)]
    out_specs = [pl.BlockSpec((1, tile, d), lambda b, t: (b, t, 0)),
                 pl.BlockSpec((1, kv_rows, KV_WIDTH), lambda b, t: (b, 0, 0)),
                 pl.BlockSpec((1, kv_rows, KV_WIDTH), lambda b, t: (b, 0, 0)),
                 pl.BlockSpec(st_block, lambda b, t: (b, 0, 0, 0))]
    f32, bf16 = jnp.float32, jnp.bfloat16
    rows_g = (N_HEADS // 2 // N_KV_HEADS) * chunk
    scratch = [pltpu.VMEM((n_var, 4, rows_g, key_pad), f32),
               pltpu.VMEM((n_chunks, N_HEADS // 2, chunk, LANES), bf16),
               pltpu.VMEM((tile, 2 * KV_WIDTH), f32),
               pltpu.VMEM((4, n_win, win_rows, LANES), bf16),
               pltpu.VMEM((4, n_win, win_rows, LANES), bf16),
               pltpu.VMEM((tile, d), f32), pltpu.VMEM((tile, d), f32),
               pltpu.VMEM((tile, d), bf16)]
    scratch += [pltpu.VMEM((tile, d), f32) for _ in range(6)]
    scratch += [pltpu.VMEM((n_blk, n_hg, HG_DK, HG_DK), f32),
                pltpu.VMEM((n_blk, n_hg, HG_DK, HG_DK), bf16),
                pltpu.VMEM((n_chunks, 4, rows_g, key_pad), bf16),
                pltpu.VMEM((n_blk, n_hg, hg_block, hg_block), bf16),
                pltpu.VMEM((tile, d), bf16)]
    if carry:
        scratch += [pltpu.VMEM((n_hg, HG_DK, HG_DK), f32)]
    scratch += [pltpu.VMEM(w_in.shape, bf16), pltpu.VMEM(w_out.shape, bf16),
                pltpu.VMEM((2, d, 2 * LANES), w_in.dtype),
                pltpu.SemaphoreType.DMA((2,))]
    kern = functools.partial(_mixer_kernel, tile=tile, chunk=chunk, hg_block=hg_block, carry=carry, d_model=d,
                             layer=layer)
    return pl.pallas_call(
        kern,
        grid=(nb, n_t),
        in_specs=in_specs,
        out_specs=out_specs,
        out_shape=out_shape,
        scratch_shapes=scratch,
        compiler_params=pltpu.CompilerParams(
            dimension_semantics=("arbitrary", "arbitrary"), vmem_limit_bytes=VMEM_LIMIT_BYTES),
        name="mixer_prompt" if carry else "mixer_sample",
    )(*in_arrays)


def _ffn_kernel(xa_ref, xb_ref, nw_ref, wgu_hbm, wd_hbm, fw_ref, ya_ref, yb_ref, act_s, wgu_ref, wd_ref, sems, *,
                d_ff, col_tile, final_norm):
    step = pl.program_id(0)
    n_a = pl.num_programs(0) - 1
    n_cols = d_ff // col_tile

    def slab_copy(j, half):
        cols = slice(half * d_ff + j * col_tile, half * d_ff + (j + 1) * col_tile)
        return pltpu.make_async_copy(wgu_hbm.at[:, cols], wgu_ref.at[:, cols], sems.at[2 * j + half])

    def down_copy():
        return pltpu.make_async_copy(wd_hbm, wd_ref, sems.at[2 * n_cols])

    def ffn(x_ref, y_ref, weights_arriving):
        rows = x_ref.shape[0]
        x = x_ref[...]
        h = _bf(x * nw_ref[...])
        r = lax.rsqrt(jnp.mean(x * x, axis=-1, keepdims=True) + RMS_EPS)
        for j in range(n_cols):
            if weights_arriving:
                slab_copy(j, 0).wait()
                slab_copy(j, 1).wait()
            g = r * _dot(h, _bf(wgu_ref[:, j * col_tile:(j + 1) * col_tile]))
            u = r * _dot(h, _bf(wgu_ref[:, d_ff + j * col_tile:d_ff + (j + 1) * col_tile]))
            act_s[0:rows, j * col_tile:(j + 1) * col_tile] = _bf(jax.nn.silu(g) * u)
        if weights_arriving:
            down_copy().wait()
        y = x + _dot(act_s[0:rows, :], _bf(wd_ref[...]))
        if final_norm:
            y = _rms(y, fw_ref[...])
        y_ref[...] = y

    @pl.when(step == 0)
    def _first_tile():
        for j in range(n_cols):
            slab_copy(j, 0).start()
            slab_copy(j, 1).start()
        down_copy().start()
        ffn(xa_ref, ya_ref, True)

    @pl.when((step > 0) & (step < n_a))
    def _first_set():
        ffn(xa_ref, ya_ref, False)

    @pl.when(step == n_a)
    def _second_set():
        ffn(xb_ref, yb_ref, False)


def _ffn(xa, xb, norm_w, w_gate_up, w_down, final_w, *, tile, final_norm):
    n, d = xa.shape
    m = xb.shape[0]
    n_a = n // tile
    assert n == n_a * tile and n_a >= 1 and m <= tile
    d_ff = w_down.shape[0]
    col_tile = 2 * LANES
    kern = functools.partial(_ffn_kernel, d_ff=d_ff, col_tile=col_tile, final_norm=final_norm)
    a_spec = pl.BlockSpec((tile, d), lambda i: (jnp.minimum(i, n_a - 1), 0))
    in_hbm = pl.BlockSpec(memory_space=pl.ANY)
    return pl.pallas_call(
        kern,
        grid=(n_a + 1,),
        in_specs=[a_spec, _full_spec((m, d)), _full_spec((1, d)), in_hbm, in_hbm, _full_spec((1, d))],
        out_specs=[a_spec, _full_spec((m, d))],
        out_shape=[jax.ShapeDtypeStruct((n, d), jnp.float32), jax.ShapeDtypeStruct((m, d), jnp.float32)],
        scratch_shapes=[pltpu.VMEM((tile, d_ff), jnp.bfloat16),
                        pltpu.VMEM(w_gate_up.shape, w_gate_up.dtype), pltpu.VMEM(w_down.shape, w_down.dtype),
                        pltpu.SemaphoreType.DMA((2 * (d_ff // col_tile) + 1,))],
        compiler_params=pltpu.CompilerParams(
            dimension_semantics=("arbitrary",), vmem_limit_bytes=VMEM_LIMIT_BYTES),
        name="ffn",
    )(xa, xb, norm_w.reshape(1, d), w_gate_up, w_down, final_w.reshape(1, d))


def kernel(x_prompt, x_sample, cache_k, cache_v, state_hgrn, norm_mix, w_in, w_out, attn_sinks, rel_bias_table,
           hgrn_lb, hgrn_norm, norm_ffn, w_gate_up, w_down, norm_final):
    depth = w_in.shape[0]
    batch, seq, d = x_prompt.shape
    dec_batch, dec_seq, _ = x_sample.shape
    assert w_in.shape[2] == N_HEADS * HEAD_DIM + 2 * KV_WIDTH + 6 * d and KV_WIDTH == LANES and d % (4 * LANES) == 0
    assert seq % PROMPT_TILE == 0 and (batch * seq) % FFN_TILE == 0 and dec_batch * dec_seq <= FFN_TILE
    assert dec_seq % (2 * SUBLANES) == 0 and cache_k.shape[2] == WINDOW
    xp = x_prompt
    xs = x_sample.reshape(1, dec_batch * dec_seq, d)
    outs = [[] for _ in range(6)]
    for l in range(depth):
        shared = (norm_mix[l], w_in[l], w_out[l], attn_sinks[l], rel_bias_table, hgrn_lb, hgrn_norm[l])
        xp, kp, vp, sp = _mixer(xp, None, *shared, layer=l, carry=True, tile=PROMPT_TILE, chunk=CHUNK,
                                hg_block=HGRN_BLOCK)
        hist = (cache_k[l].reshape(dec_batch, WINDOW, KV_WIDTH), cache_v[l].reshape(dec_batch, WINDOW, KV_WIDTH),
                state_hgrn[l])
        xs, ks, vs, ss = _mixer(xs, hist, *shared, layer=l, carry=False, tile=dec_batch * dec_seq, chunk=dec_seq,
                                hg_block=dec_seq)
        last = l == depth - 1
        xp, xs = _ffn(xp.reshape(batch * seq, d), xs.reshape(dec_batch * dec_seq, d), norm_ffn[l], w_gate_up[l],
                      w_down[l], norm_final, tile=FFN_TILE, final_norm=last)
        xp = xp.reshape(batch, seq, d)
        xs = xs.reshape(1, dec_batch * dec_seq, d)
        ks = jnp.concatenate([hist[0][:, dec_seq:], ks.reshape(dec_batch, dec_seq, KV_WIDTH)], axis=1)
        vs = jnp.concatenate([hist[1][:, dec_seq:], vs.reshape(dec_batch, dec_seq, KV_WIDTH)], axis=1)
        kv_shape = (-1, WINDOW, N_KV_HEADS, HEAD_DIM)
        for acc, val in zip(outs, (kp.reshape(kv_shape), vp.reshape(kv_shape), sp,
                                   ks.reshape(kv_shape), vs.reshape(kv_shape), ss)):
            acc.append(val)
    return (xp, xs.reshape(dec_batch, dec_seq, d)) + tuple(jnp.stack(o) for o in outs)
```

```python
import functools
import math

import numpy as np
import jax
import jax.numpy as jnp
from jax import lax
from jax.experimental import pallas as pl
from jax.experimental.pallas import tpu as pltpu

CHUNK = 64
N_HEADS = 16
N_KV_HEADS = 2
HEAD_DIM = 64
KV_WIDTH = N_KV_HEADS * HEAD_DIM
WINDOW = 128
ATTN_SCALE = HEAD_DIM ** -0.5
LOG2E = math.log2(math.e)
N_BUCKETS = 32
MAX_DISTANCE = 128
HG_DK = 128
RMS_EPS = 1e-6
DECAY_GUARD = 75.0

LANES = 128
SUBLANES = 8
VMEM_LIMIT_BYTES = 56 * 1024 * 1024
PROMPT_TILE = 4 * CHUNK
HGRN_BLOCK = 2 * CHUNK
FFN_TILE = 512
SLABS_AHEAD = 3
N_IN_SLABS = 11

_NT = (((1,), (1,)), ((), ()))
_TN = (((0,), (0,)), ((), ()))


def _bf(x):
    return x.astype(jnp.bfloat16)


def _dot(a, b, dims=None):
    if dims is None:
        return jnp.dot(a, b, preferred_element_type=jnp.float32)
    return lax.dot_general(a, b, dims, preferred_element_type=jnp.float32)


def _sigmoid(x):
    return 0.5 * jnp.tanh(0.5 * x) + 0.5


def _rms(x, w):
    return x * lax.rsqrt(jnp.mean(x * x, axis=-1, keepdims=True) + RMS_EPS) * w


def _t5_bucket(rel):
    nb = N_BUCKETS // 2
    max_exact = nb // 2
    ret = jnp.where(rel > 0, nb, 0)
    n = jnp.abs(rel)
    nf = jnp.maximum(n, 1).astype(jnp.float32)
    large = max_exact + (jnp.log(nf / max_exact) / math.log(MAX_DISTANCE / max_exact)
                         * (nb - max_exact)).astype(jnp.int32)
    large = jnp.minimum(large, nb - 1)
    return ret + jnp.where(n < max_exact, n, large)


def _levels(c):
    out, m = [], c // 2
    while m >= 1:
        out.append(m)
        m //= 2
    return out


def _level_masks(c):
    t = np.arange(c)[:, None]
    s = np.arange(c)[None, :]
    masks = []
    for m in _levels(c):
        masks.append((t // (2 * m) == s // (2 * m)) & ((t // m) % 2 == 1) & ((s // m) % 2 == 0))
    masks += [t == s, s <= t]
    return np.stack(masks).astype(np.float32)


def _row_masks(c):
    r = np.arange(c)
    rows = [(r // m) % 2 == 1 for m in _levels(c) if m < 8]
    rows += [r % 4 == 0, r % 4 >= 2, r % 4 == 3]
    return np.repeat(np.stack(rows).astype(np.float32)[:, :, None], LANES, axis=2)


def _block_cumsum(x, block):
    r, n = x.shape
    tiles = x.reshape(r // SUBLANES, SUBLANES, n)
    row = lax.broadcasted_iota(jnp.int32, (1, SUBLANES, n), 1)
    for shift in (1, 2, 4):
        tiles = tiles + jnp.where(row >= shift, pltpu.roll(tiles, shift, axis=1), 0.0)
    out, before = [], None
    for i in range(r // SUBLANES):
        t = tiles[i] if before is None or i % (block // SUBLANES) == 0 else tiles[i] + before
        out.append(t)
        before = jnp.broadcast_to(t[SUBLANES - 1:SUBLANES, :], (SUBLANES, n))
    return jnp.concatenate(out, axis=0)


def _head_variants(a):
    lo = lax.broadcasted_iota(jnp.int32, a.shape, 1) < HEAD_DIM
    rot = pltpu.roll(a, HEAD_DIM, axis=1)
    return [_bf(jnp.where(lo, a, 0.0)), _bf(jnp.where(lo, 0.0, rot)),
            _bf(jnp.where(lo, rot, 0.0)), _bf(jnp.where(lo, 0.0, a))]


def _mixer_kernel(*refs, tile, chunk, hg_block, carry, d_model, layer):
    n_chunks = tile // chunk
    keys = WINDOW + chunk
    key_pad = 2 * LANES
    assert keys < key_pad and tile >= WINDOW
    n_hg = d_model // HG_DK
    att_w = N_HEADS * HEAD_DIM
    n_pairs = N_HEADS // 2
    ppk = n_pairs // N_KV_HEADS
    rows_g = ppk * chunk
    n_var = WINDOW // chunk + 1 if carry else 1
    f32 = jnp.float32

    refs = list(refs)
    (x_ref, nw_ref, win_ref, wout_ref, sinks_ref, relb_ref, lbp_ref, gn_ref,
     lmask_ref, hmask_ref) = refs[:10]
    refs = refs[10:]
    if not carry:
        hk_ref, hv_ref, si_ref = refs[:3]
        refs = refs[3:]
    y_ref, ko_ref, vo_ref, so_ref = refs[:4]
    refs = refs[4:]
    (bias_s, q4_s, kv_s, kx_s, vx_s, hq_s, kk_s, hi_s, bc_s, og_s, ga_s, gh_s, att_s, hg_s,
     u_s, sb_s, p_s, a_s, qe_s) = refs[:19]
    if carry:
        st_s = refs[19]

    first = (pl.program_id(0) == 0) & (pl.program_id(1) == 0)
    t_idx = pl.program_id(1)

    @pl.when(first)
    def _build_bias():
        lane = lax.broadcasted_iota(jnp.int32, (chunk, key_pad), 1)
        for head in range(N_HEADS):
            acc = jnp.where(lane == keys, sinks_ref[head], jnp.where(lane > keys, -jnp.inf, relb_ref[head])) * LOG2E
            pair, side = divmod(head, 2)
            kvh, j = divmod(pair, ppk)
            for var in range(n_var):
                n_invalid = WINDOW - var * chunk if carry else 0
                bias_s[var, 2 * kvh + side, j * chunk:(j + 1) * chunk, :] = jnp.where(lane < n_invalid, -jnp.inf, acc)

    if carry:
        @pl.when(t_idx == 0)
        def _reset():
            kx_s[:, 0, 0:WINDOW, :] = jnp.zeros((4, WINDOW, LANES), jnp.bfloat16)
            vx_s[:, 0, 0:WINDOW, :] = jnp.zeros((4, WINDOW, LANES), jnp.bfloat16)
            st_s[...] = jnp.zeros_like(st_s)

        @pl.when(t_idx > 0)
        def _shift():
            kx_s[:, 0, 0:WINDOW, :] = kx_s[:, 0, tile:tile + WINDOW, :]
            vx_s[:, 0, 0:WINDOW, :] = vx_s[:, 0, tile:tile + WINDOW, :]

    if not carry:
        win_hbm, wout_hbm = win_ref, wout_ref
        win_ref, wout_ref, wsem = refs[19:22]
        quarter = d_model // 4
        base = att_w + 2 * KV_WIDTH
        slabs = [(att_w, 2 * KV_WIDTH), (base, d_model), (base + d_model, quarter),
                 (base + 2 * d_model, d_model), (base + d_model + quarter, quarter), (0, att_w),
                 (base + d_model + 2 * quarter, quarter), (base + 3 * d_model, d_model),
                 (base + d_model + 3 * quarter, quarter), (base + 4 * d_model, d_model), (base + 5 * d_model, d_model)]

        def slab_copy(k):
            cols = slice(slabs[k][0], slabs[k][0] + slabs[k][1])
            return pltpu.make_async_copy(win_hbm.at[:, cols], win_ref.at[:, cols], wsem.at[k])

        def out_copy():
            return pltpu.make_async_copy(wout_hbm, wout_ref, wsem.at[len(slabs)])

        assert len(slabs) == N_IN_SLABS
        for k in range(len(slabs)):
            slab_copy(k).start()
        out_copy().start()

    x = x_ref[0]
    h = _bf(_rms(x, nw_ref[...]))

    def proj(i0, width):
        if not carry:
            slab_copy(slabs.index((i0, width))).wait()
        return _dot(h, win_ref[:, i0:i0 + width])

    def project_queries():
        q = _bf(proj(0, att_w) * (ATTN_SCALE * LOG2E))
        for c in range(n_chunks):
            for j in range(n_pairs):
                q4_s[c, j] = q[c * chunk:(c + 1) * chunk, j * LANES:(j + 1) * LANES]

    off = att_w
    kv = proj(off, 2 * KV_WIDTH)
    kv_s[...] = kv
    off += 2 * KV_WIDTH
    k_var = _head_variants(kv[:, 0:KV_WIDTH])
    v_var = _head_variants(kv[:, KV_WIDTH:2 * KV_WIDTH])
    if carry:
        for g in range(4):
            kx_s[g, 0, WINDOW:WINDOW + tile, :] = k_var[g]
            vx_s[g, 0, WINDOW:WINDOW + tile, :] = v_var[g]
    else:
        for c in range(n_chunks):
            hk = _head_variants(hk_ref[c])
            hv = _head_variants(hv_ref[c])
            for g in range(4):
                kx_s[g, c, 0:WINDOW, :] = hk[g]
                vx_s[g, c, 0:WINDOW, :] = hv[g]
                kx_s[g, c, WINDOW:keys, :] = k_var[g][c * chunk:(c + 1) * chunk]
                vx_s[g, c, WINDOW:keys, :] = v_var[g][c * chunk:(c + 1) * chunk]
    off_hq, off_f, off_i, off_og, off_ga, off_gh = (off + i * d_model for i in range(6))
    lbp = lbp_ref[...]
    e = jnp.exp(lbp - jnp.max(lbp, axis=0, keepdims=True))
    lb = jnp.sum(e[:layer + 1], axis=0, keepdims=True) / jnp.sum(e, axis=0, keepdims=True)

    def gate_quarter(i):
        cs = slice(i * d_model // 4, (i + 1) * d_model // 4)
        half_span = 0.5 * (1.0 - lb[:, cs])
        f = (lb[:, cs] + half_span) + half_span * jnp.tanh(0.5 * proj(off_f + cs.start, d_model // 4))
        kk_s[:, cs] = 1.0 - f
        bc_s[:, cs] = _block_cumsum(jnp.log(f), hg_block)

    hq_s[...] = proj(off_hq, d_model)
    gate_quarter(0)
    hi_s[...] = _bf(proj(off_i, d_model))
    gate_quarter(1)
    project_queries()
    gate_quarter(2)
    y_og = proj(off_og, d_model)
    og_s[...] = y_og * _sigmoid(y_og)
    gate_quarter(3)
    ga_s[...] = _sigmoid(proj(off_ga, d_model))
    gh_s[...] = _sigmoid(proj(off_gh, d_model))

    zpad = jnp.zeros((key_pad - keys, LANES), jnp.bfloat16)

    def window(ref, g, c):
        w = ref[g, 0, c * chunk:c * chunk + keys, :] if carry else ref[g, c]
        return jnp.concatenate([w, zpad], axis=0)

    for c in range(n_chunks):
        var = jnp.minimum(t_idx * n_chunks + c, n_var - 1) if carry else 0
        for kvh in range(N_KV_HEADS):
            qg = q4_s[c, kvh * ppk:(kvh + 1) * ppk].reshape(rows_g, LANES)
            for side in range(2):
                g = 2 * kvh + side
                s = _dot(qg, window(kx_s, g, c), _NT) + bias_s[var, g]
                p_s[c, g] = _bf(jnp.exp2(s - jnp.max(s, axis=-1, keepdims=True)))
    key_row = lax.broadcasted_iota(jnp.int32, (key_pad, LANES), 0)
    key_lane = lax.broadcasted_iota(jnp.int32, (key_pad, LANES), 1)
    ones_lo = _bf(jnp.where((key_row <= keys) & (key_lane < HEAD_DIM), 1.0, 0.0))
    ones_hi = _bf(jnp.where((key_row <= keys) & (key_lane >= HEAD_DIM), 1.0, 0.0))
    for c in range(n_chunks):
        for kvh in range(N_KV_HEADS):
            nd = (_dot(p_s[c, 2 * kvh], jnp.concatenate([window(vx_s, 2 * kvh, c), ones_lo], axis=1))
                  + _dot(p_s[c, 2 * kvh + 1], jnp.concatenate([window(vx_s, 2 * kvh + 1, c), ones_hi], axis=1)))
            o = nd[:, 0:LANES] * (1.0 / nd[:, LANES:2 * LANES])
            for j in range(ppk):
                col = (kvh * ppk + j) * LANES
                att_s[c * chunk:(c + 1) * chunk, col:col + LANES] = o[j * chunk:(j + 1) * chunk]

    blk, n_blk = hg_block, tile // hg_block
    levels = _levels(blk)
    gn = gn_ref[...]
    small = [m for m in levels if m < 8]

    def unit(c, hd):
        return slice(c * blk, (c + 1) * blk), slice(hd * HG_DK, (hd + 1) * HG_DK)

    def bc_row(c, r, cols):
        return bc_s[c * blk + r:c * blk + r + 1, cols]

    def level_operand(m, c, hd):
        rows, cols = unit(c, hd)
        q = hq_s[rows, cols]
        kk = kk_s[rows, cols]
        if m >= 8:
            bc = bc_s[rows, cols]
            parts = []
            for base in range(0, blk, 2 * m):
                ref = bc_row(c, base + m - 1, cols)
                lo, hi = slice(base, base + m), slice(base + m, base + 2 * m)
                parts += [kk[lo] * jnp.exp(ref - bc[lo]), q[hi] * jnp.exp(bc[hi] - ref)]
            return _bf(jnp.concatenate(parts, axis=0))
        upper = hmask_ref[small.index(m)] > 0.5
        if m == 1:
            return _bf(jnp.where(upper, q * (1.0 - kk), kk))
        if m == 4:
            bc = bc_s[rows, cols]
            ref_row = bc.reshape(blk // 8, 8, HG_DK)[:, 3:4, :]
            ref = jnp.broadcast_to(ref_row, (blk // 8, 8, HG_DK)).reshape(blk, HG_DK)
            arg = -jnp.abs(bc - ref)
        else:
            lfh = jnp.log(1.0 - kk)
            nxt = pltpu.roll(lfh, blk - 1, axis=0)
            prv = pltpu.roll(lfh, 1, axis=0)
            n_small = len(small)
            arg = hmask_ref[n_small] * nxt + hmask_ref[n_small + 1] * lfh + hmask_ref[n_small + 2] * prv
        return _bf(jnp.where(upper, q, kk) * jnp.exp(arg))

    tril = lmask_ref[len(levels) + 1] > 0.5
    for c in range(n_blk):
        for hd in range(n_hg):
            rows, cols = unit(c, hd)
            decay_in = jnp.exp(bc_s[rows, cols])
            qe = _bf(hq_s[rows, cols] * decay_in)
            qe_s[rows, cols] = qe
            k_grow_t = _bf((kk_s[rows, cols] * (1.0 / decay_in)).T)
            a_s[c, hd] = _bf(jnp.where(tril, _dot(qe, k_grow_t), 0.0))

    for c in range(n_blk):
        for hd in range(n_hg):
            rows, cols = unit(c, hd)
            k_dec = _bf(kk_s[rows, cols] * jnp.exp(bc_row(c, blk - 1, cols) - bc_s[rows, cols]))
            u_s[c, hd] = _dot(hi_s[rows, cols], k_dec, _TN)

    def readout(c):
        for hd in range(n_hg):
            rows, cols = unit(c, hd)
            o = _dot(qe_s[rows, cols], sb_s[c, hd]) + _dot(a_s[c, hd], hi_s[rows, cols])
            o = o * lax.rsqrt(jnp.mean(o * o, axis=-1, keepdims=True) + RMS_EPS) * gn
            hg_s[rows, cols] = o * og_s[rows, cols]

    def merge_and_project():
        merged = _bf(ga_s[...] * att_s[...] + gh_s[...] * hg_s[...])
        y_ref[0] = x_ref[0] + _dot(merged, wout_ref[...])

    if carry:
        states = [st_s[hd] for hd in range(n_hg)]
    for c in range(n_blk):
        for hd in range(n_hg):
            sb_s[c, hd] = _bf(states[hd].T) if carry else _bf(si_ref[c, hd])
        readout(c)
        for hd in range(n_hg):
            decay = jnp.exp(bc_row(c, blk - 1, unit(c, hd)[1]))
            if carry:
                states[hd] = states[hd] * decay + u_s[c, hd]
            else:
                so_ref[c, hd] = (si_ref[c, hd].T * decay + u_s[c, hd]).T
    if carry:
        for hd in range(n_hg):
            st_s[hd] = states[hd]
    else:
        out_copy().wait()
    merge_and_project()

    total_decay = jnp.concatenate([bc_row(c, blk - 1, slice(None)) for c in range(n_blk)], axis=0)

    @pl.when(jnp.logical_not(jnp.max(-total_decay) <= DECAY_GUARD))
    def _split():
        for c in range(n_blk):
            for hd in range(n_hg):
                rows, cols = unit(c, hd)
                a = _dot(_bf(hq_s[rows, cols]), _bf(kk_s[rows, cols]), _NT) * lmask_ref[len(levels)]
                for li, m in enumerate(levels):
                    z = level_operand(m, c, hd)
                    a = a + _dot(z, z, _NT) * lmask_ref[li]
                a_s[c, hd] = _bf(a)
        for c in range(n_blk):
            readout(c)
        merge_and_project()

    if carry:
        @pl.when(t_idx == pl.num_programs(1) - 1)
        def _emit():
            ko_ref[0] = kv_s[tile - WINDOW:tile, 0:KV_WIDTH]
            vo_ref[0] = kv_s[tile - WINDOW:tile, KV_WIDTH:2 * KV_WIDTH]
            for hd in range(n_hg):
                so_ref[0, hd] = st_s[hd].T
    else:
        ko_ref[0] = kv_s[:, 0:KV_WIDTH]
        vo_ref[0] = kv_s[:, KV_WIDTH:2 * KV_WIDTH]


def _full_spec(shape):
    nd = len(shape)
    return pl.BlockSpec(shape, lambda *_: (0,) * nd)


def _weight_spec(shape):
    nd = len(shape)
    return pl.BlockSpec(shape, lambda *_: (0,) * nd, pipeline_mode=pl.Buffered(1))


def _mixer(x, hist, norm_w, w_in, w_out, sinks, table, lb_params, gnorm_w, *, layer, carry, tile, chunk,
           hg_block):
    nb, s, d = x.shape
    n_t = s // tile
    n_chunks = tile // chunk
    n_blk = tile // hg_block
    assert carry or hg_block == chunk
    keys = WINDOW + chunk
    key_pad = 2 * LANES
    n_hg = d // HG_DK
    n_streams = nb if carry else s // chunk
    n_off = chunk + key_pad - 1
    offsets = jnp.arange(n_off, dtype=jnp.int32) - (chunk - 1) - WINDOW
    by_offset = table[_t5_bucket(offsets)].astype(jnp.float32).T
    period = jnp.roll(jnp.pad(by_offset, ((0, 0), (0, 1))), -(chunk - 1), axis=1)
    skewed = jnp.tile(period, (1, chunk))[:, :chunk * n_off].reshape(N_HEADS, chunk, n_off)
    rel_bias = skewed[:, :, :key_pad]
    lmask = jnp.asarray(_level_masks(hg_block))
    hmask = jnp.asarray(_row_masks(hg_block))
    smem = pl.BlockSpec(memory_space=pltpu.SMEM)

    in_arrays = [x, norm_w.reshape(1, d), w_in, w_out, sinks, rel_bias, lb_params,
                 gnorm_w.reshape(1, HG_DK), lmask, hmask]
    if carry:
        w_specs = [_weight_spec(w_in.shape), _weight_spec(w_out.shape)]
    else:
        assert nb == 1 and n_t == 1
        w_specs = [pl.BlockSpec(memory_space=pl.ANY)] * 2
    in_specs = [pl.BlockSpec((1, tile, d), lambda b, t: (b, t, 0)), _full_spec((1, d)),
                *w_specs, smem, _full_spec(rel_bias.shape),
                _full_spec(lb_params.shape), _full_spec((1, HG_DK)),
                _full_spec(lmask.shape), _full_spec(hmask.shape)]
    if carry:
        kv_rows = WINDOW
        st_block = (1, n_hg, HG_DK, HG_DK)
        n_win, win_rows, n_var = 1, WINDOW + tile, WINDOW // chunk + 1
    else:
        in_arrays += list(hist)
        in_specs += [_full_spec(a.shape) for a in hist]
        kv_rows = s
        st_block = (n_streams, n_hg, HG_DK, HG_DK)
        n_win, win_rows, n_var = n_chunks, keys, 1
    out_shape = [jax.ShapeDtypeStruct((nb, s, d), jnp.float32),
                 jax.ShapeDtypeStruct((nb, kv_rows, KV_WIDTH), jnp.float32),
                 jax.ShapeDtypeStruct((nb, kv_rows, KV_WIDTH), jnp.float32),
                 jax.ShapeDtypeStruct((n_streams, n_hg, HG_DK, HG_DK), jnp.float32)]
    out_specs = [pl.BlockSpec((1, tile, d), lambda b, t: (b, t, 0)),
                 pl.BlockSpec((1, kv_rows, KV_WIDTH), lambda b, t: (b, 0, 0)),
                 pl.BlockSpec((1, kv_rows, KV_WIDTH), lambda b, t: (b, 0, 0)),
                 pl.BlockSpec(st_block, lambda b, t: (b, 0, 0, 0))]
    f32, bf16 = jnp.float32, jnp.bfloat16
    rows_g = (N_HEADS // 2 // N_KV_HEADS) * chunk
    scratch = [pltpu.VMEM((n_var, 4, rows_g, key_pad), f32),
               pltpu.VMEM((n_chunks, N_HEADS // 2, chunk, LANES), bf16),
               pltpu.VMEM((tile, 2 * KV_WIDTH), f32),
               pltpu.VMEM((4, n_win, win_rows, LANES), bf16),
               pltpu.VMEM((4, n_win, win_rows, LANES), bf16),
               pltpu.VMEM((tile, d), f32), pltpu.VMEM((tile, d), f32),
               pltpu.VMEM((tile, d), bf16)]
    scratch += [pltpu.VMEM((tile, d), f32) for _ in range(6)]
    scratch += [pltpu.VMEM((n_blk, n_hg, HG_DK, HG_DK), f32),
                pltpu.VMEM((n_blk, n_hg, HG_DK, HG_DK), bf16),
                pltpu.VMEM((n_chunks, 4, rows_g, key_pad), bf16),
                pltpu.VMEM((n_blk, n_hg, hg_block, hg_block), bf16),
                pltpu.VMEM((tile, d), bf16)]
    if carry:
        scratch += [pltpu.VMEM((n_hg, HG_DK, HG_DK), f32)]
    else:
        scratch += [pltpu.VMEM(w_in.shape, w_in.dtype), pltpu.VMEM(w_out.shape, w_out.dtype),
                    pltpu.SemaphoreType.DMA((N_IN_SLABS + 1,))]
    kern = functools.partial(_mixer_kernel, tile=tile, chunk=chunk, hg_block=hg_block, carry=carry, d_model=d,
                             layer=layer)
    return pl.pallas_call(
        kern,
        grid=(nb, n_t),
        in_specs=in_specs,
        out_specs=out_specs,
        out_shape=out_shape,
        scratch_shapes=scratch,
        compiler_params=pltpu.CompilerParams(
            dimension_semantics=("arbitrary", "arbitrary"), vmem_limit_bytes=VMEM_LIMIT_BYTES),
        name="mixer_prompt" if carry else "mixer_sample",
    )(*in_arrays)


def _ffn_kernel(xa_ref, xb_ref, nw_ref, wgu_hbm, wd_hbm, fw_ref, ya_ref, yb_ref, act_s, wgu_ref, wd_ref, sems, *,
                d_ff, col_tile, final_norm):
    step = pl.program_id(0)
    n_a = pl.num_programs(0) - 1
    n_cols = d_ff // col_tile

    def slab_copy(j, half):
        cols = slice(half * d_ff + j * col_tile, half * d_ff + (j + 1) * col_tile)
        return pltpu.make_async_copy(wgu_hbm.at[:, cols], wgu_ref.at[:, cols], sems.at[2 * j + half])

    def down_copy():
        return pltpu.make_async_copy(wd_hbm, wd_ref, sems.at[2 * n_cols])

    def ffn(x_ref, y_ref, weights_arriving):
        rows = x_ref.shape[0]
        x = x_ref[...]
        h = _bf(x * nw_ref[...])
        r = lax.rsqrt(jnp.mean(x * x, axis=-1, keepdims=True) + RMS_EPS)
        for j in range(n_cols):
            if weights_arriving:
                if j + SLABS_AHEAD < n_cols:
                    slab_copy(j + SLABS_AHEAD, 0).start()
                    slab_copy(j + SLABS_AHEAD, 1).start()
                elif j + SLABS_AHEAD == n_cols:
                    down_copy().start()
                slab_copy(j, 0).wait()
                slab_copy(j, 1).wait()
            g = r * _dot(h, _bf(wgu_ref[:, j * col_tile:(j + 1) * col_tile]))
            u = r * _dot(h, _bf(wgu_ref[:, d_ff + j * col_tile:d_ff + (j + 1) * col_tile]))
            act_s[0:rows, j * col_tile:(j + 1) * col_tile] = _bf(jax.nn.silu(g) * u)
        if weights_arriving:
            down_copy().wait()
        y = x + _dot(act_s[0:rows, :], _bf(wd_ref[...]))
        if final_norm:
            y = _rms(y, fw_ref[...])
        y_ref[...] = y

    @pl.when(step == 0)
    def _first_tile():
        assert n_cols >= SLABS_AHEAD
        for j in range(SLABS_AHEAD):
            slab_copy(j, 0).start()
            slab_copy(j, 1).start()
        ffn(xa_ref, ya_ref, True)

    @pl.when((step > 0) & (step < n_a))
    def _first_set():
        ffn(xa_ref, ya_ref, False)

    @pl.when(step == n_a)
    def _second_set():
        ffn(xb_ref, yb_ref, False)


def _ffn(xa, xb, norm_w, w_gate_up, w_down, final_w, *, tile, final_norm):
    n, d = xa.shape
    m = xb.shape[0]
    n_a = n // tile
    assert n == n_a * tile and n_a >= 1 and m <= tile
    d_ff = w_down.shape[0]
    col_tile = 2 * LANES
    kern = functools.partial(_ffn_kernel, d_ff=d_ff, col_tile=col_tile, final_norm=final_norm)
    a_spec = pl.BlockSpec((tile, d), lambda i: (jnp.minimum(i, n_a - 1), 0))
    in_hbm = pl.BlockSpec(memory_space=pl.ANY)
    return pl.pallas_call(
        kern,
        grid=(n_a + 1,),
        in_specs=[a_spec, _full_spec((m, d)), _full_spec((1, d)), in_hbm, in_hbm, _full_spec((1, d))],
        out_specs=[a_spec, _full_spec((m, d))],
        out_shape=[jax.ShapeDtypeStruct((n, d), jnp.float32), jax.ShapeDtypeStruct((m, d), jnp.float32)],
        scratch_shapes=[pltpu.VMEM((tile, d_ff), jnp.bfloat16),
                        pltpu.VMEM(w_gate_up.shape, w_gate_up.dtype), pltpu.VMEM(w_down.shape, w_down.dtype),
                        pltpu.SemaphoreType.DMA((2 * (d_ff // col_tile) + 1,))],
        compiler_params=pltpu.CompilerParams(
            dimension_semantics=("arbitrary",), vmem_limit_bytes=VMEM_LIMIT_BYTES),
        name="ffn",
    )(xa, xb, norm_w.reshape(1, d), w_gate_up, w_down, final_w.reshape(1, d))


def kernel(x_prompt, x_sample, cache_k, cache_v, state_hgrn, norm_mix, w_in, w_out, attn_sinks, rel_bias_table,
           hgrn_lb, hgrn_norm, norm_ffn, w_gate_up, w_down, norm_final):
    depth = w_in.shape[0]
    batch, seq, d = x_prompt.shape
    dec_batch, dec_seq, _ = x_sample.shape
    assert w_in.shape[2] == N_HEADS * HEAD_DIM + 2 * KV_WIDTH + 6 * d and KV_WIDTH == LANES and d % (4 * LANES) == 0
    assert seq % PROMPT_TILE == 0 and (batch * seq) % FFN_TILE == 0 and dec_batch * dec_seq <= FFN_TILE
    assert dec_seq % (2 * SUBLANES) == 0 and cache_k.shape[2] == WINDOW
    xp = x_prompt
    xs = x_sample.reshape(1, dec_batch * dec_seq, d)
    outs = [[] for _ in range(6)]
    for l in range(depth):
        w_in_l, w_out_l = _bf(w_in[l]), _bf(w_out[l])
        shared = (norm_mix[l], w_in_l, w_out_l, attn_sinks[l], rel_bias_table, hgrn_lb, hgrn_norm[l])
        xp, kp, vp, sp = _mixer(xp, None, *shared, layer=l, carry=True, tile=PROMPT_TILE, chunk=CHUNK,
                                hg_block=HGRN_BLOCK)
        hist = (cache_k[l].reshape(dec_batch, WINDOW, KV_WIDTH), cache_v[l].reshape(dec_batch, WINDOW, KV_WIDTH),
                state_hgrn[l])
        xs, ks, vs, ss = _mixer(xs, hist, *shared, layer=l, carry=False, tile=dec_batch * dec_seq, chunk=dec_seq,
                                hg_block=dec_seq)
        last = l == depth - 1
        xp, xs = _ffn(xp.reshape(batch * seq, d), xs.reshape(dec_batch * dec_seq, d), norm_ffn[l], w_gate_up[l],
                      w_down[l], norm_final, tile=FFN_TILE, final_norm=last)
        xp = xp.reshape(batch, seq, d)
        xs = xs.reshape(1, dec_batch * dec_seq, d)
        ks = jnp.concatenate([hist[0][:, dec_seq:], ks.reshape(dec_batch, dec_seq, KV_WIDTH)], axis=1)
        vs = jnp.concatenate([hist[1][:, dec_seq:], vs.reshape(dec_batch, dec_seq, KV_WIDTH)], axis=1)
        kv_shape = (-1, WINDOW, N_KV_HEADS, HEAD_DIM)
        for acc, val in zip(outs, (kp.reshape(kv_shape), vp.reshape(kv_shape), sp,
                                   ks.reshape(kv_shape), vs.reshape(kv_shape), ss)):
            acc.append(val)
    return (xp, xs.reshape(dec_batch, dec_seq, d)) + tuple(jnp.stack(o) for o in outs)
```
